```python
import jax, jax.numpy as jnp
from jax import lax
import numpy as np

D_MODEL = 1024
BATCH = 2
SEQ = 8192
DEPTH = 4

D_MIX = D_MODEL
GLA_HEADS = 4
GLA_DV = D_MIX // 2 // GLA_HEADS
GLA_DK = GLA_DV // 2
GLA_RANK = 16
GLA_GATE_NORM = 16.0
HG_HEADS = 4
HG_DK = 128
HG_DV = D_MIX // 2 // HG_HEADS
CHUNK = 64
D_FF_DENSE = 2816
N_EXPERTS = 8
TOP_K = 2
D_FF_EXPERT = 3584
PLE_DIM = 256
EPS = 1e-6
HEAD_EPS = 1e-5
F_MIN = 1e-6

GLA_QK = GLA_HEADS * GLA_DK
GLA_V = GLA_HEADS * GLA_DV
HG_K = HG_HEADS * HG_DK
HG_V = HG_HEADS * HG_DV
IN_SPLIT_SIZES = (GLA_QK, GLA_QK, GLA_V, GLA_V, GLA_RANK, GLA_RANK,
                  HG_K, HG_K, HG_K, HG_V, HG_V)
D_IN_PROJ = sum(IN_SPLIT_SIZES)

kernel_name = "hybrid_gla_hgrn2_moe_bidir_encoder"


def rmsnorm(x, w, eps=EPS):
    xf = x.astype(jnp.float32)
    y = xf * lax.rsqrt(jnp.mean(xf * xf, axis=-1, keepdims=True) + eps)
    return (y * w.astype(jnp.float32)).astype(x.dtype)


def to_heads(a, n_heads):
    b, s, _ = a.shape
    return a.reshape(b, s, n_heads, -1).transpose(0, 2, 1, 3)


def chunk_gla(q, k, v, log_g):
    bn, h, t, dk = q.shape
    dv = v.shape[-1]
    n = t // CHUNK

    def to_chunks(a):
        return jnp.moveaxis(a.reshape(bn, h, n, CHUNK, a.shape[-1]), 2, 0)

    mask = jnp.tril(jnp.ones((CHUNK, CHUNK), dtype=bool))[:, :, None]
    mask_f = mask.astype(jnp.float32)

    def step(state, inp):
        q_c, k_c, v_c, g_c = inp
        b = jnp.cumsum(g_c, axis=-2)
        o_inter = jnp.einsum('bhck,bhkv->bhcv', q_c * jnp.exp(b), state)
        diff = b[:, :, :, None, :] - b[:, :, None, :, :]
        decay = jnp.exp(jnp.where(mask, diff, 0.0)) * mask_f
        scores = jnp.einsum('bhtsk,bhsk->bhts', q_c[:, :, :, None, :] * decay, k_c)
        o = o_inter + jnp.einsum('bhts,bhsv->bhtv', scores, v_c)
        b_last = b[:, :, -1:, :]
        state = (jnp.exp(b_last[:, :, 0, :])[..., None] * state
                 + jnp.einsum('bhsk,bhsv->bhkv', k_c * jnp.exp(b_last - b), v_c))
        return state, o

    s0 = jnp.zeros((bn, h, dk, dv), jnp.float32)
    _, o = lax.scan(step, s0, (to_chunks(q), to_chunks(k), to_chunks(v), to_chunks(log_g)))
    return jnp.moveaxis(o, 0, 2).reshape(bn, h, t, dv)


def bidir_gla(q, k_f, k_b, v, g_f, g_b):
    flip = lambda a: jnp.flip(a, axis=2)
    bsz = q.shape[0]
    o = chunk_gla(jnp.concatenate([q, flip(q)], 0),
                  jnp.concatenate([k_f, flip(k_b)], 0),
                  jnp.concatenate([v, flip(v)], 0),
                  jnp.concatenate([g_f, flip(g_b)], 0))
    return o[:bsz] + flip(o[bsz:])


def gated_head_norm(o, gate, w):
    o = o * lax.rsqrt(jnp.mean(o * o, axis=-1, keepdims=True) + HEAD_EPS) * w.astype(jnp.float32)
    b, h, s, dv = o.shape
    o = o.transpose(0, 2, 1, 3).reshape(b, s, h * dv)
    return o * jax.nn.silu(gate)


def hgrn_forget(z, lb):
    f = lb + (1.0 - lb) * jax.nn.sigmoid(z)
    log_f = jnp.log(jnp.maximum(f, F_MIN))
    k = (1.0 - lb) * jax.nn.sigmoid(-z)
    return log_f, k


def swiglu(v, wg, wu, wd):
    return (jax.nn.silu(v @ wg) * (v @ wu)) @ wd


def moe_swiglu(v, w_router, wg, wu, wd):
    logits = (v @ w_router).astype(jnp.float32)
    probs = jax.nn.softmax(logits, axis=-1)
    top_w, top_i = lax.top_k(probs, TOP_K)
    top_w = top_w / jnp.sum(top_w, axis=-1, keepdims=True)
    gates = jnp.sum(jax.nn.one_hot(top_i, N_EXPERTS, dtype=jnp.float32) * top_w[..., None], axis=-2)
    y = jnp.zeros(v.shape, jnp.float32)
    for e in range(N_EXPERTS):
        y = y + gates[..., e:e + 1] * swiglu(v, wg[e], wu[e], wd[e]).astype(jnp.float32)
    return y.astype(v.dtype)


def setup_inputs(seed: int = 0) -> dict:
    key = jax.random.key(seed)
    ks = jax.random.split(key, 24)
    f32 = jnp.float32

    def nrm(k, shape, fan_in):
        return jax.random.normal(k, shape, f32) * (fan_in ** -0.5)

    def gain(k, shape):
        return 1.0 + 0.05 * jax.random.normal(k, shape, f32)

    n_dense = (DEPTH + 1) // 2
    n_moe = DEPTH // 2
    return {
        "x": jax.random.normal(ks[0], (BATCH, SEQ, D_MODEL), f32),
        "p": jax.random.normal(ks[1], (DEPTH, BATCH, SEQ, PLE_DIM), f32),
        "w_in": nrm(ks[2], (DEPTH, D_MODEL, D_IN_PROJ), D_MODEL),
        "gla_wg2": nrm(ks[3], (DEPTH, 2, GLA_RANK, GLA_QK), GLA_RANK),
        "gla_bg": 0.1 * jax.random.normal(ks[4], (DEPTH, 2, GLA_QK), f32),
        "hg_lb_logits": jax.random.normal(ks[5], (DEPTH, 2, HG_K), f32),
        "gla_onorm": gain(ks[6], (DEPTH, GLA_DV)),
        "hg_onorm": gain(ks[7], (DEPTH, HG_DV)),
        "w_out": nrm(ks[8], (DEPTH, D_MIX, D_MODEL), D_MIX),
        "norm_mix": gain(ks[9], (DEPTH, D_MODEL)),
        "norm_ffn": gain(ks[10], (DEPTH, D_MODEL)),
        "w_dense_gate": nrm(ks[11], (n_dense, D_MODEL, D_FF_DENSE), D_MODEL),
        "w_dense_up": nrm(ks[12], (n_dense, D_MODEL, D_FF_DENSE), D_MODEL),
        "w_dense_down": nrm(ks[13], (n_dense, D_FF_DENSE, D_MODEL), D_FF_DENSE),
        "w_router": nrm(ks[14], (n_moe, D_MODEL, N_EXPERTS), D_MODEL),
        "w_exp_gate": nrm(ks[15], (n_moe, N_EXPERTS, D_MODEL, D_FF_EXPERT), D_MODEL),
        "w_exp_up": nrm(ks[16], (n_moe, N_EXPERTS, D_MODEL, D_FF_EXPERT), D_MODEL),
        "w_exp_down": nrm(ks[17], (n_moe, N_EXPERTS, D_FF_EXPERT, D_MODEL), D_FF_EXPERT),
        "w_ple_proj": nrm(ks[18], (DEPTH, PLE_DIM, D_MODEL), PLE_DIM),
        "w_ple_gate": nrm(ks[19], (DEPTH, D_MODEL, D_MODEL), D_MODEL),
        "norm_ple": gain(ks[20], (DEPTH, D_MODEL)),
        "norm_final": gain(ks[21], (D_MODEL,)),
    }


def reference(x, p, w_in, gla_wg2, gla_bg, hg_lb_logits, gla_onorm, hg_onorm, w_out,
              norm_mix, norm_ffn, w_dense_gate, w_dense_up, w_dense_down, w_router,
              w_exp_gate, w_exp_up, w_exp_down, w_ple_proj, w_ple_gate, norm_ple,
              norm_final):
    f32 = jnp.float32
    offsets = np.cumsum(IN_SPLIT_SIZES)[:-1].tolist()
    lb_p = jax.nn.softmax(hg_lb_logits.astype(f32), axis=0)
    lb_all = jnp.cumsum(lb_p, axis=0) - lb_p[0:1]

    h = x
    for i in range(DEPTH):
        u = rmsnorm(h, norm_mix[i])
        proj = (u @ w_in[i]).astype(f32)
        (gq, gk, gv, gr, glr_f, glr_b,
         hq, hf_f, hf_b, hv, hr) = jnp.split(proj, offsets, axis=-1)

        wg2 = gla_wg2[i].astype(f32)
        bg = gla_bg[i].astype(f32)
        g_f = jax.nn.log_sigmoid(glr_f @ wg2[0] + bg[0]) / GLA_GATE_NORM
        g_b = jax.nn.log_sigmoid(glr_b @ wg2[1] + bg[1]) / GLA_GATE_NORM
        k_gla = to_heads(gk, GLA_HEADS)
        o_gla = bidir_gla(to_heads(gq, GLA_HEADS) * (GLA_DK ** -0.5), k_gla, k_gla,
                          to_heads(gv, GLA_HEADS),
                          to_heads(g_f, GLA_HEADS), to_heads(g_b, GLA_HEADS))
        y_gla = gated_head_norm(o_gla, gr, gla_onorm[i])

        lb = lb_all[i]
        log_f_f, k_f = hgrn_forget(hf_f, lb[0])
        log_f_b, k_b = hgrn_forget(hf_b, lb[1])
        o_hg = bidir_gla(to_heads(jax.nn.silu(hq), HG_HEADS) * (HG_DK ** -0.5),
                         to_heads(k_f, HG_HEADS), to_heads(k_b, HG_HEADS),
                         to_heads(hv, HG_HEADS),
                         to_heads(log_f_f, HG_HEADS), to_heads(log_f_b, HG_HEADS))
        y_hg = gated_head_norm(o_hg, hr, hg_onorm[i])

        mixed = jnp.concatenate([y_gla, y_hg], axis=-1).astype(h.dtype) @ w_out[i]
        h = h + mixed

        v = rmsnorm(h, norm_ffn[i])
        j = i // 2
        if i % 2 == 0:
            ffn = swiglu(v, w_dense_gate[j], w_dense_up[j], w_dense_down[j])
        else:
            ffn = moe_swiglu(v, w_router[j], w_exp_gate[j], w_exp_up[j], w_exp_down[j])
        h = h + ffn

        gate = jax.nn.sigmoid(rmsnorm(h, norm_ple[i]) @ w_ple_gate[i])
        h = h + (p[i] @ w_ple_proj[i]) * gate

    return rmsnorm(h, norm_final)
```

```python
import functools

import jax
import jax.numpy as jnp
from jax import lax
from jax.experimental import pallas as pl
from jax.experimental.pallas import tpu as pltpu

F32 = jnp.float32
BF16 = jnp.bfloat16

D_MODEL = 1024
DEPTH = 4
N_HEADS = 8
HEAD_W = 128
GLA_DK = 64
GLA_RANK = 16
GLA_GATE_NORM = 16.0
HG_DK = 128
HG_K = 512
D_FF_DENSE = 2816
N_EXPERTS = 8
D_FF_EXPERT = 3584
PLE_DIM = 256
EPS = 1e-6
HEAD_EPS = 1e-5
F_MIN = 1e-6

CHUNK = 64
SUB = 16
EXP_CLAMP = 60.0
LANES = 128
VMEM_LIMIT = 56 * 1024 * 1024

MIX_TILE = 256
ROW_TILE = 512
FF_CHUNK = 256
EXP_FF_CHUNK = 512

COL_Q, COL_K, COL_V = 0, 1024, 2048
COL_GATE_F, COL_GLR_F = 3072, 4096
COL_GLR_B = 3072
NF_COLS, NB_COLS = 4224, 3200


def _dot(a, b):
    return jnp.dot(a, b, preferred_element_type=F32)


def _dot_nt(a, b):
    return lax.dot_general(a, b, (((1,), (1,)), ((), ())), preferred_element_type=F32)


def _dot_tn(a, b):
    return lax.dot_general(a, b, (((0,), (0,)), ((), ())), preferred_element_type=F32)


def _sigmoid(x):
    return 1.0 / (1.0 + jnp.exp(-x))


def _rmsnorm(x, w):
    ms = jnp.mean(x * x, axis=-1, keepdims=True)
    return x * lax.rsqrt(ms + EPS) * w


def _const_spec(shape):
    nd = len(shape)
    return pl.BlockSpec(shape, lambda *_: (0,) * nd, pipeline_mode=pl.Buffered(1))


def _params(n_grid):
    return pltpu.CompilerParams(
        dimension_semantics=("arbitrary",) * n_grid, vmem_limit_bytes=VMEM_LIMIT)


def _scan_unit(q, k, v, cum, st_ref, rev, same32, diag_mask):
    zeros = lambda n: jnp.zeros((n, HEAD_W), BF16)

    def level(half):
        q_parts, k_parts = [], []
        for a in range(0, CHUNK, 2 * half):
            lo, hi = slice(a, a + half), slice(a + half, a + 2 * half)
            if not rev:
                ref = cum[a + half - 1:a + half]
                k_rows, q_rows = lo, hi
            else:
                ref = cum[a + half:a + half + 1]
                q_rows, k_rows = lo, hi
            qp = (q[q_rows] * jnp.exp(cum[q_rows] - ref)).astype(BF16)
            kp = (k[k_rows] * jnp.exp(ref - cum[k_rows])).astype(BF16)
            if not rev:
                q_parts += [zeros(half), qp]
                k_parts += [kp, zeros(half)]
            else:
                q_parts += [qp, zeros(half)]
                k_parts += [zeros(half), kp]
        return _dot_nt(jnp.concatenate(q_parts, 0), jnp.concatenate(k_parts, 0))

    a_mat = level(2 * SUB)
    a_mat = a_mat + jnp.where(same32, level(SUB), 0.0)

    mids = []
    for a in range(0, CHUNK, SUB):
        m = 0.5 * (cum[a:a + 1] + cum[a + SUB - 1:a + SUB])
        mids.append(jnp.broadcast_to(m, (SUB, HEAD_W)))
    dd = cum - jnp.concatenate(mids, 0)
    qd = (q * jnp.exp(jnp.clip(dd, -EXP_CLAMP, EXP_CLAMP))).astype(BF16)
    kd = (k * jnp.exp(jnp.clip(-dd, -EXP_CLAMP, EXP_CLAMP))).astype(BF16)
    a_mat = a_mat + jnp.where(diag_mask, _dot_nt(qd, kd), 0.0)

    tot = cum[0:1] if rev else cum[CHUNK - 1:CHUNK]
    st = st_ref[...]
    vb = v.astype(BF16)
    o = _dot_nt((q * jnp.exp(cum)).astype(BF16), st.astype(BF16))
    o = o + _dot(a_mat.astype(BF16), vb)
    ks = (k * jnp.exp(tot - cum)).astype(BF16)
    st_ref[...] = st * jnp.exp(tot) + _dot_tn(vb, ks)
    return o


def _mix_kernel(layer, hf_ref, hb_ref, nw_ref, wf_ref, wb_ref, wg2_ref, bg_ref, lbl_ref,
                of_ref, ob_ref, gate_ref, pf_ref, pb_ref, gf_ref, gb_ref, sf_ref, sb_ref):
    j = pl.program_id(1)
    n_chunks = MIX_TILE // CHUNK

    @pl.when(j == 0)
    def _():
        sf_ref[...] = jnp.zeros_like(sf_ref)
        sb_ref[...] = jnp.zeros_like(sb_ref)

    nw = nw_ref[...]
    pf_ref[...] = _dot(_rmsnorm(hf_ref[0], nw).astype(BF16), wf_ref[...])
    pb_ref[...] = _dot(_rmsnorm(hb_ref[0], nw).astype(BF16), wb_ref[...])
    gate_ref[0] = pf_ref[:, COL_GATE_F:COL_GATE_F + 1024].astype(BF16)

    def prep(direction, p_ref, g_ref, col_glr):
        p_ref[:, COL_Q:COL_Q + 512] = p_ref[:, COL_Q:COL_Q + 512] * (GLA_DK ** -0.5)
        x = _dot(p_ref[:, col_glr:col_glr + LANES].astype(BF16), wg2_ref[direction].astype(BF16))
        x = x + bg_ref[direction]
        log_sig = jnp.minimum(x, 0.0) - jnp.log(1.0 + jnp.exp(-jnp.abs(x)))
        g_ref[:, 0:512] = log_sig * (1.0 / GLA_GATE_NORM)
        hq = p_ref[:, COL_Q + 512:COL_Q + 1024]
        p_ref[:, COL_Q + 512:COL_Q + 1024] = hq * _sigmoid(hq) * (HG_DK ** -0.5)
        rows = [lbl_ref[2 * d + direction:2 * d + direction + 1, :] for d in range(DEPTH)]
        mx = functools.reduce(jnp.maximum, rows)
        ex = [jnp.exp(r - mx) for r in rows]
        lb = sum(ex[1:layer + 1], jnp.zeros_like(mx)) / sum(ex)
        z = p_ref[:, COL_K + 512:COL_K + 1024]
        e = jnp.exp(-jnp.abs(z))
        s_big = 1.0 / (1.0 + e)
        s_small = e * s_big
        sig = jnp.where(z >= 0, s_big, s_small)
        nsig = jnp.where(z >= 0, s_small, s_big)
        f = lb + (1.0 - lb) * sig
        g_ref[:, 512:1024] = jnp.log(jnp.maximum(f, F_MIN))
        p_ref[:, COL_K + 512:COL_K + 1024] = (1.0 - lb) * nsig

    prep(0, pf_ref, gf_ref, COL_GLR_F)
    prep(1, pb_ref, gb_ref, COL_GLR_B)

    ri = lax.broadcasted_iota(jnp.int32, (CHUNK, CHUNK), 0)
    ci = lax.broadcasted_iota(jnp.int32, (CHUNK, CHUNK), 1)
    tri_f = (ci <= ri).astype(BF16)
    tri_b = (ci >= ri).astype(BF16)
    same32 = (ri // (2 * SUB)) == (ci // (2 * SUB))
    same16 = (ri // SUB) == (ci // SUB)
    diag_f = same16 & (ci <= ri)
    diag_b = same16 & (ci >= ri)

    def chunk_body(c, carry):
        for rev, p_ref, g_ref, st_ref, o_ref, tri, diag in (
                (False, pf_ref, gf_ref, sf_ref, of_ref, tri_f, diag_f),
                (True, pb_ref, gb_ref, sb_ref, ob_ref, tri_b, diag_b)):
            cc = (n_chunks - 1 - c) if rev else c
            r0 = pl.multiple_of(cc * CHUNK, CHUNK)
            g = g_ref[pl.ds(r0, CHUNK), :]
            g_hi = g.astype(BF16)
            g_lo = (g - g_hi.astype(F32)).astype(BF16)
            cum = _dot(tri, g_hi) + _dot(tri, g_lo)
            for n in range(N_HEADS):
                c0 = n * HEAD_W
                q = p_ref[pl.ds(r0, CHUNK), COL_Q + c0:COL_Q + c0 + HEAD_W]
                k = p_ref[pl.ds(r0, CHUNK), COL_K + c0:COL_K + c0 + HEAD_W]
                v = p_ref[pl.ds(r0, CHUNK), COL_V + c0:COL_V + c0 + HEAD_W]
                o = _scan_unit(q, k, v, cum[:, c0:c0 + HEAD_W], st_ref.at[n], rev, same32, diag)
                o_ref[0, pl.ds(r0, CHUNK), c0:c0 + HEAD_W] = o
        return carry

    lax.fori_loop(0, n_chunks, chunk_body, 0)


def _pad_heads(w, n_heads, width):
    lead = w.shape[:-1]
    w = w.reshape(lead + (n_heads, width))
    w = jnp.pad(w, [(0, 0)] * len(lead) + [(0, 0), (0, HEAD_W - width)])
    return w.reshape(lead + (n_heads * HEAD_W,))


def _mix(layer, h, norm_w, w_in, wg2, bg, lb_logits):
    bsz, seq, _ = h.shape
    nt = seq // MIX_TILE
    gq, gk, gv, gr, glr_f, glr_b, hq, hf_f, hf_b, hv, hr = jnp.split(
        w_in, [256, 512, 1024, 1536, 1552, 1568, 2080, 2592, 3104, 3616], axis=-1)
    gq_p, gk_p = _pad_heads(gq, 4, GLA_DK), _pad_heads(gk, 4, GLA_DK)
    pad_r = lambda w: jnp.pad(w, ((0, 0), (0, LANES - GLA_RANK)))
    wf = jnp.concatenate([gq_p, hq, gk_p, hf_f, gv, hv, gr, hr, pad_r(glr_f)], -1).astype(BF16)
    wb = jnp.concatenate([gq_p, hq, gk_p, hf_b, gv, hv, pad_r(glr_b)], -1).astype(BF16)
    wg2_p = jnp.pad(_pad_heads(wg2, 4, GLA_DK), ((0, 0), (0, LANES - GLA_RANK), (0, 0)))
    bg_p = _pad_heads(bg, 4, GLA_DK).reshape(2, 1, 512)
    lbl = lb_logits.reshape(DEPTH * 2, HG_K)

    tile = lambda idx: pl.BlockSpec((1, MIX_TILE, D_MODEL), idx)
    fwd = lambda b, j: (b, j, 0)
    bwd = lambda b, j: (b, nt - 1 - j, 0)
    return pl.pallas_call(
        functools.partial(_mix_kernel, layer),
        grid=(bsz, nt),
        in_specs=[tile(fwd), tile(bwd),
                  _const_spec((1, D_MODEL)),
                  _const_spec((D_MODEL, NF_COLS)), _const_spec((D_MODEL, NB_COLS)),
                  _const_spec((2, LANES, 512)), _const_spec((2, 1, 512)),
                  _const_spec((DEPTH * 2, HG_K))],
        out_specs=[tile(fwd), tile(bwd), tile(fwd)],
        out_shape=[jax.ShapeDtypeStruct((bsz, seq, D_MODEL), F32),
                   jax.ShapeDtypeStruct((bsz, seq, D_MODEL), F32),
                   jax.ShapeDtypeStruct((bsz, seq, D_MODEL), BF16)],
        scratch_shapes=[pltpu.VMEM((MIX_TILE, NF_COLS), F32),
                        pltpu.VMEM((MIX_TILE, NB_COLS), F32),
                        pltpu.VMEM((MIX_TILE, D_MODEL), F32),
                        pltpu.VMEM((MIX_TILE, D_MODEL), F32),
                        pltpu.VMEM((N_HEADS, HEAD_W, HEAD_W), F32),
                        pltpu.VMEM((N_HEADS, HEAD_W, HEAD_W), F32)],
        compiler_params=_params(2),
        name=f"mix_l{layer}",
    )(h, h, norm_w.reshape(1, D_MODEL), wf, wb, wg2_p, bg_p, lbl)


def _post_kernel(of_ref, ob_ref, gate_ref, h_ref, ones_ref, onw_ref, wout_ref, out_ref):
    o = of_ref[...] + ob_ref[...]
    ms = _dot((o * o).astype(BF16), ones_ref[...]) * (1.0 / HEAD_W)
    y = o * lax.rsqrt(ms + HEAD_EPS) * onw_ref[...]
    g = gate_ref[...].astype(F32)
    y = y * (g * _sigmoid(g))
    out_ref[...] = h_ref[...] + _dot(y.astype(BF16), wout_ref[...])


def _post(layer, o_f, o_b, gates, h, gla_onorm, hg_onorm, w_out):
    t = h.shape[0]
    head_id = jnp.arange(D_MODEL) // HEAD_W
    ones = (head_id[:, None] == head_id[None, :]).astype(BF16)
    onw = jnp.concatenate([jnp.tile(gla_onorm, 4), jnp.tile(hg_onorm, 4)]).reshape(1, D_MODEL)
    row = lambda: pl.BlockSpec((ROW_TILE, D_MODEL), lambda i: (i, 0))
    return pl.pallas_call(
        _post_kernel,
        grid=(t // ROW_TILE,),
        in_specs=[row(), row(), row(), row(),
                  _const_spec((D_MODEL, D_MODEL)), _const_spec((1, D_MODEL)),
                  _const_spec((D_MODEL, D_MODEL))],
        out_specs=row(),
        out_shape=jax.ShapeDtypeStruct((t, D_MODEL), F32),
        compiler_params=_params(1),
        name=f"post_l{layer}",
    )(o_f, o_b, gates, h, ones, onw, w_out.astype(BF16))


def _ple_tail(h2, p, nple, wpp_ref, wpg_ref, nfinal, final):
    gate = _sigmoid(_dot(_rmsnorm(h2, nple).astype(BF16), wpg_ref[...]))
    h3 = h2 + _dot(p.astype(BF16), wpp_ref[...]) * gate
    if final:
        h3 = _rmsnorm(h3, nfinal)
    return h3


def _dense_kernel(final, h_ref, p_ref, nffn_ref, wg_ref, wu_ref, wd_ref, nple_ref,
                  wpp_ref, wpg_ref, nfin_ref, out_ref):
    h1 = h_ref[...]
    v = _rmsnorm(h1, nffn_ref[...]).astype(BF16)
    acc = jnp.zeros_like(h1)
    for c in range(0, D_FF_DENSE, FF_CHUNK):
        a = _dot(v, wg_ref[:, c:c + FF_CHUNK])
        b = _dot(v, wu_ref[:, c:c + FF_CHUNK])
        acc = acc + _dot((a * _sigmoid(a) * b).astype(BF16), wd_ref[c:c + FF_CHUNK, :])
    out_ref[...] = _ple_tail(h1 + acc, p_ref[...], nple_ref[...], wpp_ref, wpg_ref,
                             nfin_ref[...], final)


def _dense_ffn(layer, final, h, p, norm_ffn, wg, wu, wd, norm_ple, w_pp, w_pg, norm_final):
    t = h.shape[0]
    vec = lambda w: w.reshape(1, D_MODEL)
    return pl.pallas_call(
        functools.partial(_dense_kernel, final),
        grid=(t // ROW_TILE,),
        in_specs=[pl.BlockSpec((ROW_TILE, D_MODEL), lambda i: (i, 0)),
                  pl.BlockSpec((ROW_TILE, PLE_DIM), lambda i: (i, 0)),
                  _const_spec((1, D_MODEL)),
                  _const_spec((D_MODEL, D_FF_DENSE)), _const_spec((D_MODEL, D_FF_DENSE)),
                  _const_spec((D_FF_DENSE, D_MODEL)),
                  _const_spec((1, D_MODEL)),
                  _const_spec((PLE_DIM, D_MODEL)), _const_spec((D_MODEL, D_MODEL)),
                  _const_spec((1, D_MODEL))],
        out_specs=pl.BlockSpec((ROW_TILE, D_MODEL), lambda i: (i, 0)),
        out_shape=jax.ShapeDtypeStruct((t, D_MODEL), F32),
        compiler_params=_params(1),
        name=f"dense_l{layer}",
    )(h, p, vec(norm_ffn), wg.astype(BF16), wu.astype(BF16), wd.astype(BF16),
      vec(norm_ple), w_pp.astype(BF16), w_pg.astype(BF16), vec(norm_final))


def _router_kernel(h_ref, nffn_ref, wr_ref, v_ref, gates_ref):
    vf = _rmsnorm(h_ref[...], nffn_ref[...])
    v_hi = vf.astype(BF16)
    v_ref[...] = v_hi
    v_lo = (vf - v_hi.astype(F32)).astype(BF16)
    w = wr_ref[...]
    w_hi = w.astype(BF16)
    w_lo = (w - w_hi.astype(F32)).astype(BF16)
    logits = _dot(v_hi, w_hi) + _dot(v_lo, w_hi) + _dot(v_hi, w_lo)
    lane = lax.broadcasted_iota(jnp.int32, logits.shape, 1)
    neg = jnp.float32(-jnp.inf)
    lg = jnp.where(lane < N_EXPERTS, logits, neg)
    m1 = jnp.max(lg, axis=-1, keepdims=True)
    i1 = jnp.min(jnp.where(lg == m1, lane, LANES), axis=-1, keepdims=True)
    first = lane == i1
    lg2 = jnp.where(first, neg, lg)
    m2 = jnp.max(lg2, axis=-1, keepdims=True)
    i2 = jnp.min(jnp.where(lg2 == m2, lane, LANES), axis=-1, keepdims=True)
    second = lane == i2
    e = jnp.exp(m2 - m1)
    g1 = 1.0 / (1.0 + e)
    gates_ref[...] = jnp.where(first, g1, 0.0) + jnp.where(second, e * g1, 0.0)


def _router(layer, h, norm_ffn, w_router):
    t = h.shape[0]
    wr = jnp.pad(w_router, ((0, 0), (0, LANES - N_EXPERTS)))
    return pl.pallas_call(
        _router_kernel,
        grid=(t // ROW_TILE,),
        in_specs=[pl.BlockSpec((ROW_TILE, D_MODEL), lambda i: (i, 0)),
                  _const_spec((1, D_MODEL)), _const_spec((D_MODEL, LANES))],
        out_specs=[pl.BlockSpec((ROW_TILE, D_MODEL), lambda i: (i, 0)),
                   pl.BlockSpec((ROW_TILE, LANES), lambda i: (i, 0))],
        out_shape=[jax.ShapeDtypeStruct((t, D_MODEL), BF16),
                   jax.ShapeDtypeStruct((t, LANES), F32)],
        compiler_params=_params(1),
        name=f"router_l{layer}",
    )(h, norm_ffn.reshape(1, D_MODEL), wr)


def _experts_kernel(x_ref, gates_ref, h_ref, wg_ref, wu_ref, wd_ref, out_ref, acc_ref):
    e = pl.program_id(1)
    f = pl.program_id(2)

    @pl.when((e == 0) & (f == 0))
    def _():
        acc_ref[...] = jnp.zeros_like(acc_ref)

    x = x_ref[...]
    a = _dot(x, wg_ref[0])
    b = _dot(x, wu_ref[0])
    gates = gates_ref[...]
    lane = lax.broadcasted_iota(jnp.int32, gates.shape, 1)
    gcol = jnp.sum(jnp.where(lane == e, gates, 0.0), axis=-1, keepdims=True)
    hid = (a * _sigmoid(a) * b * gcol).astype(BF16)
    acc_ref[...] += _dot(hid, wd_ref[0])

    @pl.when((e == N_EXPERTS - 1) & (f == pl.num_programs(2) - 1))
    def _():
        out_ref[...] = h_ref[...] + acc_ref[...]


def _experts(layer, x, gates, h, wg, wu, wd):
    t = h.shape[0]
    nf = D_FF_EXPERT // EXP_FF_CHUNK
    row = lambda w: pl.BlockSpec((ROW_TILE, w), lambda i, e, f: (i, 0))
    return pl.pallas_call(
        _experts_kernel,
        grid=(t // ROW_TILE, N_EXPERTS, nf),
        in_specs=[row(D_MODEL), row(LANES), row(D_MODEL),
                  pl.BlockSpec((1, D_MODEL, EXP_FF_CHUNK), lambda i, e, f: (e, 0, f)),
                  pl.BlockSpec((1, D_MODEL, EXP_FF_CHUNK), lambda i, e, f: (e, 0, f)),
                  pl.BlockSpec((1, EXP_FF_CHUNK, D_MODEL), lambda i, e, f: (e, f, 0))],
        out_specs=row(D_MODEL),
        out_shape=jax.ShapeDtypeStruct((t, D_MODEL), F32),
        scratch_shapes=[pltpu.VMEM((ROW_TILE, D_MODEL), F32)],
        compiler_params=_params(3),
        name=f"experts_l{layer}",
    )(x, gates, h, wg.astype(BF16), wu.astype(BF16), wd.astype(BF16))


def _ple_kernel(final, h_ref, p_ref, nple_ref, wpp_ref, wpg_ref, nfin_ref, out_ref):
    out_ref[...] = _ple_tail(h_ref[...], p_ref[...], nple_ref[...], wpp_ref, wpg_ref,
                             nfin_ref[...], final)


def _ple(layer, final, h, p, norm_ple, w_pp, w_pg, norm_final):
    t = h.shape[0]
    vec = lambda w: w.reshape(1, D_MODEL)
    return pl.pallas_call(
        functools.partial(_ple_kernel, final),
        grid=(t // ROW_TILE,),
        in_specs=[pl.BlockSpec((ROW_TILE, D_MODEL), lambda i: (i, 0)),
                  pl.BlockSpec((ROW_TILE, PLE_DIM), lambda i: (i, 0)),
                  _const_spec((1, D_MODEL)),
                  _const_spec((PLE_DIM, D_MODEL)), _const_spec((D_MODEL, D_MODEL)),
                  _const_spec((1, D_MODEL))],
        out_specs=pl.BlockSpec((ROW_TILE, D_MODEL), lambda i: (i, 0)),
        out_shape=jax.ShapeDtypeStruct((t, D_MODEL), F32),
        compiler_params=_params(1),
        name=f"ple_l{layer}",
    )(h, p, vec(norm_ple), w_pp.astype(BF16), w_pg.astype(BF16), vec(norm_final))


def kernel(x, p, w_in, gla_wg2, gla_bg, hg_lb_logits, gla_onorm, hg_onorm, w_out, norm_mix,
           norm_ffn, w_dense_gate, w_dense_up, w_dense_down, w_router, w_exp_gate, w_exp_up,
           w_exp_down, w_ple_proj, w_ple_gate, norm_ple, norm_final):
    bsz, seq, _ = x.shape
    t = bsz * seq
    h = x
    for i in range(DEPTH):
        final = i == DEPTH - 1
        o_f, o_b, gates = _mix(i, h, norm_mix[i], w_in[i], gla_wg2[i], gla_bg[i], hg_lb_logits)
        flat = lambda a: a.reshape(t, a.shape[-1])
        h1 = _post(i, flat(o_f), flat(o_b), flat(gates), flat(h), gla_onorm[i], hg_onorm[i],
                   w_out[i])
        p_i = flat(p[i])
        j = i // 2
        if i % 2 == 0:
            h3 = _dense_ffn(i, final, h1, p_i, norm_ffn[i], w_dense_gate[j], w_dense_up[j],
                            w_dense_down[j], norm_ple[i], w_ple_proj[i], w_ple_gate[i],
                            norm_final)
        else:
            v, rg = _router(i, h1, norm_ffn[i], w_router[j])
            h2 = _experts(i, v, rg, h1, w_exp_gate[j], w_exp_up[j], w_exp_down[j])
            h3 = _ple(i, final, h2, p_i, norm_ple[i], w_ple_proj[i], w_ple_gate[i], norm_final)
        h = h3.reshape(bsz, seq, D_MODEL)
    return h
```

```python
import functools

import jax
import jax.numpy as jnp
from jax import lax
from jax.experimental import pallas as pl
from jax.experimental.pallas import tpu as pltpu
from jax.experimental.pallas import tpu_sc as plsc

F32 = jnp.float32
BF16 = jnp.bfloat16

D_MODEL = 1024
DEPTH = 4
N_HEADS = 8
HEAD_W = 128
GLA_DK = 64
GLA_RANK = 16
GLA_GATE_NORM = 16.0
HG_DK = 128
HG_K = 512
D_FF_DENSE = 2816
N_EXPERTS = 8
D_FF_EXPERT = 3584
PLE_DIM = 256
EPS = 1e-6
HEAD_EPS = 1e-5
F_MIN = 1e-6

CHUNK = 64
SUB = 16
EXP_CLAMP = 60.0
LANES = 128
VMEM_LIMIT = 56 * 1024 * 1024

MIX_TILE = 256
ROW_TILE = 512
FF_CHUNK = 256
EXP_FF_CHUNK = 512

COL_Q, COL_K, COL_V = 0, 1024, 2048
COL_GATE_F, COL_GLR_F = 3072, 4096
COL_GLR_B = 3072
NF_COLS, NB_COLS = 4224, 3200


def _dot(a, b):
    return jnp.dot(a, b, preferred_element_type=F32)


def _dot_nt(a, b):
    return lax.dot_general(a, b, (((1,), (1,)), ((), ())), preferred_element_type=F32)


def _dot_tn(a, b):
    return lax.dot_general(a, b, (((0,), (0,)), ((), ())), preferred_element_type=F32)


def _sigmoid(x):
    return 1.0 / (1.0 + jnp.exp(-x))


def _rmsnorm(x, w):
    ms = jnp.mean(x * x, axis=-1, keepdims=True)
    return x * lax.rsqrt(ms + EPS) * w


def _const_spec(shape):
    nd = len(shape)
    return pl.BlockSpec(shape, lambda *_: (0,) * nd, pipeline_mode=pl.Buffered(1))


def _params(n_grid):
    return pltpu.CompilerParams(
        dimension_semantics=("arbitrary",) * n_grid, vmem_limit_bytes=VMEM_LIMIT)


def _scan_unit(q, k, v, cum, st_ref, rev, same32, diag_mask):
    zeros = lambda n: jnp.zeros((n, HEAD_W), BF16)

    def level(half):
        q_parts, k_parts = [], []
        for a in range(0, CHUNK, 2 * half):
            lo, hi = slice(a, a + half), slice(a + half, a + 2 * half)
            if not rev:
                ref = cum[a + half - 1:a + half]
                k_rows, q_rows = lo, hi
            else:
                ref = cum[a + half:a + half + 1]
                q_rows, k_rows = lo, hi
            qp = (q[q_rows] * jnp.exp(cum[q_rows] - ref)).astype(BF16)
            kp = (k[k_rows] * jnp.exp(ref - cum[k_rows])).astype(BF16)
            if not rev:
                q_parts += [zeros(half), qp]
                k_parts += [kp, zeros(half)]
            else:
                q_parts += [qp, zeros(half)]
                k_parts += [zeros(half), kp]
        return _dot_nt(jnp.concatenate(q_parts, 0), jnp.concatenate(k_parts, 0))

    a_mat = level(2 * SUB)
    a_mat = a_mat + jnp.where(same32, level(SUB), 0.0)

    mids = []
    for a in range(0, CHUNK, SUB):
        m = 0.5 * (cum[a:a + 1] + cum[a + SUB - 1:a + SUB])
        mids.append(jnp.broadcast_to(m, (SUB, HEAD_W)))
    dd = cum - jnp.concatenate(mids, 0)
    qd = (q * jnp.exp(jnp.clip(dd, -EXP_CLAMP, EXP_CLAMP))).astype(BF16)
    kd = (k * jnp.exp(jnp.clip(-dd, -EXP_CLAMP, EXP_CLAMP))).astype(BF16)
    a_mat = a_mat + jnp.where(diag_mask, _dot_nt(qd, kd), 0.0)

    tot = cum[0:1] if rev else cum[CHUNK - 1:CHUNK]
    st = st_ref[...]
    vb = v.astype(BF16)
    o = _dot_nt((q * jnp.exp(cum)).astype(BF16), st.astype(BF16))
    o = o + _dot(a_mat.astype(BF16), vb)
    ks = (k * jnp.exp(tot - cum)).astype(BF16)
    st_ref[...] = st * jnp.exp(tot) + _dot_tn(vb, ks)
    return o


def _mix_kernel(layer, hf_ref, hb_ref, nw_ref, wf_ref, wb_ref, wg2_ref, bg_ref, lbl_ref,
                of_ref, ob_ref, gate_ref, pf_ref, pb_ref, gf_ref, gb_ref, sf_ref, sb_ref):
    j = pl.program_id(1)
    n_chunks = MIX_TILE // CHUNK

    @pl.when(j == 0)
    def _():
        sf_ref[...] = jnp.zeros_like(sf_ref)
        sb_ref[...] = jnp.zeros_like(sb_ref)

    nw = nw_ref[...]
    pf_ref[...] = _dot(_rmsnorm(hf_ref[0], nw).astype(BF16), wf_ref[...])
    pb_ref[...] = _dot(_rmsnorm(hb_ref[0], nw).astype(BF16), wb_ref[...])
    gate_ref[0] = pf_ref[:, COL_GATE_F:COL_GATE_F + 1024].astype(BF16)

    def prep(direction, p_ref, g_ref, col_glr):
        p_ref[:, COL_Q:COL_Q + 512] = p_ref[:, COL_Q:COL_Q + 512] * (GLA_DK ** -0.5)
        x = _dot(p_ref[:, col_glr:col_glr + LANES].astype(BF16), wg2_ref[direction].astype(BF16))
        x = x + bg_ref[direction]
        log_sig = jnp.minimum(x, 0.0) - jnp.log(1.0 + jnp.exp(-jnp.abs(x)))
        g_ref[:, 0:512] = log_sig * (1.0 / GLA_GATE_NORM)
        hq = p_ref[:, COL_Q + 512:COL_Q + 1024]
        p_ref[:, COL_Q + 512:COL_Q + 1024] = hq * _sigmoid(hq) * (HG_DK ** -0.5)
        rows = [lbl_ref[2 * d + direction:2 * d + direction + 1, :] for d in range(DEPTH)]
        mx = functools.reduce(jnp.maximum, rows)
        ex = [jnp.exp(r - mx) for r in rows]
        lb = sum(ex[1:layer + 1], jnp.zeros_like(mx)) / sum(ex)
        z = p_ref[:, COL_K + 512:COL_K + 1024]
        e = jnp.exp(-jnp.abs(z))
        s_big = 1.0 / (1.0 + e)
        s_small = e * s_big
        sig = jnp.where(z >= 0, s_big, s_small)
        nsig = jnp.where(z >= 0, s_small, s_big)
        f = lb + (1.0 - lb) * sig
        g_ref[:, 512:1024] = jnp.log(jnp.maximum(f, F_MIN))
        p_ref[:, COL_K + 512:COL_K + 1024] = (1.0 - lb) * nsig

    prep(0, pf_ref, gf_ref, COL_GLR_F)
    prep(1, pb_ref, gb_ref, COL_GLR_B)

    ri = lax.broadcasted_iota(jnp.int32, (CHUNK, CHUNK), 0)
    ci = lax.broadcasted_iota(jnp.int32, (CHUNK, CHUNK), 1)
    tri_f = (ci <= ri).astype(BF16)
    tri_b = (ci >= ri).astype(BF16)
    same32 = (ri // (2 * SUB)) == (ci // (2 * SUB))
    same16 = (ri // SUB) == (ci // SUB)
    diag_f = same16 & (ci <= ri)
    diag_b = same16 & (ci >= ri)

    def chunk_body(c, carry):
        for rev, p_ref, g_ref, st_ref, o_ref, tri, diag in (
                (False, pf_ref, gf_ref, sf_ref, of_ref, tri_f, diag_f),
                (True, pb_ref, gb_ref, sb_ref, ob_ref, tri_b, diag_b)):
            cc = (n_chunks - 1 - c) if rev else c
            r0 = pl.multiple_of(cc * CHUNK, CHUNK)
            g = g_ref[pl.ds(r0, CHUNK), :]
            g_hi = g.astype(BF16)
            g_lo = (g - g_hi.astype(F32)).astype(BF16)
            cum = _dot(tri, g_hi) + _dot(tri, g_lo)
            for n in range(N_HEADS):
                c0 = n * HEAD_W
                q = p_ref[pl.ds(r0, CHUNK), COL_Q + c0:COL_Q + c0 + HEAD_W]
                k = p_ref[pl.ds(r0, CHUNK), COL_K + c0:COL_K + c0 + HEAD_W]
                v = p_ref[pl.ds(r0, CHUNK), COL_V + c0:COL_V + c0 + HEAD_W]
                o = _scan_unit(q, k, v, cum[:, c0:c0 + HEAD_W], st_ref.at[n], rev, same32, diag)
                o_ref[0, pl.ds(r0, CHUNK), c0:c0 + HEAD_W] = o
        return carry

    lax.fori_loop(0, n_chunks, chunk_body, 0)


def _pad_heads(w, n_heads, width):
    lead = w.shape[:-1]
    w = w.reshape(lead + (n_heads, width))
    w = jnp.pad(w, [(0, 0)] * len(lead) + [(0, 0), (0, HEAD_W - width)])
    return w.reshape(lead + (n_heads * HEAD_W,))


def _mix(layer, h, norm_w, w_in, wg2, bg, lb_logits):
    bsz, seq, _ = h.shape
    nt = seq // MIX_TILE
    gq, gk, gv, gr, glr_f, glr_b, hq, hf_f, hf_b, hv, hr = jnp.split(
        w_in, [256, 512, 1024, 1536, 1552, 1568, 2080, 2592, 3104, 3616], axis=-1)
    gq_p, gk_p = _pad_heads(gq, 4, GLA_DK), _pad_heads(gk, 4, GLA_DK)
    pad_r = lambda w: jnp.pad(w, ((0, 0), (0, LANES - GLA_RANK)))
    wf = jnp.concatenate([gq_p, hq, gk_p, hf_f, gv, hv, gr, hr, pad_r(glr_f)], -1).astype(BF16)
    wb = jnp.concatenate([gq_p, hq, gk_p, hf_b, gv, hv, pad_r(glr_b)], -1).astype(BF16)
    wg2_p = jnp.pad(_pad_heads(wg2, 4, GLA_DK), ((0, 0), (0, LANES - GLA_RANK), (0, 0)))
    bg_p = _pad_heads(bg, 4, GLA_DK).reshape(2, 1, 512)
    lbl = lb_logits.reshape(DEPTH * 2, HG_K)

    tile = lambda idx: pl.BlockSpec((1, MIX_TILE, D_MODEL), idx)
    fwd = lambda b, j: (b, j, 0)
    bwd = lambda b, j: (b, nt - 1 - j, 0)
    return pl.pallas_call(
        functools.partial(_mix_kernel, layer),
        grid=(bsz, nt),
        in_specs=[tile(fwd), tile(bwd),
                  _const_spec((1, D_MODEL)),
                  _const_spec((D_MODEL, NF_COLS)), _const_spec((D_MODEL, NB_COLS)),
                  _const_spec((2, LANES, 512)), _const_spec((2, 1, 512)),
                  _const_spec((DEPTH * 2, HG_K))],
        out_specs=[tile(fwd), tile(bwd), tile(fwd)],
        out_shape=[jax.ShapeDtypeStruct((bsz, seq, D_MODEL), F32),
                   jax.ShapeDtypeStruct((bsz, seq, D_MODEL), F32),
                   jax.ShapeDtypeStruct((bsz, seq, D_MODEL), BF16)],
        scratch_shapes=[pltpu.VMEM((MIX_TILE, NF_COLS), F32),
                        pltpu.VMEM((MIX_TILE, NB_COLS), F32),
                        pltpu.VMEM((MIX_TILE, D_MODEL), F32),
                        pltpu.VMEM((MIX_TILE, D_MODEL), F32),
                        pltpu.VMEM((N_HEADS, HEAD_W, HEAD_W), F32),
                        pltpu.VMEM((N_HEADS, HEAD_W, HEAD_W), F32)],
        compiler_params=_params(2),
        name=f"mix_l{layer}",
    )(h, h, norm_w.reshape(1, D_MODEL), wf, wb, wg2_p, bg_p, lbl)


def _post_kernel(of_ref, ob_ref, gate_ref, h_ref, ones_ref, onw_ref, wout_ref, out_ref):
    o = of_ref[...] + ob_ref[...]
    ms = _dot((o * o).astype(BF16), ones_ref[...]) * (1.0 / HEAD_W)
    y = o * lax.rsqrt(ms + HEAD_EPS) * onw_ref[...]
    g = gate_ref[...].astype(F32)
    y = y * (g * _sigmoid(g))
    out_ref[...] = h_ref[...] + _dot(y.astype(BF16), wout_ref[...])


def _post(layer, o_f, o_b, gates, h, gla_onorm, hg_onorm, w_out):
    t = h.shape[0]
    head_id = jnp.arange(D_MODEL) // HEAD_W
    ones = (head_id[:, None] == head_id[None, :]).astype(BF16)
    onw = jnp.concatenate([jnp.tile(gla_onorm, 4), jnp.tile(hg_onorm, 4)]).reshape(1, D_MODEL)
    row = lambda: pl.BlockSpec((ROW_TILE, D_MODEL), lambda i: (i, 0))
    return pl.pallas_call(
        _post_kernel,
        grid=(t // ROW_TILE,),
        in_specs=[row(), row(), row(), row(),
                  _const_spec((D_MODEL, D_MODEL)), _const_spec((1, D_MODEL)),
                  _const_spec((D_MODEL, D_MODEL))],
        out_specs=row(),
        out_shape=jax.ShapeDtypeStruct((t, D_MODEL), F32),
        compiler_params=_params(1),
        name=f"post_l{layer}",
    )(o_f, o_b, gates, h, ones, onw, w_out.astype(BF16))


def _ple_tail(h2, p, nple, wpp_ref, wpg_ref, nfinal, final):
    gate = _sigmoid(_dot(_rmsnorm(h2, nple).astype(BF16), wpg_ref[...]))
    h3 = h2 + _dot(p.astype(BF16), wpp_ref[...]) * gate
    if final:
        h3 = _rmsnorm(h3, nfinal)
    return h3


def _dense_kernel(final, h_ref, p_ref, nffn_ref, wg_ref, wu_ref, wd_ref, nple_ref,
                  wpp_ref, wpg_ref, nfin_ref, out_ref):
    h1 = h_ref[...]
    v = _rmsnorm(h1, nffn_ref[...]).astype(BF16)
    acc = jnp.zeros_like(h1)
    for c in range(0, D_FF_DENSE, FF_CHUNK):
        a = _dot(v, wg_ref[:, c:c + FF_CHUNK])
        b = _dot(v, wu_ref[:, c:c + FF_CHUNK])
        acc = acc + _dot((a * _sigmoid(a) * b).astype(BF16), wd_ref[c:c + FF_CHUNK, :])
    out_ref[...] = _ple_tail(h1 + acc, p_ref[...], nple_ref[...], wpp_ref, wpg_ref,
                             nfin_ref[...], final)


def _dense_ffn(layer, final, h, p, norm_ffn, wg, wu, wd, norm_ple, w_pp, w_pg, norm_final):
    t = h.shape[0]
    vec = lambda w: w.reshape(1, D_MODEL)
    return pl.pallas_call(
        functools.partial(_dense_kernel, final),
        grid=(t // ROW_TILE,),
        in_specs=[pl.BlockSpec((ROW_TILE, D_MODEL), lambda i: (i, 0)),
                  pl.BlockSpec((ROW_TILE, PLE_DIM), lambda i: (i, 0)),
                  _const_spec((1, D_MODEL)),
                  _const_spec((D_MODEL, D_FF_DENSE)), _const_spec((D_MODEL, D_FF_DENSE)),
                  _const_spec((D_FF_DENSE, D_MODEL)),
                  _const_spec((1, D_MODEL)),
                  _const_spec((PLE_DIM, D_MODEL)), _const_spec((D_MODEL, D_MODEL)),
                  _const_spec((1, D_MODEL))],
        out_specs=pl.BlockSpec((ROW_TILE, D_MODEL), lambda i: (i, 0)),
        out_shape=jax.ShapeDtypeStruct((t, D_MODEL), F32),
        compiler_params=_params(1),
        name=f"dense_l{layer}",
    )(h, p, vec(norm_ffn), wg.astype(BF16), wu.astype(BF16), wd.astype(BF16),
      vec(norm_ple), w_pp.astype(BF16), w_pg.astype(BF16), vec(norm_final))


META_G1, META_G2, META_E1, META_E2, META_R1, META_R2 = range(6)
HALF = D_MODEL // 2
HI_MASK = 0xFFFF0000


def _pack_rows(x):
    bits = pltpu.bitcast(x.astype(BF16).astype(F32), jnp.uint32)
    return (bits[:, :HALF] >> 16) | (bits[:, HALF:] & jnp.uint32(HI_MASK))


def _unpack_rows(w):
    lo = pltpu.bitcast(w << 16, F32)
    hi = pltpu.bitcast(w & jnp.uint32(HI_MASK), F32)
    return jnp.concatenate([lo, hi], axis=1)


def _router_kernel(h_ref, nffn_ref, wr_ref, tri_ref, xp_ref, meta_ref, cnt_ref, base_ref):
    @pl.when(pl.program_id(0) == 0)
    def _():
        base_ref[...] = jnp.zeros_like(base_ref)

    vf = _rmsnorm(h_ref[...], nffn_ref[...])
    xp_ref[...] = _pack_rows(vf)
    v_hi = vf.astype(BF16)
    v_lo = (vf - v_hi.astype(F32)).astype(BF16)
    w = wr_ref[...]
    w_hi = w.astype(BF16)
    w_lo = (w - w_hi.astype(F32)).astype(BF16)
    logits = _dot(v_hi, w_hi) + _dot(v_lo, w_hi) + _dot(v_hi, w_lo)
    lane = lax.broadcasted_iota(jnp.int32, logits.shape, 1)
    neg = jnp.float32(-jnp.inf)
    lg = jnp.where(lane < N_EXPERTS, logits, neg)
    m1 = jnp.max(lg, axis=-1, keepdims=True)
    i1 = jnp.min(jnp.where(lg == m1, lane, LANES), axis=-1, keepdims=True)
    first = lane == i1
    lg2 = jnp.where(first, neg, lg)
    m2 = jnp.max(lg2, axis=-1, keepdims=True)
    i2 = jnp.min(jnp.where(lg2 == m2, lane, LANES), axis=-1, keepdims=True)
    second = lane == i2
    e = jnp.exp(m2 - m1)
    g1 = 1.0 / (1.0 + e)
    g2 = e * g1
    cnt = jnp.where(first | second, 1.0, 0.0)
    rank = base_ref[...] + _dot(tri_ref[...], cnt.astype(BF16))
    r1 = jnp.sum(jnp.where(first, rank, 0.0), axis=-1, keepdims=True)
    r2 = jnp.sum(jnp.where(second, rank, 0.0), axis=-1, keepdims=True)
    base_ref[...] = base_ref[...] + jnp.sum(cnt, axis=0, keepdims=True)
    cnt_ref[...] = base_ref[...]
    meta = jnp.zeros(logits.shape, F32)
    for col, val in ((META_G1, g1), (META_G2, g2), (META_E1, i1.astype(F32)),
                     (META_E2, i2.astype(F32)), (META_R1, r1), (META_R2, r2)):
        meta = jnp.where(lane == col, val, meta)
    meta_ref[...] = meta


def _router(layer, h, norm_ffn, w_router):
    t = h.shape[0]
    wr = jnp.pad(w_router, ((0, 0), (0, LANES - N_EXPERTS)))
    ids = jnp.arange(ROW_TILE)
    tri = (ids[None, :] < ids[:, None]).astype(BF16)
    return pl.pallas_call(
        _router_kernel,
        grid=(t // ROW_TILE,),
        in_specs=[pl.BlockSpec((ROW_TILE, D_MODEL), lambda i: (i, 0)),
                  _const_spec((1, D_MODEL)), _const_spec((D_MODEL, LANES)),
                  _const_spec((ROW_TILE, ROW_TILE))],
        out_specs=[pl.BlockSpec((ROW_TILE, HALF), lambda i: (i, 0)),
                   pl.BlockSpec((ROW_TILE, LANES), lambda i: (i, 0)),
                   pl.BlockSpec((1, LANES), lambda i: (0, 0))],
        out_shape=[jax.ShapeDtypeStruct((t, HALF), jnp.uint32),
                   jax.ShapeDtypeStruct((t, LANES), F32),
                   jax.ShapeDtypeStruct((1, LANES), F32)],
        scratch_shapes=[pltpu.VMEM((1, LANES), F32)],
        compiler_params=_params(1),
        name=f"router_l{layer}",
    )(h, norm_ffn.reshape(1, D_MODEL), wr, tri)


SC_CORES, SC_SUBCORES = 2, 16
SC_WORKERS = SC_CORES * SC_SUBCORES
SC_WINDOW = 128


def _sc_gather(table, idx):
    n_out, d = idx.shape[0], table.shape[1]
    per_worker = n_out // SC_WORKERS
    n_win = per_worker // SC_WINDOW
    assert n_win * SC_WINDOW * SC_WORKERS == n_out
    mesh = plsc.VectorSubcoreMesh(core_axis_name="c", subcore_axis_name="s")

    def body(table_hbm, idx_hbm, out_hbm, idx_v, rows_v, sem):
        wid = lax.axis_index("s") * SC_CORES + lax.axis_index("c")

        @pl.loop(0, n_win)
        def _(w):
            base = pl.multiple_of(wid * per_worker + w * SC_WINDOW, SC_WINDOW)
            pltpu.sync_copy(idx_hbm.at[pl.ds(base, SC_WINDOW)], idx_v)
            pltpu.async_copy(table_hbm.at[idx_v], rows_v, sem).wait()
            pltpu.sync_copy(rows_v, out_hbm.at[pl.ds(base, SC_WINDOW)])

    return pl.kernel(
        body, mesh=mesh,
        out_type=jax.ShapeDtypeStruct((n_out, d), table.dtype),
        scratch_types=[pltpu.VMEM((SC_WINDOW,), jnp.int32),
                       pltpu.VMEM((SC_WINDOW, d), table.dtype),
                       pltpu.SemaphoreType.DMA],
    )(table, idx)


EXP_ROWS = 512


def _experts_kernel(be_ref, na_ref, xs_ref, wg_ref, wu_ref, wd_ref, ys_ref, x_ref, acc_ref):
    b = pl.program_id(0)
    f = pl.program_id(1)

    @pl.when(b < na_ref[0])
    def _():
        @pl.when(f == 0)
        def _():
            x_ref[...] = _unpack_rows(xs_ref[...]).astype(BF16)
            acc_ref[...] = jnp.zeros_like(acc_ref)

        x = x_ref[...]
        a = _dot(x, wg_ref[0])
        u = _dot(x, wu_ref[0])
        acc_ref[...] += _dot((a * _sigmoid(a) * u).astype(BF16), wd_ref[0])

        @pl.when(f == pl.num_programs(1) - 1)
        def _():
            ys_ref[...] = _pack_rows(acc_ref[...])


def _experts(layer, xs, blk_expert, n_active, wg, wu, wd):
    rows = xs.shape[0]
    nf = D_FF_EXPERT // EXP_FF_CHUNK
    fsel = lambda b, f, na: jnp.where(b < na[0], f, nf - 1)
    grid_spec = pltpu.PrefetchScalarGridSpec(
        num_scalar_prefetch=2,
        grid=(rows // EXP_ROWS, nf),
        in_specs=[pl.BlockSpec((EXP_ROWS, HALF), lambda b, f, be, na: (b, 0)),
                  pl.BlockSpec((1, D_MODEL, EXP_FF_CHUNK),
                               lambda b, f, be, na: (be[b], 0, fsel(b, f, na))),
                  pl.BlockSpec((1, D_MODEL, EXP_FF_CHUNK),
                               lambda b, f, be, na: (be[b], 0, fsel(b, f, na))),
                  pl.BlockSpec((1, EXP_FF_CHUNK, D_MODEL),
                               lambda b, f, be, na: (be[b], fsel(b, f, na), 0))],
        out_specs=pl.BlockSpec((EXP_ROWS, HALF), lambda b, f, be, na: (b, 0)),
        scratch_shapes=[pltpu.VMEM((EXP_ROWS, D_MODEL), BF16),
                        pltpu.VMEM((EXP_ROWS, D_MODEL), F32)])
    return pl.pallas_call(
        _experts_kernel,
        grid_spec=grid_spec,
        out_shape=jax.ShapeDtypeStruct((rows, HALF), jnp.uint32),
        compiler_params=_params(2),
        name=f"experts_l{layer}",
    )(blk_expert, n_active, xs, wg.astype(BF16), wu.astype(BF16), wd.astype(BF16))


def _combine_kernel(final, h_ref, y1_ref, y2_ref, meta_ref, p_ref, nple_ref, wpp_ref, wpg_ref,
                    nfin_ref, out_ref):
    meta = meta_ref[...]
    g1 = meta[:, META_G1:META_G1 + 1]
    g2 = meta[:, META_G2:META_G2 + 1]
    h2 = h_ref[...] + g1 * _unpack_rows(y1_ref[...]) + g2 * _unpack_rows(y2_ref[...])
    out_ref[...] = _ple_tail(h2, p_ref[...], nple_ref[...], wpp_ref, wpg_ref, nfin_ref[...], final)


def _combine(layer, final, h, yg, meta, p, norm_ple, w_pp, w_pg, norm_final):
    t = h.shape[0]
    nt = t // ROW_TILE
    vec = lambda w: w.reshape(1, D_MODEL)
    return pl.pallas_call(
        functools.partial(_combine_kernel, final),
        grid=(nt,),
        in_specs=[pl.BlockSpec((ROW_TILE, D_MODEL), lambda i: (i, 0)),
                  pl.BlockSpec((ROW_TILE, HALF), lambda i: (i, 0)),
                  pl.BlockSpec((ROW_TILE, HALF), lambda i: (i + nt, 0)),
                  pl.BlockSpec((ROW_TILE, LANES), lambda i: (i, 0)),
                  pl.BlockSpec((ROW_TILE, PLE_DIM), lambda i: (i, 0)),
                  _const_spec((1, D_MODEL)),
                  _const_spec((PLE_DIM, D_MODEL)), _const_spec((D_MODEL, D_MODEL)),
                  _const_spec((1, D_MODEL))],
        out_specs=pl.BlockSpec((ROW_TILE, D_MODEL), lambda i: (i, 0)),
        out_shape=jax.ShapeDtypeStruct((t, D_MODEL), F32),
        compiler_params=_params(1),
        name=f"combine_l{layer}",
    )(h, yg, yg, meta, p, vec(norm_ple), w_pp.astype(BF16), w_pg.astype(BF16), vec(norm_final))


def _moe(layer, final, h1, p, norm_ffn, w_router, wg, wu, wd, norm_ple, w_pp, w_pg, norm_final):
    t = h1.shape[0]
    xp, meta, counts = _router(layer, h1, norm_ffn, w_router)
    cnt = counts[0, :N_EXPERTS].astype(jnp.int32)
    padded = ((cnt + EXP_ROWS - 1) // EXP_ROWS) * EXP_ROWS
    ends = jnp.cumsum(padded)
    offs = ends - padded
    e1 = meta[:, META_E1].astype(jnp.int32)
    e2 = meta[:, META_E2].astype(jnp.int32)
    pos1 = offs[e1] + meta[:, META_R1].astype(jnp.int32)
    pos2 = offs[e2] + meta[:, META_R2].astype(jnp.int32)
    rows = 2 * t + N_EXPERTS * EXP_ROWS
    tok = jnp.arange(t, dtype=jnp.int32)
    tok_sorted = jnp.zeros((rows,), jnp.int32).at[pos1].set(tok).at[pos2].set(tok)
    blk_start = jnp.arange(rows // EXP_ROWS, dtype=jnp.int32) * EXP_ROWS
    blk_expert = jnp.minimum(
        jnp.sum(blk_start[:, None] >= ends[None, :], axis=1), N_EXPERTS - 1).astype(jnp.int32)
    n_active = (ends[-1:] // EXP_ROWS).astype(jnp.int32)

    xs = _sc_gather(xp, tok_sorted)
    ys = _experts(layer, xs, blk_expert, n_active, wg, wu, wd)
    yg = _sc_gather(ys, jnp.concatenate([pos1, pos2]))
    return _combine(layer, final, h1, yg, meta, p, norm_ple, w_pp, w_pg, norm_final)


def kernel(x, p, w_in, gla_wg2, gla_bg, hg_lb_logits, gla_onorm, hg_onorm, w_out, norm_mix,
           norm_ffn, w_dense_gate, w_dense_up, w_dense_down, w_router, w_exp_gate, w_exp_up,
           w_exp_down, w_ple_proj, w_ple_gate, norm_ple, norm_final):
    bsz, seq, _ = x.shape
    t = bsz * seq
    h = x
    for i in range(DEPTH):
        final = i == DEPTH - 1
        o_f, o_b, gates = _mix(i, h, norm_mix[i], w_in[i], gla_wg2[i], gla_bg[i], hg_lb_logits)
        flat = lambda a: a.reshape(t, a.shape[-1])
        h1 = _post(i, flat(o_f), flat(o_b), flat(gates), flat(h), gla_onorm[i], hg_onorm[i],
                   w_out[i])
        p_i = flat(p[i])
        j = i // 2
        if i % 2 == 0:
            h3 = _dense_ffn(i, final, h1, p_i, norm_ffn[i], w_dense_gate[j], w_dense_up[j],
                            w_dense_down[j], norm_ple[i], w_ple_proj[i], w_ple_gate[i],
                            norm_final)
        else:
            h3 = _moe(i, final, h1, p_i, norm_ffn[i], w_router[j], w_exp_gate[j], w_exp_up[j],
                      w_exp_down[j], norm_ple[i], w_ple_proj[i], w_ple_gate[i], norm_final)
        h = h3.reshape(bsz, seq, D_MODEL)
    return h
```

```python
import functools

import jax
import jax.numpy as jnp
from jax import lax
from jax.experimental import pallas as pl
from jax.experimental.pallas import tpu as pltpu
from jax.experimental.pallas import tpu_sc as plsc

F32 = jnp.float32
BF16 = jnp.bfloat16

D_MODEL = 1024
DEPTH = 4
N_HEADS = 8
HEAD_W = 128
GLA_DK = 64
GLA_RANK = 16
GLA_GATE_NORM = 16.0
HG_DK = 128
HG_K = 512
D_FF_DENSE = 2816
N_EXPERTS = 8
D_FF_EXPERT = 3584
PLE_DIM = 256
EPS = 1e-6
HEAD_EPS = 1e-5
F_MIN = 1e-6

CHUNK = 64
SUB = 16
EXP_CLAMP = 60.0
LANES = 128
VMEM_LIMIT = 56 * 1024 * 1024

MIX_TILE = 256
ROW_TILE = 512
FF_CHUNK = 256
EXP_FF_CHUNK = 512

COL_Q, COL_K, COL_V = 0, 1024, 2048
COL_GATE_F, COL_GLR_F = 3072, 4096
COL_GLR_B = 3072
NF_COLS, NB_COLS = 4224, 3200


def _dot(a, b):
    return jnp.dot(a, b, preferred_element_type=F32)


def _dot_nt(a, b):
    return lax.dot_general(a, b, (((1,), (1,)), ((), ())), preferred_element_type=F32)


def _dot_tn(a, b):
    return lax.dot_general(a, b, (((0,), (0,)), ((), ())), preferred_element_type=F32)


def _sigmoid(x):
    return 1.0 / (1.0 + jnp.exp(-x))


def _rmsnorm(x, w):
    ms = jnp.mean(x * x, axis=-1, keepdims=True)
    return x * lax.rsqrt(ms + EPS) * w


def _const_spec(shape):
    nd = len(shape)
    return pl.BlockSpec(shape, lambda *_: (0,) * nd, pipeline_mode=pl.Buffered(1))


def _params(n_grid):
    return pltpu.CompilerParams(
        dimension_semantics=("arbitrary",) * n_grid, vmem_limit_bytes=VMEM_LIMIT)


A_GROUPS = 3
A_WIDTH = A_GROUPS * HEAD_W


def _stage_a(rev, p_ref, g_ref, r0, tri, bufs, slot):
    qa, ka, qd, kd, qi, ks, vb, et, _ = bufs
    g = g_ref[r0:r0 + CHUNK, :]
    g_hi = g.astype(BF16)
    g_lo = (g - g_hi.astype(F32)).astype(BF16)
    cum = _dot(tri, g_hi) + _dot(tri, g_lo)
    q = p_ref[r0:r0 + CHUNK, COL_Q:COL_Q + D_MODEL]
    k = p_ref[r0:r0 + CHUNK, COL_K:COL_K + D_MODEL]

    def put(ref, rows, group, val):
        for n in range(N_HEADS):
            c0 = n * A_WIDTH + group * HEAD_W
            ref[slot, rows, c0:c0 + HEAD_W] = val[:, n * HEAD_W:(n + 1) * HEAD_W]

    for group, (half, a) in enumerate(((2 * SUB, 0), (SUB, 0), (SUB, 2 * SUB))):
        lo, hi = slice(a, a + half), slice(a + half, a + 2 * half)
        if not rev:
            ref, k_rows, q_rows = cum[a + half - 1:a + half], lo, hi
        else:
            ref, q_rows, k_rows = cum[a + half:a + half + 1], lo, hi
        put(qa, q_rows, group, (q[q_rows] * jnp.exp(cum[q_rows] - ref)).astype(BF16))
        put(ka, k_rows, group, (k[k_rows] * jnp.exp(ref - cum[k_rows])).astype(BF16))

    mids = []
    for a in range(0, CHUNK, SUB):
        m = 0.5 * (cum[a:a + 1] + cum[a + SUB - 1:a + SUB])
        mids.append(jnp.broadcast_to(m, (SUB, D_MODEL)))
    dd = cum - jnp.concatenate(mids, 0)
    qd[slot] = (q * jnp.exp(jnp.clip(dd, -EXP_CLAMP, EXP_CLAMP))).astype(BF16)
    kd[slot] = (k * jnp.exp(jnp.clip(-dd, -EXP_CLAMP, EXP_CLAMP))).astype(BF16)

    tot = cum[0:1] if rev else cum[CHUNK - 1:CHUNK]
    qi[slot] = (q * jnp.exp(cum)).astype(BF16)
    ks[slot] = (k * jnp.exp(tot - cum)).astype(BF16)
    vb[slot] = p_ref[r0:r0 + CHUNK, COL_V:COL_V + D_MODEL].astype(BF16)
    et[slot] = jnp.exp(tot)


def _stage_b(bufs, slot, st_ref, o_ref, r0, diag):
    qa, ka, qd, kd, qi, ks, vb, et, ab = bufs
    for n in range(N_HEADS):
        wide = slice(n * A_WIDTH, (n + 1) * A_WIDTH)
        head = slice(n * HEAD_W, (n + 1) * HEAD_W)
        scores = _dot_nt(qa[slot, :, wide], ka[slot, :, wide])
        scores = scores + jnp.where(diag, _dot_nt(qd[slot, :, head], kd[slot, :, head]), 0.0)
        ab[slot, n] = scores.astype(BF16)
        st = st_ref[n]
        o_ref[0, r0:r0 + CHUNK, head] = _dot_nt(qi[slot, :, head], st.astype(BF16))
        st_ref[n] = st * et[slot, :, head] + _dot_tn(vb[slot, :, head], ks[slot, :, head])


def _stage_c(bufs, slot, o_ref, r0):
    vb, ab = bufs[6], bufs[8]
    for n in range(N_HEADS):
        head = slice(n * HEAD_W, (n + 1) * HEAD_W)
        o_ref[0, r0:r0 + CHUNK, head] += _dot(ab[slot, n], vb[slot, :, head])


def _mix_kernel(layer, hf_ref, hb_ref, nw_ref, wf_ref, wb_ref, wg2_ref, bg_ref, lbl_ref,
                of_ref, ob_ref, gate_ref, pf_ref, pb_ref, gf_ref, gb_ref, sf_ref, sb_ref,
                *buf_refs):
    j = pl.program_id(1)
    n_chunks = MIX_TILE // CHUNK
    bufs_f, bufs_b = buf_refs[:len(buf_refs) // 2], buf_refs[len(buf_refs) // 2:]

    @pl.when(j == 0)
    def _():
        sf_ref[...] = jnp.zeros_like(sf_ref)
        sb_ref[...] = jnp.zeros_like(sb_ref)
        for bufs in (bufs_f, bufs_b):
            bufs[0][...] = jnp.zeros_like(bufs[0])
            bufs[1][...] = jnp.zeros_like(bufs[1])

    nw = nw_ref[...]
    pf_ref[...] = _dot(_rmsnorm(hf_ref[0], nw).astype(BF16), wf_ref[...])
    pb_ref[...] = _dot(_rmsnorm(hb_ref[0], nw).astype(BF16), wb_ref[...])
    gate_ref[0] = pf_ref[:, COL_GATE_F:COL_GATE_F + 1024].astype(BF16)

    def prep(direction, p_ref, g_ref, col_glr):
        p_ref[:, COL_Q:COL_Q + 512] = p_ref[:, COL_Q:COL_Q + 512] * (GLA_DK ** -0.5)
        x = _dot(p_ref[:, col_glr:col_glr + LANES].astype(BF16), wg2_ref[direction].astype(BF16))
        x = x + bg_ref[direction]
        log_sig = jnp.minimum(x, 0.0) - jnp.log(1.0 + jnp.exp(-jnp.abs(x)))
        g_ref[:, 0:512] = log_sig * (1.0 / GLA_GATE_NORM)
        hq = p_ref[:, COL_Q + 512:COL_Q + 1024]
        p_ref[:, COL_Q + 512:COL_Q + 1024] = hq * _sigmoid(hq) * (HG_DK ** -0.5)
        rows = [lbl_ref[2 * d + direction:2 * d + direction + 1, :] for d in range(DEPTH)]
        mx = functools.reduce(jnp.maximum, rows)
        ex = [jnp.exp(r - mx) for r in rows]
        lb = sum(ex[1:layer + 1], jnp.zeros_like(mx)) / sum(ex)
        z = p_ref[:, COL_K + 512:COL_K + 1024]
        e = jnp.exp(-jnp.abs(z))
        s_big = 1.0 / (1.0 + e)
        s_small = e * s_big
        sig = jnp.where(z >= 0, s_big, s_small)
        nsig = jnp.where(z >= 0, s_small, s_big)
        f = lb + (1.0 - lb) * sig
        g_ref[:, 512:1024] = jnp.log(jnp.maximum(f, F_MIN))
        p_ref[:, COL_K + 512:COL_K + 1024] = (1.0 - lb) * nsig

    prep(0, pf_ref, gf_ref, COL_GLR_F)
    prep(1, pb_ref, gb_ref, COL_GLR_B)

    ri = lax.broadcasted_iota(jnp.int32, (CHUNK, CHUNK), 0)
    ci = lax.broadcasted_iota(jnp.int32, (CHUNK, CHUNK), 1)
    tri_f = (ci <= ri).astype(BF16)
    tri_b = (ci >= ri).astype(BF16)
    same16 = (ri // SUB) == (ci // SUB)
    diag_f = same16 & (ci <= ri)
    diag_b = same16 & (ci >= ri)

    def stage_a(c):
        _stage_a(False, pf_ref, gf_ref, c * CHUNK, tri_f, bufs_f, c % 2)
        _stage_a(True, pb_ref, gb_ref, (n_chunks - 1 - c) * CHUNK, tri_b, bufs_b, c % 2)

    def stage_b(c):
        _stage_b(bufs_f, c % 2, sf_ref, of_ref, c * CHUNK, diag_f)
        _stage_b(bufs_b, c % 2, sb_ref, ob_ref, (n_chunks - 1 - c) * CHUNK, diag_b)

    def stage_c(c):
        _stage_c(bufs_f, c % 2, of_ref, c * CHUNK)
        _stage_c(bufs_b, c % 2, ob_ref, (n_chunks - 1 - c) * CHUNK)

    stage_a(0)
    for c in range(n_chunks):
        if c + 1 < n_chunks:
            stage_a(c + 1)
        stage_b(c)
        stage_c(c)


def _pad_heads(w, n_heads, width):
    lead = w.shape[:-1]
    w = w.reshape(lead + (n_heads, width))
    w = jnp.pad(w, [(0, 0)] * len(lead) + [(0, 0), (0, HEAD_W - width)])
    return w.reshape(lead + (n_heads * HEAD_W,))


def _scan_bufs():
    wide = pltpu.VMEM((2, CHUNK, N_HEADS * A_WIDTH), BF16)
    narrow = pltpu.VMEM((2, CHUNK, D_MODEL), BF16)
    return [wide, wide, narrow, narrow, narrow, narrow, narrow,
            pltpu.VMEM((2, 1, D_MODEL), F32),
            pltpu.VMEM((2, N_HEADS, CHUNK, CHUNK), BF16)]


def _mix(layer, h, norm_w, w_in, wg2, bg, lb_logits):
    bsz, seq, _ = h.shape
    nt = seq // MIX_TILE
    gq, gk, gv, gr, glr_f, glr_b, hq, hf_f, hf_b, hv, hr = jnp.split(
        w_in, [256, 512, 1024, 1536, 1552, 1568, 2080, 2592, 3104, 3616], axis=-1)
    gq_p, gk_p = _pad_heads(gq, 4, GLA_DK), _pad_heads(gk, 4, GLA_DK)
    pad_r = lambda w: jnp.pad(w, ((0, 0), (0, LANES - GLA_RANK)))
    wf = jnp.concatenate([gq_p, hq, gk_p, hf_f, gv, hv, gr, hr, pad_r(glr_f)], -1).astype(BF16)
    wb = jnp.concatenate([gq_p, hq, gk_p, hf_b, gv, hv, pad_r(glr_b)], -1).astype(BF16)
    wg2_p = jnp.pad(_pad_heads(wg2, 4, GLA_DK), ((0, 0), (0, LANES - GLA_RANK), (0, 0)))
    bg_p = _pad_heads(bg, 4, GLA_DK).reshape(2, 1, 512)
    lbl = lb_logits.reshape(DEPTH * 2, HG_K)

    tile = lambda idx: pl.BlockSpec((1, MIX_TILE, D_MODEL), idx)
    fwd = lambda b, j: (b, j, 0)
    bwd = lambda b, j: (b, nt - 1 - j, 0)
    return pl.pallas_call(
        functools.partial(_mix_kernel, layer),
        grid=(bsz, nt),
        in_specs=[tile(fwd), tile(bwd),
                  _const_spec((1, D_MODEL)),
                  _const_spec((D_MODEL, NF_COLS)), _const_spec((D_MODEL, NB_COLS)),
                  _const_spec((2, LANES, 512)), _const_spec((2, 1, 512)),
                  _const_spec((DEPTH * 2, HG_K))],
        out_specs=[tile(fwd), tile(bwd), tile(fwd)],
        out_shape=[jax.ShapeDtypeStruct((bsz, seq, D_MODEL), F32),
                   jax.ShapeDtypeStruct((bsz, seq, D_MODEL), F32),
                   jax.ShapeDtypeStruct((bsz, seq, D_MODEL), BF16)],
        scratch_shapes=[pltpu.VMEM((MIX_TILE, NF_COLS), F32),
                        pltpu.VMEM((MIX_TILE, NB_COLS), F32),
                        pltpu.VMEM((MIX_TILE, D_MODEL), F32),
                        pltpu.VMEM((MIX_TILE, D_MODEL), F32),
                        pltpu.VMEM((N_HEADS, HEAD_W, HEAD_W), F32),
                        pltpu.VMEM((N_HEADS, HEAD_W, HEAD_W), F32)] + _scan_bufs() + _scan_bufs(),
        compiler_params=_params(2),
        name=f"mix_l{layer}",
    )(h, h, norm_w.reshape(1, D_MODEL), wf, wb, wg2_p, bg_p, lbl)


def _post_kernel(of_ref, ob_ref, gate_ref, h_ref, ones_ref, onw_ref, wout_ref, out_ref):
    o = of_ref[...] + ob_ref[...]
    ms = _dot((o * o).astype(BF16), ones_ref[...]) * (1.0 / HEAD_W)
    y = o * lax.rsqrt(ms + HEAD_EPS) * onw_ref[...]
    g = gate_ref[...].astype(F32)
    y = y * (g * _sigmoid(g))
    out_ref[...] = h_ref[...] + _dot(y.astype(BF16), wout_ref[...])


def _post(layer, o_f, o_b, gates, h, gla_onorm, hg_onorm, w_out):
    t = h.shape[0]
    head_id = jnp.arange(D_MODEL) // HEAD_W
    ones = (head_id[:, None] == head_id[None, :]).astype(BF16)
    onw = jnp.concatenate([jnp.tile(gla_onorm, 4), jnp.tile(hg_onorm, 4)]).reshape(1, D_MODEL)
    row = lambda: pl.BlockSpec((ROW_TILE, D_MODEL), lambda i: (i, 0))
    return pl.pallas_call(
        _post_kernel,
        grid=(t // ROW_TILE,),
        in_specs=[row(), row(), row(), row(),
                  _const_spec((D_MODEL, D_MODEL)), _const_spec((1, D_MODEL)),
                  _const_spec((D_MODEL, D_MODEL))],
        out_specs=row(),
        out_shape=jax.ShapeDtypeStruct((t, D_MODEL), F32),
        compiler_params=_params(1),
        name=f"post_l{layer}",
    )(o_f, o_b, gates, h, ones, onw, w_out.astype(BF16))


def _ple_tail(h2, p, nple, wpp_ref, wpg_ref, nfinal, final):
    gate = _sigmoid(_dot(_rmsnorm(h2, nple).astype(BF16), wpg_ref[...]))
    h3 = h2 + _dot(p.astype(BF16), wpp_ref[...]) * gate
    if final:
        h3 = _rmsnorm(h3, nfinal)
    return h3


def _dense_kernel(final, h_ref, p_ref, nffn_ref, wg_ref, wu_ref, wd_ref, nple_ref,
                  wpp_ref, wpg_ref, nfin_ref, out_ref):
    h1 = h_ref[...]
    v = _rmsnorm(h1, nffn_ref[...]).astype(BF16)
    acc = jnp.zeros_like(h1)
    for c in range(0, D_FF_DENSE, FF_CHUNK):
        a = _dot(v, wg_ref[:, c:c + FF_CHUNK])
        b = _dot(v, wu_ref[:, c:c + FF_CHUNK])
        acc = acc + _dot((a * _sigmoid(a) * b).astype(BF16), wd_ref[c:c + FF_CHUNK, :])
    out_ref[...] = _ple_tail(h1 + acc, p_ref[...], nple_ref[...], wpp_ref, wpg_ref,
                             nfin_ref[...], final)


def _dense_ffn(layer, final, h, p, norm_ffn, wg, wu, wd, norm_ple, w_pp, w_pg, norm_final):
    t = h.shape[0]
    vec = lambda w: w.reshape(1, D_MODEL)
    return pl.pallas_call(
        functools.partial(_dense_kernel, final),
        grid=(t // ROW_TILE,),
        in_specs=[pl.BlockSpec((ROW_TILE, D_MODEL), lambda i: (i, 0)),
                  pl.BlockSpec((ROW_TILE, PLE_DIM), lambda i: (i, 0)),
                  _const_spec((1, D_MODEL)),
                  _const_spec((D_MODEL, D_FF_DENSE)), _const_spec((D_MODEL, D_FF_DENSE)),
                  _const_spec((D_FF_DENSE, D_MODEL)),
                  _const_spec((1, D_MODEL)),
                  _const_spec((PLE_DIM, D_MODEL)), _const_spec((D_MODEL, D_MODEL)),
                  _const_spec((1, D_MODEL))],
        out_specs=pl.BlockSpec((ROW_TILE, D_MODEL), lambda i: (i, 0)),
        out_shape=jax.ShapeDtypeStruct((t, D_MODEL), F32),
        compiler_params=_params(1),
        name=f"dense_l{layer}",
    )(h, p, vec(norm_ffn), wg.astype(BF16), wu.astype(BF16), wd.astype(BF16),
      vec(norm_ple), w_pp.astype(BF16), w_pg.astype(BF16), vec(norm_final))


META_G1, META_G2, META_E1, META_E2, META_R1, META_R2 = range(6)
HALF = D_MODEL // 2
HI_MASK = 0xFFFF0000


def _pack_rows(x):
    bits = pltpu.bitcast(x.astype(BF16).astype(F32), jnp.uint32)
    return (bits[:, :HALF] >> 16) | (bits[:, HALF:] & jnp.uint32(HI_MASK))


def _unpack_rows(w):
    lo = pltpu.bitcast(w << 16, F32)
    hi = pltpu.bitcast(w & jnp.uint32(HI_MASK), F32)
    return jnp.concatenate([lo, hi], axis=1)


def _router_kernel(h_ref, nffn_ref, wr_ref, tri_ref, xp_ref, meta_ref, cnt_ref, base_ref):
    @pl.when(pl.program_id(0) == 0)
    def _():
        base_ref[...] = jnp.zeros_like(base_ref)

    vf = _rmsnorm(h_ref[...], nffn_ref[...])
    xp_ref[...] = _pack_rows(vf)
    v_hi = vf.astype(BF16)
    v_lo = (vf - v_hi.astype(F32)).astype(BF16)
    w = wr_ref[...]
    w_hi = w.astype(BF16)
    w_lo = (w - w_hi.astype(F32)).astype(BF16)
    logits = _dot(v_hi, w_hi) + _dot(v_lo, w_hi) + _dot(v_hi, w_lo)
    lane = lax.broadcasted_iota(jnp.int32, logits.shape, 1)
    neg = jnp.float32(-jnp.inf)
    lg = jnp.where(lane < N_EXPERTS, logits, neg)
    m1 = jnp.max(lg, axis=-1, keepdims=True)
    i1 = jnp.min(jnp.where(lg == m1, lane, LANES), axis=-1, keepdims=True)
    first = lane == i1
    lg2 = jnp.where(first, neg, lg)
    m2 = jnp.max(lg2, axis=-1, keepdims=True)
    i2 = jnp.min(jnp.where(lg2 == m2, lane, LANES), axis=-1, keepdims=True)
    second = lane == i2
    e = jnp.exp(m2 - m1)
    g1 = 1.0 / (1.0 + e)
    g2 = e * g1
    cnt = jnp.where(first | second, 1.0, 0.0)
    rank = base_ref[...] + _dot(tri_ref[...], cnt.astype(BF16))
    r1 = jnp.sum(jnp.where(first, rank, 0.0), axis=-1, keepdims=True)
    r2 = jnp.sum(jnp.where(second, rank, 0.0), axis=-1, keepdims=True)
    base_ref[...] = base_ref[...] + jnp.sum(cnt, axis=0, keepdims=True)
    cnt_ref[...] = base_ref[...]
    meta = jnp.zeros(logits.shape, F32)
    for col, val in ((META_G1, g1), (META_G2, g2), (META_E1, i1.astype(F32)),
                     (META_E2, i2.astype(F32)), (META_R1, r1), (META_R2, r2)):
        meta = jnp.where(lane == col, val, meta)
    meta_ref[...] = meta


def _router(layer, h, norm_ffn, w_router):
    t = h.shape[0]
    wr = jnp.pad(w_router, ((0, 0), (0, LANES - N_EXPERTS)))
    ids = jnp.arange(ROW_TILE)
    tri = (ids[None, :] < ids[:, None]).astype(BF16)
    return pl.pallas_call(
        _router_kernel,
        grid=(t // ROW_TILE,),
        in_specs=[pl.BlockSpec((ROW_TILE, D_MODEL), lambda i: (i, 0)),
                  _const_spec((1, D_MODEL)), _const_spec((D_MODEL, LANES)),
                  _const_spec((ROW_TILE, ROW_TILE))],
        out_specs=[pl.BlockSpec((ROW_TILE, HALF), lambda i: (i, 0)),
                   pl.BlockSpec((ROW_TILE, LANES), lambda i: (i, 0)),
                   pl.BlockSpec((1, LANES), lambda i: (0, 0))],
        out_shape=[jax.ShapeDtypeStruct((t, HALF), jnp.uint32),
                   jax.ShapeDtypeStruct((t, LANES), F32),
                   jax.ShapeDtypeStruct((1, LANES), F32)],
        scratch_shapes=[pltpu.VMEM((1, LANES), F32)],
        compiler_params=_params(1),
        name=f"router_l{layer}",
    )(h, norm_ffn.reshape(1, D_MODEL), wr, tri)


SC_CORES, SC_SUBCORES = 2, 16
SC_WORKERS = SC_CORES * SC_SUBCORES
SC_WINDOW = 128


def _sc_gather(table, idx):
    n_out, d = idx.shape[0], table.shape[1]
    per_worker = n_out // SC_WORKERS
    n_win = per_worker // SC_WINDOW
    assert n_win * SC_WINDOW * SC_WORKERS == n_out
    mesh = plsc.VectorSubcoreMesh(core_axis_name="c", subcore_axis_name="s")

    def body(table_hbm, idx_hbm, out_hbm, idx_v, rows_v, sem):
        wid = lax.axis_index("s") * SC_CORES + lax.axis_index("c")

        @pl.loop(0, n_win)
        def _(w):
            base = pl.multiple_of(wid * per_worker + w * SC_WINDOW, SC_WINDOW)
            pltpu.sync_copy(idx_hbm.at[pl.ds(base, SC_WINDOW)], idx_v)
            pltpu.async_copy(table_hbm.at[idx_v], rows_v, sem).wait()
            pltpu.sync_copy(rows_v, out_hbm.at[pl.ds(base, SC_WINDOW)])

    return pl.kernel(
        body, mesh=mesh,
        out_type=jax.ShapeDtypeStruct((n_out, d), table.dtype),
        scratch_types=[pltpu.VMEM((SC_WINDOW,), jnp.int32),
                       pltpu.VMEM((SC_WINDOW, d), table.dtype),
                       pltpu.SemaphoreType.DMA],
    )(table, idx)


EXP_ROWS = 512


def _experts_kernel(be_ref, na_ref, xs_ref, wg_ref, wu_ref, wd_ref, ys_ref, x_ref, acc_ref):
    b = pl.program_id(0)
    f = pl.program_id(1)

    @pl.when(b < na_ref[0])
    def _():
        @pl.when(f == 0)
        def _():
            x_ref[...] = _unpack_rows(xs_ref[...]).astype(BF16)
            acc_ref[...] = jnp.zeros_like(acc_ref)

        x = x_ref[...]
        a = _dot(x, wg_ref[0])
        u = _dot(x, wu_ref[0])
        acc_ref[...] += _dot((a * _sigmoid(a) * u).astype(BF16), wd_ref[0])

        @pl.when(f == pl.num_programs(1) - 1)
        def _():
            ys_ref[...] = _pack_rows(acc_ref[...])


def _experts(layer, xs, blk_expert, n_active, wg, wu, wd):
    rows = xs.shape[0]
    nf = D_FF_EXPERT // EXP_FF_CHUNK
    fsel = lambda b, f, na: jnp.where(b < na[0], f, nf - 1)
    grid_spec = pltpu.PrefetchScalarGridSpec(
        num_scalar_prefetch=2,
        grid=(rows // EXP_ROWS, nf),
        in_specs=[pl.BlockSpec((EXP_ROWS, HALF), lambda b, f, be, na: (b, 0)),
                  pl.BlockSpec((1, D_MODEL, EXP_FF_CHUNK),
                               lambda b, f, be, na: (be[b], 0, fsel(b, f, na))),
                  pl.BlockSpec((1, D_MODEL, EXP_FF_CHUNK),
                               lambda b, f, be, na: (be[b], 0, fsel(b, f, na))),
                  pl.BlockSpec((1, EXP_FF_CHUNK, D_MODEL),
                               lambda b, f, be, na: (be[b], fsel(b, f, na), 0))],
        out_specs=pl.BlockSpec((EXP_ROWS, HALF), lambda b, f, be, na: (b, 0)),
        scratch_shapes=[pltpu.VMEM((EXP_ROWS, D_MODEL), BF16),
                        pltpu.VMEM((EXP_ROWS, D_MODEL), F32)])
    return pl.pallas_call(
        _experts_kernel,
        grid_spec=grid_spec,
        out_shape=jax.ShapeDtypeStruct((rows, HALF), jnp.uint32),
        compiler_params=_params(2),
        name=f"experts_l{layer}",
    )(blk_expert, n_active, xs, wg.astype(BF16), wu.astype(BF16), wd.astype(BF16))


def _combine_kernel(final, h_ref, y1_ref, y2_ref, meta_ref, p_ref, nple_ref, wpp_ref, wpg_ref,
                    nfin_ref, out_ref):
    meta = meta_ref[...]
    g1 = meta[:, META_G1:META_G1 + 1]
    g2 = meta[:, META_G2:META_G2 + 1]
    h2 = h_ref[...] + g1 * _unpack_rows(y1_ref[...]) + g2 * _unpack_rows(y2_ref[...])
    out_ref[...] = _ple_tail(h2, p_ref[...], nple_ref[...], wpp_ref, wpg_ref, nfin_ref[...], final)


def _combine(layer, final, h, yg, meta, p, norm_ple, w_pp, w_pg, norm_final):
    t = h.shape[0]
    nt = t // ROW_TILE
    vec = lambda w: w.reshape(1, D_MODEL)
    return pl.pallas_call(
        functools.partial(_combine_kernel, final),
        grid=(nt,),
        in_specs=[pl.BlockSpec((ROW_TILE, D_MODEL), lambda i: (i, 0)),
                  pl.BlockSpec((ROW_TILE, HALF), lambda i: (i, 0)),
                  pl.BlockSpec((ROW_TILE, HALF), lambda i: (i + nt, 0)),
                  pl.BlockSpec((ROW_TILE, LANES), lambda i: (i, 0)),
                  pl.BlockSpec((ROW_TILE, PLE_DIM), lambda i: (i, 0)),
                  _const_spec((1, D_MODEL)),
                  _const_spec((PLE_DIM, D_MODEL)), _const_spec((D_MODEL, D_MODEL)),
                  _const_spec((1, D_MODEL))],
        out_specs=pl.BlockSpec((ROW_TILE, D_MODEL), lambda i: (i, 0)),
        out_shape=jax.ShapeDtypeStruct((t, D_MODEL), F32),
        compiler_params=_params(1),
        name=f"combine_l{layer}",
    )(h, yg, yg, meta, p, vec(norm_ple), w_pp.astype(BF16), w_pg.astype(BF16), vec(norm_final))


def _moe(layer, final, h1, p, norm_ffn, w_router, wg, wu, wd, norm_ple, w_pp, w_pg, norm_final):
    t = h1.shape[0]
    xp, meta, counts = _router(layer, h1, norm_ffn, w_router)
    cnt = counts[0, :N_EXPERTS].astype(jnp.int32)
    padded = ((cnt + EXP_ROWS - 1) // EXP_ROWS) * EXP_ROWS
    ends = jnp.cumsum(padded)
    offs = ends - padded
    e1 = meta[:, META_E1].astype(jnp.int32)
    e2 = meta[:, META_E2].astype(jnp.int32)
    pos1 = offs[e1] + meta[:, META_R1].astype(jnp.int32)
    pos2 = offs[e2] + meta[:, META_R2].astype(jnp.int32)
    rows = 2 * t + N_EXPERTS * EXP_ROWS
    tok = jnp.arange(t, dtype=jnp.int32)
    tok_sorted = jnp.zeros((rows,), jnp.int32).at[pos1].set(tok).at[pos2].set(tok)
    blk_start = jnp.arange(rows // EXP_ROWS, dtype=jnp.int32) * EXP_ROWS
    blk_expert = jnp.minimum(
        jnp.sum(blk_start[:, None] >= ends[None, :], axis=1), N_EXPERTS - 1).astype(jnp.int32)
    n_active = (ends[-1:] // EXP_ROWS).astype(jnp.int32)

    xs = _sc_gather(xp, tok_sorted)
    ys = _experts(layer, xs, blk_expert, n_active, wg, wu, wd)
    yg = _sc_gather(ys, jnp.concatenate([pos1, pos2]))
    return _combine(layer, final, h1, yg, meta, p, norm_ple, w_pp, w_pg, norm_final)


def kernel(x, p, w_in, gla_wg2, gla_bg, hg_lb_logits, gla_onorm, hg_onorm, w_out, norm_mix,
           norm_ffn, w_dense_gate, w_dense_up, w_dense_down, w_router, w_exp_gate, w_exp_up,
           w_exp_down, w_ple_proj, w_ple_gate, norm_ple, norm_final):
    bsz, seq, _ = x.shape
    t = bsz * seq
    h = x
    for i in range(DEPTH):
        final = i == DEPTH - 1
        o_f, o_b, gates = _mix(i, h, norm_mix[i], w_in[i], gla_wg2[i], gla_bg[i], hg_lb_logits)
        flat = lambda a: a.reshape(t, a.shape[-1])
        h1 = _post(i, flat(o_f), flat(o_b), flat(gates), flat(h), gla_onorm[i], hg_onorm[i],
                   w_out[i])
        p_i = flat(p[i])
        j = i // 2
        if i % 2 == 0:
            h3 = _dense_ffn(i, final, h1, p_i, norm_ffn[i], w_dense_gate[j], w_dense_up[j],
                            w_dense_down[j], norm_ple[i], w_ple_proj[i], w_ple_gate[i],
                            norm_final)
        else:
            h3 = _moe(i, final, h1, p_i, norm_ffn[i], w_router[j], w_exp_gate[j], w_exp_up[j],
                      w_exp_down[j], norm_ple[i], w_ple_proj[i], w_ple_gate[i], norm_final)
        h = h3.reshape(bsz, seq, D_MODEL)
    return h
```

```python
import functools

import jax
import jax.numpy as jnp
from jax import lax
from jax.experimental import pallas as pl
from jax.experimental.pallas import tpu as pltpu
from jax.experimental.pallas import tpu_sc as plsc

F32 = jnp.float32
BF16 = jnp.bfloat16

D_MODEL = 1024
DEPTH = 4
N_HEADS = 8
HEAD_W = 128
GLA_DK = 64
GLA_RANK = 16
GLA_GATE_NORM = 16.0
HG_DK = 128
HG_K = 512
D_FF_DENSE = 2816
N_EXPERTS = 8
D_FF_EXPERT = 3584
PLE_DIM = 256
EPS = 1e-6
HEAD_EPS = 1e-5
F_MIN = 1e-6

CHUNK = 64
SUB = 16
EXP2_CLAMP = 86.0
LANES = 128
VMEM_LIMIT = 56 * 1024 * 1024

MIX_TILE = 256
ROW_TILE = 512
FF_CHUNK = 256
EXP_FF_CHUNK = 512

COL_HQ = 0
COL_HF = 512
COL_GLR = 1536
COL_GQ = 1792
COL_GK = 2304
COL_V = 2816
COL_GATE = 3840
NP_COLS = 4864
COL_GROUPS = ((0, COL_GQ), (COL_GQ, COL_V), (COL_V, NP_COLS))
LOG2E = 1.4426950408889634


def _dot(a, b):
    return jnp.dot(a, b, preferred_element_type=F32)


def _dot_nt(a, b):
    return lax.dot_general(a, b, (((1,), (1,)), ((), ())), preferred_element_type=F32)


def _dot_tn(a, b):
    return lax.dot_general(a, b, (((0,), (0,)), ((), ())), preferred_element_type=F32)


def _sigmoid(x):
    return 1.0 / (1.0 + jnp.exp(-x))


def _rmsnorm(x, w):
    ms = jnp.mean(x * x, axis=-1, keepdims=True)
    return x * lax.rsqrt(ms + EPS) * w


def _const_spec(shape):
    nd = len(shape)
    return pl.BlockSpec(shape, lambda *_: (0,) * nd, pipeline_mode=pl.Buffered(1))


def _params(n_grid):
    return pltpu.CompilerParams(
        dimension_semantics=("arbitrary",) * n_grid, vmem_limit_bytes=VMEM_LIMIT)


def _proj_kernel(layer, h_ref, nw_ref, w_ref, wg2_ref, bg_ref, lbl_ref,
                 q_ref, kf_ref, kb_ref, v_ref, gate_ref, gfh_ref, gfl_ref, gbh_ref, gbl_ref,
                 u_ref, p1_ref, p2_ref, p3_ref):
    u_ref[...] = _rmsnorm(h_ref[...], nw_ref[...]).astype(BF16)
    for p_ref, (c0, c1) in zip((p1_ref, p2_ref, p3_ref), COL_GROUPS):
        p_ref[...] = _dot(u_ref[...], w_ref[:, c0:c1])

    def split(hi_ref, lo_ref, cols, g):
        hi = g.astype(BF16)
        hi_ref[:, cols] = hi
        lo_ref[:, cols] = (g - hi.astype(F32)).astype(BF16)

    hq = p1_ref[:, COL_HQ:COL_HQ + 512]
    q_ref[:, 512:1024] = (hq * _sigmoid(hq) * (HG_DK ** -0.5)).astype(BF16)
    for direction, (k_ref, hi_ref, lo_ref) in enumerate(
            ((kf_ref, gfh_ref, gfl_ref), (kb_ref, gbh_ref, gbl_ref))):
        c0 = COL_GLR + direction * LANES
        x = _dot(p1_ref[:, c0:c0 + LANES].astype(BF16), wg2_ref[direction].astype(BF16))
        x = x + bg_ref[direction]
        log_sig = jnp.minimum(x, 0.0) - jnp.log(1.0 + jnp.exp(-jnp.abs(x)))
        split(hi_ref, lo_ref, slice(0, 512), log_sig * (LOG2E / GLA_GATE_NORM))
        rows = [lbl_ref[2 * d + direction:2 * d + direction + 1, :] for d in range(DEPTH)]
        mx = functools.reduce(jnp.maximum, rows)
        ex = [jnp.exp(r - mx) for r in rows]
        lb = sum(ex[1:layer + 1], jnp.zeros_like(mx)) / sum(ex)
        c0 = COL_HF + direction * HG_K
        z = p1_ref[:, c0:c0 + HG_K]
        e = jnp.exp(-jnp.abs(z))
        s_big = 1.0 / (1.0 + e)
        s_small = e * s_big
        sig = jnp.where(z >= 0, s_big, s_small)
        nsig = jnp.where(z >= 0, s_small, s_big)
        f = lb + (1.0 - lb) * sig
        split(hi_ref, lo_ref, slice(512, 1024), jnp.log(jnp.maximum(f, F_MIN)) * LOG2E)
        k_ref[:, 512:1024] = ((1.0 - lb) * nsig).astype(BF16)

    q_ref[:, 0:512] = (p2_ref[:, 0:512] * (GLA_DK ** -0.5)).astype(BF16)
    gk = p2_ref[:, 512:1024].astype(BF16)
    kf_ref[:, 0:512] = gk
    kb_ref[:, 0:512] = gk
    v_ref[...] = p3_ref[:, 0:1024].astype(BF16)
    gate_ref[...] = p3_ref[:, 1024:2048].astype(BF16)


def _pad_heads(w, n_heads, width):
    lead = w.shape[:-1]
    w = w.reshape(lead + (n_heads, width))
    w = jnp.pad(w, [(0, 0)] * len(lead) + [(0, 0), (0, HEAD_W - width)])
    return w.reshape(lead + (n_heads * HEAD_W,))


def _proj(layer, h, norm_w, w_in, wg2, bg, lb_logits):
    t = h.shape[0]
    gq, gk, gv, gr, glr_f, glr_b, hq, hf_f, hf_b, hv, hr = jnp.split(
        w_in, [256, 512, 1024, 1536, 1552, 1568, 2080, 2592, 3104, 3616], axis=-1)
    pad_r = lambda w: jnp.pad(w, ((0, 0), (0, LANES - GLA_RANK)))
    w = jnp.concatenate([hq, hf_f, hf_b, pad_r(glr_f), pad_r(glr_b), _pad_heads(gq, 4, GLA_DK),
                         _pad_heads(gk, 4, GLA_DK), gv, hv, gr, hr], -1).astype(BF16)
    wg2_p = jnp.pad(_pad_heads(wg2, 4, GLA_DK), ((0, 0), (0, LANES - GLA_RANK), (0, 0)))
    bg_p = _pad_heads(bg, 4, GLA_DK).reshape(2, 1, 512)
    lbl = lb_logits.reshape(DEPTH * 2, HG_K)
    row = lambda: pl.BlockSpec((ROW_TILE, D_MODEL), lambda i: (i, 0))
    return pl.pallas_call(
        functools.partial(_proj_kernel, layer),
        grid=(t // ROW_TILE,),
        in_specs=[row(), _const_spec((1, D_MODEL)), _const_spec((D_MODEL, NP_COLS)),
                  _const_spec((2, LANES, 512)), _const_spec((2, 1, 512)),
                  _const_spec((DEPTH * 2, HG_K))],
        out_specs=[row() for _ in range(9)],
        out_shape=[jax.ShapeDtypeStruct((t, D_MODEL), BF16) for _ in range(9)],
        scratch_shapes=[pltpu.VMEM((ROW_TILE, D_MODEL), BF16)] + [
            pltpu.VMEM((ROW_TILE, c1 - c0), F32) for c0, c1 in COL_GROUPS],
        compiler_params=_params(1),
        name=f"proj_l{layer}",
    )(h, norm_w.reshape(1, D_MODEL), w, wg2_p, bg_p, lbl)


A_GROUPS = 3
A_WIDTH = A_GROUPS * HEAD_W


def _stage_a(rev, q_ref, k_ref, gh_ref, gl_ref, r0, tri, bufs, slot):
    qa, ka, qd, kd, qi, ks, et, _ = bufs
    rows = slice(r0, r0 + CHUNK)
    cum = _dot(tri, gh_ref[0, rows, :]) + _dot(tri, gl_ref[0, rows, :])
    q = q_ref[0, rows, :].astype(F32)
    k = k_ref[0, rows, :].astype(F32)

    def put(ref, rows, group, val):
        for n in range(N_HEADS):
            c0 = n * A_WIDTH + group * HEAD_W
            ref[slot, rows, c0:c0 + HEAD_W] = val[:, n * HEAD_W:(n + 1) * HEAD_W]

    for group, (half, a) in enumerate(((2 * SUB, 0), (SUB, 0), (SUB, 2 * SUB))):
        lo, hi = slice(a, a + half), slice(a + half, a + 2 * half)
        if not rev:
            ref, k_rows, q_rows = cum[a + half - 1:a + half], lo, hi
        else:
            ref, q_rows, k_rows = cum[a + half:a + half + 1], lo, hi
        put(qa, q_rows, group, (q[q_rows] * jnp.exp2(cum[q_rows] - ref)).astype(BF16))
        put(ka, k_rows, group, (k[k_rows] * jnp.exp2(ref - cum[k_rows])).astype(BF16))

    mids = []
    for a in range(0, CHUNK, SUB):
        m = 0.5 * (cum[a:a + 1] + cum[a + SUB - 1:a + SUB])
        mids.append(jnp.broadcast_to(m, (SUB, D_MODEL)))
    dd = cum - jnp.concatenate(mids, 0)
    qd[slot] = (q * jnp.exp2(jnp.clip(dd, -EXP2_CLAMP, EXP2_CLAMP))).astype(BF16)
    kd[slot] = (k * jnp.exp2(jnp.clip(-dd, -EXP2_CLAMP, EXP2_CLAMP))).astype(BF16)

    tot = cum[0:1] if rev else cum[CHUNK - 1:CHUNK]
    qi[slot] = (q * jnp.exp2(cum)).astype(BF16)
    ks[slot] = (k * jnp.exp2(tot - cum)).astype(BF16)
    et[slot] = jnp.exp2(tot)


def _stage_b(bufs, slot, v_ref, st_ref, o_ref, r0, diag):
    qa, ka, qd, kd, qi, ks, et, ab = bufs
    for n in range(N_HEADS):
        wide = slice(n * A_WIDTH, (n + 1) * A_WIDTH)
        head = slice(n * HEAD_W, (n + 1) * HEAD_W)
        scores = _dot_nt(qa[slot, :, wide], ka[slot, :, wide])
        scores = scores + jnp.where(diag, _dot_nt(qd[slot, :, head], kd[slot, :, head]), 0.0)
        ab[slot, n] = scores.astype(BF16)
        st = st_ref[n]
        o_ref[0, r0:r0 + CHUNK, head] = _dot_nt(qi[slot, :, head], st.astype(BF16))
        st_ref[n] = st * et[slot, :, head] + _dot_tn(v_ref[0, r0:r0 + CHUNK, head],
                                                    ks[slot, :, head])


def _stage_c(bufs, slot, v_ref, o_ref, r0):
    ab = bufs[7]
    for n in range(N_HEADS):
        head = slice(n * HEAD_W, (n + 1) * HEAD_W)
        o_ref[0, r0:r0 + CHUNK, head] += _dot(ab[slot, n], v_ref[0, r0:r0 + CHUNK, head])


def _scan_kernel(qf_ref, qb_ref, kf_ref, kb_ref, vf_ref, vb_ref, gfh_ref, gfl_ref, gbh_ref,
                 gbl_ref, of_ref, ob_ref, sf_ref, sb_ref, *buf_refs):
    n_chunks = MIX_TILE // CHUNK
    bufs_f, bufs_b = buf_refs[:len(buf_refs) // 2], buf_refs[len(buf_refs) // 2:]

    @pl.when(pl.program_id(1) == 0)
    def _():
        sf_ref[...] = jnp.zeros_like(sf_ref)
        sb_ref[...] = jnp.zeros_like(sb_ref)
        for bufs in (bufs_f, bufs_b):
            bufs[0][...] = jnp.zeros_like(bufs[0])
            bufs[1][...] = jnp.zeros_like(bufs[1])

    ri = lax.broadcasted_iota(jnp.int32, (CHUNK, CHUNK), 0)
    ci = lax.broadcasted_iota(jnp.int32, (CHUNK, CHUNK), 1)
    tri_f = (ci <= ri).astype(BF16)
    tri_b = (ci >= ri).astype(BF16)
    same16 = (ri // SUB) == (ci // SUB)
    diag_f = same16 & (ci <= ri)
    diag_b = same16 & (ci >= ri)
    row_f = lambda c: c * CHUNK
    row_b = lambda c: (n_chunks - 1 - c) * CHUNK

    def stage_a(c):
        _stage_a(False, qf_ref, kf_ref, gfh_ref, gfl_ref, row_f(c), tri_f, bufs_f, c % 2)
        _stage_a(True, qb_ref, kb_ref, gbh_ref, gbl_ref, row_b(c), tri_b, bufs_b, c % 2)

    stage_a(0)
    for c in range(n_chunks):
        if c + 1 < n_chunks:
            stage_a(c + 1)
        _stage_b(bufs_f, c % 2, vf_ref, sf_ref, of_ref, row_f(c), diag_f)
        _stage_b(bufs_b, c % 2, vb_ref, sb_ref, ob_ref, row_b(c), diag_b)
        _stage_c(bufs_f, c % 2, vf_ref, of_ref, row_f(c))
        _stage_c(bufs_b, c % 2, vb_ref, ob_ref, row_b(c))


def _scan_bufs():
    wide = pltpu.VMEM((2, CHUNK, N_HEADS * A_WIDTH), BF16)
    narrow = pltpu.VMEM((2, CHUNK, D_MODEL), BF16)
    return [wide, wide, narrow, narrow, narrow, narrow,
            pltpu.VMEM((2, 1, D_MODEL), F32),
            pltpu.VMEM((2, N_HEADS, CHUNK, CHUNK), BF16)]


def _scan(layer, bsz, q, kf, kb, v, gfh, gfl, gbh, gbl):
    seq = q.shape[0] // bsz
    nt = seq // MIX_TILE
    to3 = lambda a: a.reshape(bsz, seq, D_MODEL)
    tile = lambda idx: pl.BlockSpec((1, MIX_TILE, D_MODEL), idx)
    fwd = lambda b, j: (b, j, 0)
    bwd = lambda b, j: (b, nt - 1 - j, 0)
    state = pltpu.VMEM((N_HEADS, HEAD_W, HEAD_W), F32)
    return pl.pallas_call(
        _scan_kernel,
        grid=(bsz, nt),
        in_specs=[tile(fwd), tile(bwd), tile(fwd), tile(bwd), tile(fwd), tile(bwd),
                  tile(fwd), tile(fwd), tile(bwd), tile(bwd)],
        out_specs=[tile(fwd), tile(bwd)],
        out_shape=[jax.ShapeDtypeStruct((bsz, seq, D_MODEL), F32),
                   jax.ShapeDtypeStruct((bsz, seq, D_MODEL), F32)],
        scratch_shapes=[state, state] + _scan_bufs() + _scan_bufs(),
        compiler_params=_params(2),
        name=f"scan_l{layer}",
    )(to3(q), to3(q), to3(kf), to3(kb), to3(v), to3(v), to3(gfh), to3(gfl), to3(gbh), to3(gbl))


def _post_kernel(of_ref, ob_ref, gate_ref, h_ref, ones_ref, onw_ref, wout_ref, out_ref):
    o = of_ref[...] + ob_ref[...]
    ms = _dot((o * o).astype(BF16), ones_ref[...]) * (1.0 / HEAD_W)
    y = o * lax.rsqrt(ms + HEAD_EPS) * onw_ref[...]
    g = gate_ref[...].astype(F32)
    y = y * (g * _sigmoid(g))
    out_ref[...] = h_ref[...] + _dot(y.astype(BF16), wout_ref[...])


def _post(layer, o_f, o_b, gates, h, gla_onorm, hg_onorm, w_out):
    t = h.shape[0]
    head_id = jnp.arange(D_MODEL) // HEAD_W
    ones = (head_id[:, None] == head_id[None, :]).astype(BF16)
    onw = jnp.concatenate([jnp.tile(gla_onorm, 4), jnp.tile(hg_onorm, 4)]).reshape(1, D_MODEL)
    row = lambda: pl.BlockSpec((ROW_TILE, D_MODEL), lambda i: (i, 0))
    return pl.pallas_call(
        _post_kernel,
        grid=(t // ROW_TILE,),
        in_specs=[row(), row(), row(), row(),
                  _const_spec((D_MODEL, D_MODEL)), _const_spec((1, D_MODEL)),
                  _const_spec((D_MODEL, D_MODEL))],
        out_specs=row(),
        out_shape=jax.ShapeDtypeStruct((t, D_MODEL), F32),
        compiler_params=_params(1),
        name=f"post_l{layer}",
    )(o_f, o_b, gates, h, ones, onw, w_out.astype(BF16))


def _ple_tail(h2, p, nple, wpp_ref, wpg_ref, nfinal, final):
    gate = _sigmoid(_dot(_rmsnorm(h2, nple).astype(BF16), wpg_ref[...]))
    h3 = h2 + _dot(p.astype(BF16), wpp_ref[...]) * gate
    if final:
        h3 = _rmsnorm(h3, nfinal)
    return h3


def _dense_kernel(final, h_ref, p_ref, nffn_ref, wg_ref, wu_ref, wd_ref, nple_ref,
                  wpp_ref, wpg_ref, nfin_ref, out_ref):
    h1 = h_ref[...]
    v = _rmsnorm(h1, nffn_ref[...]).astype(BF16)
    acc = jnp.zeros_like(h1)
    for c in range(0, D_FF_DENSE, FF_CHUNK):
        a = _dot(v, wg_ref[:, c:c + FF_CHUNK])
        b = _dot(v, wu_ref[:, c:c + FF_CHUNK])
        acc = acc + _dot((a * _sigmoid(a) * b).astype(BF16), wd_ref[c:c + FF_CHUNK, :])
    out_ref[...] = _ple_tail(h1 + acc, p_ref[...], nple_ref[...], wpp_ref, wpg_ref,
                             nfin_ref[...], final)


def _dense_ffn(layer, final, h, p, norm_ffn, wg, wu, wd, norm_ple, w_pp, w_pg, norm_final):
    t = h.shape[0]
    vec = lambda w: w.reshape(1, D_MODEL)
    return pl.pallas_call(
        functools.partial(_dense_kernel, final),
        grid=(t // ROW_TILE,),
        in_specs=[pl.BlockSpec((ROW_TILE, D_MODEL), lambda i: (i, 0)),
                  pl.BlockSpec((ROW_TILE, PLE_DIM), lambda i: (i, 0)),
                  _const_spec((1, D_MODEL)),
                  _const_spec((D_MODEL, D_FF_DENSE)), _const_spec((D_MODEL, D_FF_DENSE)),
                  _const_spec((D_FF_DENSE, D_MODEL)),
                  _const_spec((1, D_MODEL)),
                  _const_spec((PLE_DIM, D_MODEL)), _const_spec((D_MODEL, D_MODEL)),
                  _const_spec((1, D_MODEL))],
        out_specs=pl.BlockSpec((ROW_TILE, D_MODEL), lambda i: (i, 0)),
        out_shape=jax.ShapeDtypeStruct((t, D_MODEL), F32),
        compiler_params=_params(1),
        name=f"dense_l{layer}",
    )(h, p, vec(norm_ffn), wg.astype(BF16), wu.astype(BF16), wd.astype(BF16),
      vec(norm_ple), w_pp.astype(BF16), w_pg.astype(BF16), vec(norm_final))


META_G1, META_G2, META_E1, META_E2, META_R1, META_R2 = range(6)
HALF = D_MODEL // 2
HI_MASK = 0xFFFF0000


def _pack_rows(x):
    bits = pltpu.bitcast(x.astype(BF16).astype(F32), jnp.uint32)
    return (bits[:, :HALF] >> 16) | (bits[:, HALF:] & jnp.uint32(HI_MASK))


def _unpack_rows(w):
    lo = pltpu.bitcast(w << 16, F32)
    hi = pltpu.bitcast(w & jnp.uint32(HI_MASK), F32)
    return jnp.concatenate([lo, hi], axis=1)


def _router_kernel(h_ref, nffn_ref, wr_ref, tri_ref, xp_ref, meta_ref, cnt_ref, base_ref):
    @pl.when(pl.program_id(0) == 0)
    def _():
        base_ref[...] = jnp.zeros_like(base_ref)

    vf = _rmsnorm(h_ref[...], nffn_ref[...])
    xp_ref[...] = _pack_rows(vf)
    v_hi = vf.astype(BF16)
    v_lo = (vf - v_hi.astype(F32)).astype(BF16)
    w = wr_ref[...]
    w_hi = w.astype(BF16)
    w_lo = (w - w_hi.astype(F32)).astype(BF16)
    logits = _dot(v_hi, w_hi) + _dot(v_lo, w_hi) + _dot(v_hi, w_lo)
    lane = lax.broadcasted_iota(jnp.int32, logits.shape, 1)
    neg = jnp.float32(-jnp.inf)
    lg = jnp.where(lane < N_EXPERTS, logits, neg)
    m1 = jnp.max(lg, axis=-1, keepdims=True)
    i1 = jnp.min(jnp.where(lg == m1, lane, LANES), axis=-1, keepdims=True)
    first = lane == i1
    lg2 = jnp.where(first, neg, lg)
    m2 = jnp.max(lg2, axis=-1, keepdims=True)
    i2 = jnp.min(jnp.where(lg2 == m2, lane, LANES), axis=-1, keepdims=True)
    second = lane == i2
    e = jnp.exp(m2 - m1)
    g1 = 1.0 / (1.0 + e)
    g2 = e * g1
    cnt = jnp.where(first | second, 1.0, 0.0)
    rank = base_ref[...] + _dot(tri_ref[...], cnt.astype(BF16))
    r1 = jnp.sum(jnp.where(first, rank, 0.0), axis=-1, keepdims=True)
    r2 = jnp.sum(jnp.where(second, rank, 0.0), axis=-1, keepdims=True)
    base_ref[...] = base_ref[...] + jnp.sum(cnt, axis=0, keepdims=True)
    cnt_ref[...] = base_ref[...]
    meta = jnp.zeros(logits.shape, F32)
    for col, val in ((META_G1, g1), (META_G2, g2), (META_E1, i1.astype(F32)),
                     (META_E2, i2.astype(F32)), (META_R1, r1), (META_R2, r2)):
        meta = jnp.where(lane == col, val, meta)
    meta_ref[...] = meta


def _router(layer, h, norm_ffn, w_router):
    t = h.shape[0]
    wr = jnp.pad(w_router, ((0, 0), (0, LANES - N_EXPERTS)))
    ids = jnp.arange(ROW_TILE)
    tri = (ids[None, :] < ids[:, None]).astype(BF16)
    return pl.pallas_call(
        _router_kernel,
        grid=(t // ROW_TILE,),
        in_specs=[pl.BlockSpec((ROW_TILE, D_MODEL), lambda i: (i, 0)),
                  _const_spec((1, D_MODEL)), _const_spec((D_MODEL, LANES)),
                  _const_spec((ROW_TILE, ROW_TILE))],
        out_specs=[pl.BlockSpec((ROW_TILE, HALF), lambda i: (i, 0)),
                   pl.BlockSpec((ROW_TILE, LANES), lambda i: (i, 0)),
                   pl.BlockSpec((1, LANES), lambda i: (0, 0))],
        out_shape=[jax.ShapeDtypeStruct((t, HALF), jnp.uint32),
                   jax.ShapeDtypeStruct((t, LANES), F32),
                   jax.ShapeDtypeStruct((1, LANES), F32)],
        scratch_shapes=[pltpu.VMEM((1, LANES), F32)],
        compiler_params=_params(1),
        name=f"router_l{layer}",
    )(h, norm_ffn.reshape(1, D_MODEL), wr, tri)


SC_CORES, SC_SUBCORES = 2, 16
SC_WORKERS = SC_CORES * SC_SUBCORES
SC_WINDOW = 128


def _sc_gather(table, idx):
    n_out, d = idx.shape[0], table.shape[1]
    per_worker = n_out // SC_WORKERS
    n_win = per_worker // SC_WINDOW
    assert n_win * SC_WINDOW * SC_WORKERS == n_out
    mesh = plsc.VectorSubcoreMesh(core_axis_name="c", subcore_axis_name="s")

    def body(table_hbm, idx_hbm, out_hbm, idx_v, rows_v, sem):
        wid = lax.axis_index("s") * SC_CORES + lax.axis_index("c")

        @pl.loop(0, n_win)
        def _(w):
            base = pl.multiple_of(wid * per_worker + w * SC_WINDOW, SC_WINDOW)
            pltpu.sync_copy(idx_hbm.at[pl.ds(base, SC_WINDOW)], idx_v)
            pltpu.async_copy(table_hbm.at[idx_v], rows_v, sem).wait()
            pltpu.sync_copy(rows_v, out_hbm.at[pl.ds(base, SC_WINDOW)])

    return pl.kernel(
        body, mesh=mesh,
        out_type=jax.ShapeDtypeStruct((n_out, d), table.dtype),
        scratch_types=[pltpu.VMEM((SC_WINDOW,), jnp.int32),
                       pltpu.VMEM((SC_WINDOW, d), table.dtype),
                       pltpu.SemaphoreType.DMA],
    )(table, idx)


EXP_ROWS = 512


def _experts_kernel(be_ref, na_ref, xs_ref, wg_ref, wu_ref, wd_ref, ys_ref, x_ref, acc_ref):
    b = pl.program_id(0)
    f = pl.program_id(1)

    @pl.when(b < na_ref[0])
    def _():
        @pl.when(f == 0)
        def _():
            x_ref[...] = _unpack_rows(xs_ref[...]).astype(BF16)
            acc_ref[...] = jnp.zeros_like(acc_ref)

        x = x_ref[...]
        a = _dot(x, wg_ref[0])
        u = _dot(x, wu_ref[0])
        acc_ref[...] += _dot((a * _sigmoid(a) * u).astype(BF16), wd_ref[0])

        @pl.when(f == pl.num_programs(1) - 1)
        def _():
            ys_ref[...] = _pack_rows(acc_ref[...])


def _experts(layer, xs, blk_expert, n_active, wg, wu, wd):
    rows = xs.shape[0]
    nf = D_FF_EXPERT // EXP_FF_CHUNK
    fsel = lambda b, f, na: jnp.where(b < na[0], f, nf - 1)
    grid_spec = pltpu.PrefetchScalarGridSpec(
        num_scalar_prefetch=2,
        grid=(rows // EXP_ROWS, nf),
        in_specs=[pl.BlockSpec((EXP_ROWS, HALF), lambda b, f, be, na: (b, 0)),
                  pl.BlockSpec((1, D_MODEL, EXP_FF_CHUNK),
                               lambda b, f, be, na: (be[b], 0, fsel(b, f, na))),
                  pl.BlockSpec((1, D_MODEL, EXP_FF_CHUNK),
                               lambda b, f, be, na: (be[b], 0, fsel(b, f, na))),
                  pl.BlockSpec((1, EXP_FF_CHUNK, D_MODEL),
                               lambda b, f, be, na: (be[b], fsel(b, f, na), 0))],
        out_specs=pl.BlockSpec((EXP_ROWS, HALF), lambda b, f, be, na: (b, 0)),
        scratch_shapes=[pltpu.VMEM((EXP_ROWS, D_MODEL), BF16),
                        pltpu.VMEM((EXP_ROWS, D_MODEL), F32)])
    return pl.pallas_call(
        _experts_kernel,
        grid_spec=grid_spec,
        out_shape=jax.ShapeDtypeStruct((rows, HALF), jnp.uint32),
        compiler_params=_params(2),
        name=f"experts_l{layer}",
    )(blk_expert, n_active, xs, wg.astype(BF16), wu.astype(BF16), wd.astype(BF16))


def _combine_kernel(final, h_ref, y1_ref, y2_ref, meta_ref, p_ref, nple_ref, wpp_ref, wpg_ref,
                    nfin_ref, out_ref):
    meta = meta_ref[...]
    g1 = meta[:, META_G1:META_G1 + 1]
    g2 = meta[:, META_G2:META_G2 + 1]
    h2 = h_ref[...] + g1 * _unpack_rows(y1_ref[...]) + g2 * _unpack_rows(y2_ref[...])
    out_ref[...] = _ple_tail(h2, p_ref[...], nple_ref[...], wpp_ref, wpg_ref, nfin_ref[...], final)


def _combine(layer, final, h, yg, meta, p, norm_ple, w_pp, w_pg, norm_final):
    t = h.shape[0]
    nt = t // ROW_TILE
    vec = lambda w: w.reshape(1, D_MODEL)
    return pl.pallas_call(
        functools.partial(_combine_kernel, final),
        grid=(nt,),
        in_specs=[pl.BlockSpec((ROW_TILE, D_MODEL), lambda i: (i, 0)),
                  pl.BlockSpec((ROW_TILE, HALF), lambda i: (i, 0)),
                  pl.BlockSpec((ROW_TILE, HALF), lambda i: (i + nt, 0)),
                  pl.BlockSpec((ROW_TILE, LANES), lambda i: (i, 0)),
                  pl.BlockSpec((ROW_TILE, PLE_DIM), lambda i: (i, 0)),
                  _const_spec((1, D_MODEL)),
                  _const_spec((PLE_DIM, D_MODEL)), _const_spec((D_MODEL, D_MODEL)),
                  _const_spec((1, D_MODEL))],
        out_specs=pl.BlockSpec((ROW_TILE, D_MODEL), lambda i: (i, 0)),
        out_shape=jax.ShapeDtypeStruct((t, D_MODEL), F32),
        compiler_params=_params(1),
        name=f"combine_l{layer}",
    )(h, yg, yg, meta, p, vec(norm_ple), w_pp.astype(BF16), w_pg.astype(BF16), vec(norm_final))


def _moe(layer, final, h1, p, norm_ffn, w_router, wg, wu, wd, norm_ple, w_pp, w_pg, norm_final):
    t = h1.shape[0]
    xp, meta, counts = _router(layer, h1, norm_ffn, w_router)
    cnt = counts[0, :N_EXPERTS].astype(jnp.int32)
    padded = ((cnt + EXP_ROWS - 1) // EXP_ROWS) * EXP_ROWS
    ends = jnp.cumsum(padded)
    offs = ends - padded
    e1 = meta[:, META_E1].astype(jnp.int32)
    e2 = meta[:, META_E2].astype(jnp.int32)
    pos1 = offs[e1] + meta[:, META_R1].astype(jnp.int32)
    pos2 = offs[e2] + meta[:, META_R2].astype(jnp.int32)
    rows = 2 * t + N_EXPERTS * EXP_ROWS
    tok = jnp.arange(t, dtype=jnp.int32)
    tok_sorted = jnp.zeros((rows,), jnp.int32).at[pos1].set(tok).at[pos2].set(tok)
    blk_start = jnp.arange(rows // EXP_ROWS, dtype=jnp.int32) * EXP_ROWS
    blk_expert = jnp.minimum(
        jnp.sum(blk_start[:, None] >= ends[None, :], axis=1), N_EXPERTS - 1).astype(jnp.int32)
    n_active = (ends[-1:] // EXP_ROWS).astype(jnp.int32)

    xs = _sc_gather(xp, tok_sorted)
    ys = _experts(layer, xs, blk_expert, n_active, wg, wu, wd)
    yg = _sc_gather(ys, jnp.concatenate([pos1, pos2]))
    return _combine(layer, final, h1, yg, meta, p, norm_ple, w_pp, w_pg, norm_final)


def kernel(x, p, w_in, gla_wg2, gla_bg, hg_lb_logits, gla_onorm, hg_onorm, w_out, norm_mix,
           norm_ffn, w_dense_gate, w_dense_up, w_dense_down, w_router, w_exp_gate, w_exp_up,
           w_exp_down, w_ple_proj, w_ple_gate, norm_ple, norm_final):
    bsz, seq, _ = x.shape
    t = bsz * seq
    h = x.reshape(t, D_MODEL)
    for i in range(DEPTH):
        final = i == DEPTH - 1
        q, kf, kb, v, gates, gfh, gfl, gbh, gbl = _proj(
            i, h, norm_mix[i], w_in[i], gla_wg2[i], gla_bg[i], hg_lb_logits)
        o_f, o_b = _scan(i, bsz, q, kf, kb, v, gfh, gfl, gbh, gbl)
        h1 = _post(i, o_f.reshape(t, D_MODEL), o_b.reshape(t, D_MODEL), gates, h,
                   gla_onorm[i], hg_onorm[i], w_out[i])
        p_i = p[i].reshape(t, PLE_DIM)
        j = i // 2
        if i % 2 == 0:
            h = _dense_ffn(i, final, h1, p_i, norm_ffn[i], w_dense_gate[j], w_dense_up[j],
                           w_dense_down[j], norm_ple[i], w_ple_proj[i], w_ple_gate[i],
                           norm_final)
        else:
            h = _moe(i, final, h1, p_i, norm_ffn[i], w_router[j], w_exp_gate[j], w_exp_up[j],
                     w_exp_down[j], norm_ple[i], w_ple_proj[i], w_ple_gate[i], norm_final)
    return h.reshape(bsz, seq, D_MODEL)
```

```python
import functools

import jax
import jax.numpy as jnp
from jax import lax
from jax.experimental import pallas as pl
from jax.experimental.pallas import tpu as pltpu
from jax.experimental.pallas import tpu_sc as plsc

F32 = jnp.float32
BF16 = jnp.bfloat16

D_MODEL = 1024
DEPTH = 4
N_HEADS = 8
HEAD_W = 128
GLA_DK = 64
GLA_RANK = 16
GLA_GATE_NORM = 16.0
HG_DK = 128
HG_K = 512
D_FF_DENSE = 2816
N_EXPERTS = 8
D_FF_EXPERT = 3584
PLE_DIM = 256
EPS = 1e-6
HEAD_EPS = 1e-5
F_MIN = 1e-6

CHUNK = 64
SUB = 16
EXP2_CLAMP = 86.0
LANES = 128
VMEM_LIMIT = 56 * 1024 * 1024

MIX_TILE = 256
ROW_TILE = 512
FF_CHUNK = 256
EXP_FF_CHUNK = 512

COL_HQ = 0
COL_HF = 512
COL_GLR = 1536
COL_GQ = 1792
COL_GK = 2304
COL_V = 2816
COL_GATE = 3840
NP_COLS = 4864
COL_GROUPS = ((0, COL_GQ), (COL_GQ, COL_V), (COL_V, NP_COLS))
LOG2E = 1.4426950408889634


def _dot(a, b):
    return jnp.dot(a, b, preferred_element_type=F32)


def _dot_nt(a, b):
    return lax.dot_general(a, b, (((1,), (1,)), ((), ())), preferred_element_type=F32)


def _dot_tn(a, b):
    return lax.dot_general(a, b, (((0,), (0,)), ((), ())), preferred_element_type=F32)


def _sigmoid(x):
    return 1.0 / (1.0 + jnp.exp(-x))


def _rmsnorm(x, w):
    ms = jnp.mean(x * x, axis=-1, keepdims=True)
    return x * lax.rsqrt(ms + EPS) * w


def _const_spec(shape):
    nd = len(shape)
    return pl.BlockSpec(shape, lambda *_: (0,) * nd, pipeline_mode=pl.Buffered(1))


def _params(n_grid):
    return pltpu.CompilerParams(
        dimension_semantics=("arbitrary",) * n_grid, vmem_limit_bytes=VMEM_LIMIT)


def _proj_kernel(layer, h_ref, nw_ref, w_ref, wg2_ref, bg_ref, lbl_ref,
                 q_ref, kf_ref, kb_ref, v_ref, gate_ref, gfh_ref, gfl_ref, gbh_ref, gbl_ref,
                 u_ref, p1_ref, p2_ref, p3_ref):
    u_ref[...] = _rmsnorm(h_ref[...], nw_ref[...]).astype(BF16)
    for p_ref, (c0, c1) in zip((p1_ref, p2_ref, p3_ref), COL_GROUPS):
        p_ref[...] = _dot(u_ref[...], w_ref[:, c0:c1])

    def split(hi_ref, lo_ref, cols, g):
        hi = g.astype(BF16)
        hi_ref[:, cols] = hi
        lo_ref[:, cols] = (g - hi.astype(F32)).astype(BF16)

    hq = p1_ref[:, COL_HQ:COL_HQ + 512]
    q_ref[:, 512:1024] = (hq * _sigmoid(hq) * (HG_DK ** -0.5)).astype(BF16)
    for direction, (k_ref, hi_ref, lo_ref) in enumerate(
            ((kf_ref, gfh_ref, gfl_ref), (kb_ref, gbh_ref, gbl_ref))):
        c0 = COL_GLR + direction * LANES
        x = _dot(p1_ref[:, c0:c0 + LANES].astype(BF16), wg2_ref[direction].astype(BF16))
        x = x + bg_ref[direction]
        log_sig = jnp.minimum(x, 0.0) - jnp.log(1.0 + jnp.exp(-jnp.abs(x)))
        split(hi_ref, lo_ref, slice(0, 512), log_sig * (LOG2E / GLA_GATE_NORM))
        rows = [lbl_ref[2 * d + direction:2 * d + direction + 1, :] for d in range(DEPTH)]
        mx = functools.reduce(jnp.maximum, rows)
        ex = [jnp.exp(r - mx) for r in rows]
        lb = sum(ex[1:layer + 1], jnp.zeros_like(mx)) / sum(ex)
        c0 = COL_HF + direction * HG_K
        z = p1_ref[:, c0:c0 + HG_K]
        e = jnp.exp(-jnp.abs(z))
        s_big = 1.0 / (1.0 + e)
        s_small = e * s_big
        sig = jnp.where(z >= 0, s_big, s_small)
        nsig = jnp.where(z >= 0, s_small, s_big)
        f = lb + (1.0 - lb) * sig
        split(hi_ref, lo_ref, slice(512, 1024), jnp.log(jnp.maximum(f, F_MIN)) * LOG2E)
        k_ref[:, 512:1024] = ((1.0 - lb) * nsig).astype(BF16)

    q_ref[:, 0:512] = (p2_ref[:, 0:512] * (GLA_DK ** -0.5)).astype(BF16)
    gk = p2_ref[:, 512:1024].astype(BF16)
    kf_ref[:, 0:512] = gk
    kb_ref[:, 0:512] = gk
    v_ref[...] = p3_ref[:, 0:1024].astype(BF16)
    gate_ref[...] = p3_ref[:, 1024:2048].astype(BF16)


def _pad_heads(w, n_heads, width):
    lead = w.shape[:-1]
    w = w.reshape(lead + (n_heads, width))
    w = jnp.pad(w, [(0, 0)] * len(lead) + [(0, 0), (0, HEAD_W - width)])
    return w.reshape(lead + (n_heads * HEAD_W,))


def _proj(layer, h, norm_w, w_in, wg2, bg, lb_logits):
    t = h.shape[0]
    gq, gk, gv, gr, glr_f, glr_b, hq, hf_f, hf_b, hv, hr = jnp.split(
        w_in, [256, 512, 1024, 1536, 1552, 1568, 2080, 2592, 3104, 3616], axis=-1)
    pad_r = lambda w: jnp.pad(w, ((0, 0), (0, LANES - GLA_RANK)))
    w = jnp.concatenate([hq, hf_f, hf_b, pad_r(glr_f), pad_r(glr_b), _pad_heads(gq, 4, GLA_DK),
                         _pad_heads(gk, 4, GLA_DK), gv, hv, gr, hr], -1).astype(BF16)
    wg2_p = jnp.pad(_pad_heads(wg2, 4, GLA_DK), ((0, 0), (0, LANES - GLA_RANK), (0, 0)))
    bg_p = _pad_heads(bg, 4, GLA_DK).reshape(2, 1, 512)
    lbl = lb_logits.reshape(DEPTH * 2, HG_K)
    row = lambda: pl.BlockSpec((ROW_TILE, D_MODEL), lambda i: (i, 0))
    return pl.pallas_call(
        functools.partial(_proj_kernel, layer),
        grid=(t // ROW_TILE,),
        in_specs=[row(), _const_spec((1, D_MODEL)), _const_spec((D_MODEL, NP_COLS)),
                  _const_spec((2, LANES, 512)), _const_spec((2, 1, 512)),
                  _const_spec((DEPTH * 2, HG_K))],
        out_specs=[row() for _ in range(9)],
        out_shape=[jax.ShapeDtypeStruct((t, D_MODEL), BF16) for _ in range(9)],
        scratch_shapes=[pltpu.VMEM((ROW_TILE, D_MODEL), BF16)] + [
            pltpu.VMEM((ROW_TILE, c1 - c0), F32) for c0, c1 in COL_GROUPS],
        compiler_params=_params(1),
        name=f"proj_l{layer}",
    )(h, norm_w.reshape(1, D_MODEL), w, wg2_p, bg_p, lbl)


A_GROUPS = 3
A_WIDTH = A_GROUPS * HEAD_W


def _stage_a(rev, q_ref, k_ref, gh_ref, gl_ref, r0, tri, bufs, slot):
    qa, ka, qd, kd, qi, ks, et, _ = bufs
    rows = slice(r0, r0 + CHUNK)
    cum = _dot(tri, gh_ref[0, rows, :]) + _dot(tri, gl_ref[0, rows, :])
    q = q_ref[0, rows, :].astype(F32)
    k = k_ref[0, rows, :].astype(F32)

    def put(ref, rows, group, val):
        for n in range(N_HEADS):
            c0 = n * A_WIDTH + group * HEAD_W
            ref[slot, rows, c0:c0 + HEAD_W] = val[:, n * HEAD_W:(n + 1) * HEAD_W]

    for group, (half, a) in enumerate(((2 * SUB, 0), (SUB, 0), (SUB, 2 * SUB))):
        lo, hi = slice(a, a + half), slice(a + half, a + 2 * half)
        if not rev:
            ref, k_rows, q_rows = cum[a + half - 1:a + half], lo, hi
        else:
            ref, q_rows, k_rows = cum[a + half:a + half + 1], lo, hi
        put(qa, q_rows, group, (q[q_rows] * jnp.exp2(cum[q_rows] - ref)).astype(BF16))
        put(ka, k_rows, group, (k[k_rows] * jnp.exp2(ref - cum[k_rows])).astype(BF16))

    mids = []
    for a in range(0, CHUNK, SUB):
        m = 0.5 * (cum[a:a + 1] + cum[a + SUB - 1:a + SUB])
        mids.append(jnp.broadcast_to(m, (SUB, D_MODEL)))
    dd = cum - jnp.concatenate(mids, 0)
    qd[slot] = (q * jnp.exp2(jnp.clip(dd, -EXP2_CLAMP, EXP2_CLAMP))).astype(BF16)
    kd[slot] = (k * jnp.exp2(jnp.clip(-dd, -EXP2_CLAMP, EXP2_CLAMP))).astype(BF16)

    tot = cum[0:1] if rev else cum[CHUNK - 1:CHUNK]
    qi[slot] = (q * jnp.exp2(cum)).astype(BF16)
    ks[slot] = (k * jnp.exp2(tot - cum)).astype(BF16)
    et[slot] = jnp.exp2(tot)


def _stage_b(bufs, slot, v_ref, st_ref, o_ref, r0, diag):
    qa, ka, qd, kd, qi, ks, et, ab = bufs
    for n in range(N_HEADS):
        wide = slice(n * A_WIDTH, (n + 1) * A_WIDTH)
        head = slice(n * HEAD_W, (n + 1) * HEAD_W)
        scores = _dot_nt(qa[slot, :, wide], ka[slot, :, wide])
        scores = scores + jnp.where(diag, _dot_nt(qd[slot, :, head], kd[slot, :, head]), 0.0)
        ab[slot, n] = scores.astype(BF16)
        st = st_ref[n]
        o_ref[0, r0:r0 + CHUNK, head] = _dot_nt(qi[slot, :, head], st.astype(BF16))
        st_ref[n] = st * et[slot, :, head] + _dot_tn(v_ref[0, r0:r0 + CHUNK, head],
                                                    ks[slot, :, head])


def _stage_c(bufs, slot, v_ref, o_ref, r0):
    ab = bufs[7]
    for n in range(N_HEADS):
        head = slice(n * HEAD_W, (n + 1) * HEAD_W)
        o_ref[0, r0:r0 + CHUNK, head] += _dot(ab[slot, n], v_ref[0, r0:r0 + CHUNK, head])


def _scan_kernel(qf_ref, qb_ref, kf_ref, kb_ref, vf_ref, vb_ref, gfh_ref, gfl_ref, gbh_ref,
                 gbl_ref, of_ref, ob_ref, sf_ref, sb_ref, *buf_refs):
    n_chunks = MIX_TILE // CHUNK
    bufs_f, bufs_b = buf_refs[:len(buf_refs) // 2], buf_refs[len(buf_refs) // 2:]

    @pl.when(pl.program_id(1) == 0)
    def _():
        sf_ref[...] = jnp.zeros_like(sf_ref)
        sb_ref[...] = jnp.zeros_like(sb_ref)
        for bufs in (bufs_f, bufs_b):
            bufs[0][...] = jnp.zeros_like(bufs[0])
            bufs[1][...] = jnp.zeros_like(bufs[1])

    ri = lax.broadcasted_iota(jnp.int32, (CHUNK, CHUNK), 0)
    ci = lax.broadcasted_iota(jnp.int32, (CHUNK, CHUNK), 1)
    tri_f = (ci <= ri).astype(BF16)
    tri_b = (ci >= ri).astype(BF16)
    same16 = (ri // SUB) == (ci // SUB)
    diag_f = same16 & (ci <= ri)
    diag_b = same16 & (ci >= ri)
    row_f = lambda c: c * CHUNK
    row_b = lambda c: (n_chunks - 1 - c) * CHUNK

    def stage_a(c):
        _stage_a(False, qf_ref, kf_ref, gfh_ref, gfl_ref, row_f(c), tri_f, bufs_f, c % 2)
        _stage_a(True, qb_ref, kb_ref, gbh_ref, gbl_ref, row_b(c), tri_b, bufs_b, c % 2)

    stage_a(0)
    for c in range(n_chunks):
        if c + 1 < n_chunks:
            stage_a(c + 1)
        _stage_b(bufs_f, c % 2, vf_ref, sf_ref, of_ref, row_f(c), diag_f)
        _stage_b(bufs_b, c % 2, vb_ref, sb_ref, ob_ref, row_b(c), diag_b)
        _stage_c(bufs_f, c % 2, vf_ref, of_ref, row_f(c))
        _stage_c(bufs_b, c % 2, vb_ref, ob_ref, row_b(c))


def _scan_bufs():
    wide = pltpu.VMEM((2, CHUNK, N_HEADS * A_WIDTH), BF16)
    narrow = pltpu.VMEM((2, CHUNK, D_MODEL), BF16)
    return [wide, wide, narrow, narrow, narrow, narrow,
            pltpu.VMEM((2, 1, D_MODEL), F32),
            pltpu.VMEM((2, N_HEADS, CHUNK, CHUNK), BF16)]


def _scan(layer, bsz, q, kf, kb, v, gfh, gfl, gbh, gbl):
    seq = q.shape[0] // bsz
    nt = seq // MIX_TILE
    to3 = lambda a: a.reshape(bsz, seq, D_MODEL)
    tile = lambda idx: pl.BlockSpec((1, MIX_TILE, D_MODEL), idx)
    fwd = lambda b, j: (b, j, 0)
    bwd = lambda b, j: (b, nt - 1 - j, 0)
    state = pltpu.VMEM((N_HEADS, HEAD_W, HEAD_W), F32)
    return pl.pallas_call(
        _scan_kernel,
        grid=(bsz, nt),
        in_specs=[tile(fwd), tile(bwd), tile(fwd), tile(bwd), tile(fwd), tile(bwd),
                  tile(fwd), tile(fwd), tile(bwd), tile(bwd)],
        out_specs=[tile(fwd), tile(bwd)],
        out_shape=[jax.ShapeDtypeStruct((bsz, seq, D_MODEL), F32),
                   jax.ShapeDtypeStruct((bsz, seq, D_MODEL), F32)],
        scratch_shapes=[state, state] + _scan_bufs() + _scan_bufs(),
        compiler_params=_params(2),
        name=f"scan_l{layer}",
    )(to3(q), to3(q), to3(kf), to3(kb), to3(v), to3(v), to3(gfh), to3(gfl), to3(gbh), to3(gbl))


def _post_kernel(of_ref, ob_ref, gate_ref, h_ref, ones_ref, onw_ref, wout_ref, out_ref):
    o = of_ref[...] + ob_ref[...]
    ms = _dot((o * o).astype(BF16), ones_ref[...]) * (1.0 / HEAD_W)
    y = o * lax.rsqrt(ms + HEAD_EPS) * onw_ref[...]
    g = gate_ref[...].astype(F32)
    y = y * (g * _sigmoid(g))
    out_ref[...] = h_ref[...] + _dot(y.astype(BF16), wout_ref[...])


def _post(layer, o_f, o_b, gates, h, gla_onorm, hg_onorm, w_out):
    t = h.shape[0]
    head_id = jnp.arange(D_MODEL) // HEAD_W
    ones = (head_id[:, None] == head_id[None, :]).astype(BF16)
    onw = jnp.concatenate([jnp.tile(gla_onorm, 4), jnp.tile(hg_onorm, 4)]).reshape(1, D_MODEL)
    row = lambda: pl.BlockSpec((ROW_TILE, D_MODEL), lambda i: (i, 0))
    return pl.pallas_call(
        _post_kernel,
        grid=(t // ROW_TILE,),
        in_specs=[row(), row(), row(), row(),
                  _const_spec((D_MODEL, D_MODEL)), _const_spec((1, D_MODEL)),
                  _const_spec((D_MODEL, D_MODEL))],
        out_specs=row(),
        out_shape=jax.ShapeDtypeStruct((t, D_MODEL), F32),
        compiler_params=_params(1),
        name=f"post_l{layer}",
    )(o_f, o_b, gates, h, ones, onw, w_out.astype(BF16))


def _ple_tail(h2, p, nple, wpp_ref, wpg_ref, nfinal, final):
    gate = _sigmoid(_dot(_rmsnorm(h2, nple).astype(BF16), wpg_ref[...]))
    h3 = h2 + _dot(p.astype(BF16), wpp_ref[...]) * gate
    if final:
        h3 = _rmsnorm(h3, nfinal)
    return h3


def _dense_kernel(final, h_ref, p_ref, nffn_ref, wg_ref, wu_ref, wd_ref, nple_ref,
                  wpp_ref, wpg_ref, nfin_ref, out_ref):
    h1 = h_ref[...]
    v = _rmsnorm(h1, nffn_ref[...]).astype(BF16)
    acc = jnp.zeros_like(h1)
    for c in range(0, D_FF_DENSE, FF_CHUNK):
        a = _dot(v, wg_ref[:, c:c + FF_CHUNK])
        b = _dot(v, wu_ref[:, c:c + FF_CHUNK])
        acc = acc + _dot((a * _sigmoid(a) * b).astype(BF16), wd_ref[c:c + FF_CHUNK, :])
    out_ref[...] = _ple_tail(h1 + acc, p_ref[...], nple_ref[...], wpp_ref, wpg_ref,
                             nfin_ref[...], final)


def _dense_ffn(layer, final, h, p, norm_ffn, wg, wu, wd, norm_ple, w_pp, w_pg, norm_final):
    t = h.shape[0]
    vec = lambda w: w.reshape(1, D_MODEL)
    return pl.pallas_call(
        functools.partial(_dense_kernel, final),
        grid=(t // ROW_TILE,),
        in_specs=[pl.BlockSpec((ROW_TILE, D_MODEL), lambda i: (i, 0)),
                  pl.BlockSpec((ROW_TILE, PLE_DIM), lambda i: (i, 0)),
                  _const_spec((1, D_MODEL)),
                  _const_spec((D_MODEL, D_FF_DENSE)), _const_spec((D_MODEL, D_FF_DENSE)),
                  _const_spec((D_FF_DENSE, D_MODEL)),
                  _const_spec((1, D_MODEL)),
                  _const_spec((PLE_DIM, D_MODEL)), _const_spec((D_MODEL, D_MODEL)),
                  _const_spec((1, D_MODEL))],
        out_specs=pl.BlockSpec((ROW_TILE, D_MODEL), lambda i: (i, 0)),
        out_shape=jax.ShapeDtypeStruct((t, D_MODEL), F32),
        compiler_params=_params(1),
        name=f"dense_l{layer}",
    )(h, p, vec(norm_ffn), wg.astype(BF16), wu.astype(BF16), wd.astype(BF16),
      vec(norm_ple), w_pp.astype(BF16), w_pg.astype(BF16), vec(norm_final))


META_G1, META_G2, META_E1, META_E2, META_R1, META_R2 = range(6)
HALF = D_MODEL // 2
HI_MASK = 0xFFFF0000


def _pack_rows(x):
    bits = pltpu.bitcast(x.astype(BF16).astype(F32), jnp.uint32)
    return (bits[:, :HALF] >> 16) | (bits[:, HALF:] & jnp.uint32(HI_MASK))


def _unpack_rows(w):
    lo = pltpu.bitcast(w << 16, F32)
    hi = pltpu.bitcast(w & jnp.uint32(HI_MASK), F32)
    return jnp.concatenate([lo, hi], axis=1)


def _router_kernel(h_ref, nffn_ref, wr_ref, tri_ref, xp_ref, meta_ref, cnt_ref, base_ref):
    @pl.when(pl.program_id(0) == 0)
    def _():
        base_ref[...] = jnp.zeros_like(base_ref)

    vf = _rmsnorm(h_ref[...], nffn_ref[...])
    xp_ref[...] = _pack_rows(vf)
    v_hi = vf.astype(BF16)
    v_lo = (vf - v_hi.astype(F32)).astype(BF16)
    w = wr_ref[...]
    w_hi = w.astype(BF16)
    w_lo = (w - w_hi.astype(F32)).astype(BF16)
    logits = _dot(v_hi, w_hi) + _dot(v_lo, w_hi) + _dot(v_hi, w_lo)
    lane = lax.broadcasted_iota(jnp.int32, logits.shape, 1)
    neg = jnp.float32(-jnp.inf)
    lg = jnp.where(lane < N_EXPERTS, logits, neg)
    m1 = jnp.max(lg, axis=-1, keepdims=True)
    i1 = jnp.min(jnp.where(lg == m1, lane, LANES), axis=-1, keepdims=True)
    first = lane == i1
    lg2 = jnp.where(first, neg, lg)
    m2 = jnp.max(lg2, axis=-1, keepdims=True)
    i2 = jnp.min(jnp.where(lg2 == m2, lane, LANES), axis=-1, keepdims=True)
    second = lane == i2
    e = jnp.exp(m2 - m1)
    g1 = 1.0 / (1.0 + e)
    g2 = e * g1
    cnt = jnp.where(first | second, 1.0, 0.0)
    rank = base_ref[...] + _dot(tri_ref[...], cnt.astype(BF16))
    r1 = jnp.sum(jnp.where(first, rank, 0.0), axis=-1, keepdims=True)
    r2 = jnp.sum(jnp.where(second, rank, 0.0), axis=-1, keepdims=True)
    base_ref[...] = base_ref[...] + jnp.sum(cnt, axis=0, keepdims=True)
    cnt_ref[...] = base_ref[...]
    meta = jnp.zeros(logits.shape, F32)
    for col, val in ((META_G1, g1), (META_G2, g2), (META_E1, i1.astype(F32)),
                     (META_E2, i2.astype(F32)), (META_R1, r1), (META_R2, r2)):
        meta = jnp.where(lane == col, val, meta)
    meta_ref[...] = meta


def _router(layer, h, norm_ffn, w_router):
    t = h.shape[0]
    wr = jnp.pad(w_router, ((0, 0), (0, LANES - N_EXPERTS)))
    ids = jnp.arange(ROW_TILE)
    tri = (ids[None, :] < ids[:, None]).astype(BF16)
    return pl.pallas_call(
        _router_kernel,
        grid=(t // ROW_TILE,),
        in_specs=[pl.BlockSpec((ROW_TILE, D_MODEL), lambda i: (i, 0)),
                  _const_spec((1, D_MODEL)), _const_spec((D_MODEL, LANES)),
                  _const_spec((ROW_TILE, ROW_TILE))],
        out_specs=[pl.BlockSpec((ROW_TILE, HALF), lambda i: (i, 0)),
                   pl.BlockSpec((ROW_TILE, LANES), lambda i: (i, 0)),
                   pl.BlockSpec((1, LANES), lambda i: (0, 0))],
        out_shape=[jax.ShapeDtypeStruct((t, HALF), jnp.uint32),
                   jax.ShapeDtypeStruct((t, LANES), F32),
                   jax.ShapeDtypeStruct((1, LANES), F32)],
        scratch_shapes=[pltpu.VMEM((1, LANES), F32)],
        compiler_params=_params(1),
        name=f"router_l{layer}",
    )(h, norm_ffn.reshape(1, D_MODEL), wr, tri)


SC_CORES, SC_SUBCORES = 2, 16
SC_WORKERS = SC_CORES * SC_SUBCORES
SC_WINDOW = 128


def _sc_gather(table, idx):
    n_out, d = idx.shape[0], table.shape[1]
    per_worker = n_out // SC_WORKERS
    n_win = per_worker // SC_WINDOW
    assert n_win * SC_WINDOW * SC_WORKERS == n_out
    mesh = plsc.VectorSubcoreMesh(core_axis_name="c", subcore_axis_name="s")

    def body(table_hbm, idx_hbm, out_hbm, idx_v, rows_v, sem):
        wid = lax.axis_index("s") * SC_CORES + lax.axis_index("c")

        @pl.loop(0, n_win)
        def _(w):
            base = pl.multiple_of(wid * per_worker + w * SC_WINDOW, SC_WINDOW)
            pltpu.sync_copy(idx_hbm.at[pl.ds(base, SC_WINDOW)], idx_v)
            pltpu.async_copy(table_hbm.at[idx_v], rows_v, sem).wait()
            pltpu.sync_copy(rows_v, out_hbm.at[pl.ds(base, SC_WINDOW)])

    return pl.kernel(
        body, mesh=mesh,
        out_type=jax.ShapeDtypeStruct((n_out, d), table.dtype),
        scratch_types=[pltpu.VMEM((SC_WINDOW,), jnp.int32),
                       pltpu.VMEM((SC_WINDOW, d), table.dtype),
                       pltpu.SemaphoreType.DMA],
    )(table, idx)


EXP_ROWS = 1024


def _experts_kernel(be_ref, na_ref, xs_ref, wg_ref, wu_ref, wd_ref, ys_ref, x_ref, acc_ref):
    b = pl.program_id(0)
    f = pl.program_id(1)

    @pl.when(b < na_ref[0])
    def _():
        @pl.when(f == 0)
        def _():
            x_ref[...] = _unpack_rows(xs_ref[...]).astype(BF16)
            acc_ref[...] = jnp.zeros_like(acc_ref)

        x = x_ref[...]
        a = _dot(x, wg_ref[0].astype(BF16))
        u = _dot(x, wu_ref[0].astype(BF16))
        acc_ref[...] += _dot((a * _sigmoid(a) * u).astype(BF16), wd_ref[0].astype(BF16))

        @pl.when(f == pl.num_programs(1) - 1)
        def _():
            ys_ref[...] = _pack_rows(acc_ref[...])


def _experts(layer, xs, blk_expert, n_active, wg, wu, wd):
    rows = xs.shape[0]
    nf = D_FF_EXPERT // EXP_FF_CHUNK
    fsel = lambda b, f, na: jnp.where(b < na[0], f, nf - 1)
    grid_spec = pltpu.PrefetchScalarGridSpec(
        num_scalar_prefetch=2,
        grid=(rows // EXP_ROWS, nf),
        in_specs=[pl.BlockSpec((EXP_ROWS, HALF), lambda b, f, be, na: (b, 0)),
                  pl.BlockSpec((1, D_MODEL, EXP_FF_CHUNK),
                               lambda b, f, be, na: (be[b], 0, fsel(b, f, na))),
                  pl.BlockSpec((1, D_MODEL, EXP_FF_CHUNK),
                               lambda b, f, be, na: (be[b], 0, fsel(b, f, na))),
                  pl.BlockSpec((1, EXP_FF_CHUNK, D_MODEL),
                               lambda b, f, be, na: (be[b], fsel(b, f, na), 0))],
        out_specs=pl.BlockSpec((EXP_ROWS, HALF), lambda b, f, be, na: (b, 0)),
        scratch_shapes=[pltpu.VMEM((EXP_ROWS, D_MODEL), BF16),
                        pltpu.VMEM((EXP_ROWS, D_MODEL), F32)])
    return pl.pallas_call(
        _experts_kernel,
        grid_spec=grid_spec,
        out_shape=jax.ShapeDtypeStruct((rows, HALF), jnp.uint32),
        compiler_params=_params(2),
        name=f"experts_l{layer}",
    )(blk_expert, n_active, xs, wg, wu, wd)


def _combine_kernel(final, h_ref, y1_ref, y2_ref, meta_ref, p_ref, nple_ref, wpp_ref, wpg_ref,
                    nfin_ref, out_ref):
    meta = meta_ref[...]
    g1 = meta[:, META_G1:META_G1 + 1]
    g2 = meta[:, META_G2:META_G2 + 1]
    h2 = h_ref[...] + g1 * _unpack_rows(y1_ref[...]) + g2 * _unpack_rows(y2_ref[...])
    out_ref[...] = _ple_tail(h2, p_ref[...], nple_ref[...], wpp_ref, wpg_ref, nfin_ref[...], final)


def _combine(layer, final, h, yg, meta, p, norm_ple, w_pp, w_pg, norm_final):
    t = h.shape[0]
    nt = t // ROW_TILE
    vec = lambda w: w.reshape(1, D_MODEL)
    return pl.pallas_call(
        functools.partial(_combine_kernel, final),
        grid=(nt,),
        in_specs=[pl.BlockSpec((ROW_TILE, D_MODEL), lambda i: (i, 0)),
                  pl.BlockSpec((ROW_TILE, HALF), lambda i: (i, 0)),
                  pl.BlockSpec((ROW_TILE, HALF), lambda i: (i + nt, 0)),
                  pl.BlockSpec((ROW_TILE, LANES), lambda i: (i, 0)),
                  pl.BlockSpec((ROW_TILE, PLE_DIM), lambda i: (i, 0)),
                  _const_spec((1, D_MODEL)),
                  _const_spec((PLE_DIM, D_MODEL)), _const_spec((D_MODEL, D_MODEL)),
                  _const_spec((1, D_MODEL))],
        out_specs=pl.BlockSpec((ROW_TILE, D_MODEL), lambda i: (i, 0)),
        out_shape=jax.ShapeDtypeStruct((t, D_MODEL), F32),
        compiler_params=_params(1),
        name=f"combine_l{layer}",
    )(h, yg, yg, meta, p, vec(norm_ple), w_pp.astype(BF16), w_pg.astype(BF16), vec(norm_final))


def _moe(layer, final, h1, p, norm_ffn, w_router, wg, wu, wd, norm_ple, w_pp, w_pg, norm_final):
    t = h1.shape[0]
    xp, meta, counts = _router(layer, h1, norm_ffn, w_router)
    cnt = counts[0, :N_EXPERTS].astype(jnp.int32)
    padded = ((cnt + EXP_ROWS - 1) // EXP_ROWS) * EXP_ROWS
    ends = jnp.cumsum(padded)
    offs = ends - padded
    e1 = meta[:, META_E1].astype(jnp.int32)
    e2 = meta[:, META_E2].astype(jnp.int32)
    pos1 = offs[e1] + meta[:, META_R1].astype(jnp.int32)
    pos2 = offs[e2] + meta[:, META_R2].astype(jnp.int32)
    rows = 2 * t + N_EXPERTS * EXP_ROWS
    tok = jnp.arange(t, dtype=jnp.int32)
    tok_sorted = jnp.zeros((rows,), jnp.int32).at[jnp.concatenate([pos1, pos2])].set(
        jnp.concatenate([tok, tok]), unique_indices=True)
    blk_start = jnp.arange(rows // EXP_ROWS, dtype=jnp.int32) * EXP_ROWS
    blk_expert = jnp.minimum(
        jnp.sum(blk_start[:, None] >= ends[None, :], axis=1), N_EXPERTS - 1).astype(jnp.int32)
    n_active = (ends[-1:] // EXP_ROWS).astype(jnp.int32)

    xs = _sc_gather(xp, tok_sorted)
    ys = _experts(layer, xs, blk_expert, n_active, wg, wu, wd)
    yg = _sc_gather(ys, jnp.concatenate([pos1, pos2]))
    return _combine(layer, final, h1, yg, meta, p, norm_ple, w_pp, w_pg, norm_final)


def kernel(x, p, w_in, gla_wg2, gla_bg, hg_lb_logits, gla_onorm, hg_onorm, w_out, norm_mix,
           norm_ffn, w_dense_gate, w_dense_up, w_dense_down, w_router, w_exp_gate, w_exp_up,
           w_exp_down, w_ple_proj, w_ple_gate, norm_ple, norm_final):
    bsz, seq, _ = x.shape
    t = bsz * seq
    h = x.reshape(t, D_MODEL)
    for i in range(DEPTH):
        final = i == DEPTH - 1
        q, kf, kb, v, gates, gfh, gfl, gbh, gbl = _proj(
            i, h, norm_mix[i], w_in[i], gla_wg2[i], gla_bg[i], hg_lb_logits)
        o_f, o_b = _scan(i, bsz, q, kf, kb, v, gfh, gfl, gbh, gbl)
        h1 = _post(i, o_f.reshape(t, D_MODEL), o_b.reshape(t, D_MODEL), gates, h,
                   gla_onorm[i], hg_onorm[i], w_out[i])
        p_i = p[i].reshape(t, PLE_DIM)
        j = i // 2
        if i % 2 == 0:
            h = _dense_ffn(i, final, h1, p_i, norm_ffn[i], w_dense_gate[j], w_dense_up[j],
                           w_dense_down[j], norm_ple[i], w_ple_proj[i], w_ple_gate[i],
                           norm_final)
        else:
            h = _moe(i, final, h1, p_i, norm_ffn[i], w_router[j], w_exp_gate[j], w_exp_up[j],
                     w_exp_down[j], norm_ple[i], w_ple_proj[i], w_ple_gate[i], norm_final)
    return h.reshape(bsz, seq, D_MODEL)
```

```python
import functools

import jax
import jax.numpy as jnp
from jax import lax
from jax.experimental import pallas as pl
from jax.experimental.pallas import tpu as pltpu
from jax.experimental.pallas import tpu_sc as plsc

F32 = jnp.float32
BF16 = jnp.bfloat16

D_MODEL = 1024
DEPTH = 4
N_HEADS = 8
HEAD_W = 128
GLA_DK = 64
GLA_RANK = 16
GLA_GATE_NORM = 16.0
HG_DK = 128
HG_K = 512
D_FF_DENSE = 2816
N_EXPERTS = 8
D_FF_EXPERT = 3584
PLE_DIM = 256
EPS = 1e-6
HEAD_EPS = 1e-5
F_MIN = 1e-6

CHUNK = 64
SUB = 16
EXP2_CLAMP = 86.0
LANES = 128
VMEM_LIMIT = 56 * 1024 * 1024

MIX_TILE = 256
ROW_TILE = 512
FF_CHUNK = 256
EXP_FF_CHUNK = 512

COL_HQ = 0
COL_HF = 512
COL_GLR = 1536
COL_GQ = 1792
COL_GK = 2304
COL_V = 2816
COL_GATE = 3840
NP_COLS = 4864
COL_GROUPS = ((0, COL_GQ), (COL_GQ, COL_V), (COL_V, NP_COLS))
LOG2E = 1.4426950408889634


def _dot(a, b):
    return jnp.dot(a, b, preferred_element_type=F32)


def _dot_nt(a, b):
    return lax.dot_general(a, b, (((1,), (1,)), ((), ())), preferred_element_type=F32)


def _dot_tn(a, b):
    return lax.dot_general(a, b, (((0,), (0,)), ((), ())), preferred_element_type=F32)


def _sigmoid(x):
    return 1.0 / (1.0 + jnp.exp(-x))


def _rmsnorm(x, w):
    ms = jnp.mean(x * x, axis=-1, keepdims=True)
    return x * lax.rsqrt(ms + EPS) * w


def _const_spec(shape):
    nd = len(shape)
    return pl.BlockSpec(shape, lambda *_: (0,) * nd, pipeline_mode=pl.Buffered(1))


def _params(n_grid):
    return pltpu.CompilerParams(
        dimension_semantics=("arbitrary",) * n_grid, vmem_limit_bytes=VMEM_LIMIT)


def _proj_kernel(layer, h_ref, nw_ref, w_ref, wg2_ref, bg_ref, lbl_ref,
                 q_ref, kf_ref, kb_ref, v_ref, gate_ref, gfh_ref, gfl_ref, gbh_ref, gbl_ref,
                 u_ref, p1_ref, p2_ref, p3_ref):
    u_ref[...] = _rmsnorm(h_ref[...], nw_ref[...]).astype(BF16)
    for p_ref, (c0, c1) in zip((p1_ref, p2_ref, p3_ref), COL_GROUPS):
        p_ref[...] = _dot(u_ref[...], w_ref[:, c0:c1])

    def split(hi_ref, lo_ref, cols, g):
        hi = g.astype(BF16)
        hi_ref[:, cols] = hi
        lo_ref[:, cols] = (g - hi.astype(F32)).astype(BF16)

    hq = p1_ref[:, COL_HQ:COL_HQ + 512]
    q_ref[:, 512:1024] = (hq * _sigmoid(hq) * (HG_DK ** -0.5)).astype(BF16)
    for direction, (k_ref, hi_ref, lo_ref) in enumerate(
            ((kf_ref, gfh_ref, gfl_ref), (kb_ref, gbh_ref, gbl_ref))):
        c0 = COL_GLR + direction * LANES
        x = _dot(p1_ref[:, c0:c0 + LANES].astype(BF16), wg2_ref[direction].astype(BF16))
        x = x + bg_ref[direction]
        log_sig = jnp.minimum(x, 0.0) - jnp.log(1.0 + jnp.exp(-jnp.abs(x)))
        split(hi_ref, lo_ref, slice(0, 512), log_sig * (LOG2E / GLA_GATE_NORM))
        rows = [lbl_ref[2 * d + direction:2 * d + direction + 1, :] for d in range(DEPTH)]
        mx = functools.reduce(jnp.maximum, rows)
        ex = [jnp.exp(r - mx) for r in rows]
        lb = sum(ex[1:layer + 1], jnp.zeros_like(mx)) / sum(ex)
        c0 = COL_HF + direction * HG_K
        z = p1_ref[:, c0:c0 + HG_K]
        e = jnp.exp(-jnp.abs(z))
        s_big = 1.0 / (1.0 + e)
        s_small = e * s_big
        sig = jnp.where(z >= 0, s_big, s_small)
        nsig = jnp.where(z >= 0, s_small, s_big)
        f = lb + (1.0 - lb) * sig
        split(hi_ref, lo_ref, slice(512, 1024), jnp.log(jnp.maximum(f, F_MIN)) * LOG2E)
        k_ref[:, 512:1024] = ((1.0 - lb) * nsig).astype(BF16)

    q_ref[:, 0:512] = (p2_ref[:, 0:512] * (GLA_DK ** -0.5)).astype(BF16)
    gk = p2_ref[:, 512:1024].astype(BF16)
    kf_ref[:, 0:512] = gk
    kb_ref[:, 0:512] = gk
    v_ref[...] = p3_ref[:, 0:1024].astype(BF16)
    gate_ref[...] = p3_ref[:, 1024:2048].astype(BF16)


def _pad_heads(w, n_heads, width):
    lead = w.shape[:-1]
    w = w.reshape(lead + (n_heads, width))
    w = jnp.pad(w, [(0, 0)] * len(lead) + [(0, 0), (0, HEAD_W - width)])
    return w.reshape(lead + (n_heads * HEAD_W,))


def _proj(layer, h, norm_w, w_in, wg2, bg, lb_logits):
    t = h.shape[0]
    gq, gk, gv, gr, glr_f, glr_b, hq, hf_f, hf_b, hv, hr = jnp.split(
        w_in, [256, 512, 1024, 1536, 1552, 1568, 2080, 2592, 3104, 3616], axis=-1)
    pad_r = lambda w: jnp.pad(w, ((0, 0), (0, LANES - GLA_RANK)))
    w = jnp.concatenate([hq, hf_f, hf_b, pad_r(glr_f), pad_r(glr_b), _pad_heads(gq, 4, GLA_DK),
                         _pad_heads(gk, 4, GLA_DK), gv, hv, gr, hr], -1).astype(BF16)
    wg2_p = jnp.pad(_pad_heads(wg2, 4, GLA_DK), ((0, 0), (0, LANES - GLA_RANK), (0, 0)))
    bg_p = _pad_heads(bg, 4, GLA_DK).reshape(2, 1, 512)
    lbl = lb_logits.reshape(DEPTH * 2, HG_K)
    row = lambda: pl.BlockSpec((ROW_TILE, D_MODEL), lambda i: (i, 0))
    return pl.pallas_call(
        functools.partial(_proj_kernel, layer),
        grid=(t // ROW_TILE,),
        in_specs=[row(), _const_spec((1, D_MODEL)), _const_spec((D_MODEL, NP_COLS)),
                  _const_spec((2, LANES, 512)), _const_spec((2, 1, 512)),
                  _const_spec((DEPTH * 2, HG_K))],
        out_specs=[row() for _ in range(9)],
        out_shape=[jax.ShapeDtypeStruct((t, D_MODEL), BF16) for _ in range(9)],
        scratch_shapes=[pltpu.VMEM((ROW_TILE, D_MODEL), BF16)] + [
            pltpu.VMEM((ROW_TILE, c1 - c0), F32) for c0, c1 in COL_GROUPS],
        compiler_params=_params(1),
        name=f"proj_l{layer}",
    )(h, norm_w.reshape(1, D_MODEL), w, wg2_p, bg_p, lbl)


A_GROUPS = 3
A_WIDTH = A_GROUPS * HEAD_W


def _stage_a(rev, q_ref, k_ref, gh_ref, gl_ref, r0, tri, bufs, slot):
    qa, ka, qd, kd, qi, ks, et, _ = bufs
    rows = slice(r0, r0 + CHUNK)
    cum = _dot(tri, gh_ref[0, rows, :]) + _dot(tri, gl_ref[0, rows, :])
    q = q_ref[0, rows, :].astype(F32)
    k = k_ref[0, rows, :].astype(F32)

    def put(ref, rows, group, val):
        for n in range(N_HEADS):
            c0 = n * A_WIDTH + group * HEAD_W
            ref[slot, rows, c0:c0 + HEAD_W] = val[:, n * HEAD_W:(n + 1) * HEAD_W]

    for group, (half, a) in enumerate(((2 * SUB, 0), (SUB, 0), (SUB, 2 * SUB))):
        lo, hi = slice(a, a + half), slice(a + half, a + 2 * half)
        if not rev:
            ref, k_rows, q_rows = cum[a + half - 1:a + half], lo, hi
        else:
            ref, q_rows, k_rows = cum[a + half:a + half + 1], lo, hi
        put(qa, q_rows, group, (q[q_rows] * jnp.exp2(cum[q_rows] - ref)).astype(BF16))
        put(ka, k_rows, group, (k[k_rows] * jnp.exp2(ref - cum[k_rows])).astype(BF16))

    mids = []
    for a in range(0, CHUNK, SUB):
        m = 0.5 * (cum[a:a + 1] + cum[a + SUB - 1:a + SUB])
        mids.append(jnp.broadcast_to(m, (SUB, D_MODEL)))
    dd = cum - jnp.concatenate(mids, 0)
    qd[slot] = (q * jnp.exp2(jnp.clip(dd, -EXP2_CLAMP, EXP2_CLAMP))).astype(BF16)
    kd[slot] = (k * jnp.exp2(jnp.clip(-dd, -EXP2_CLAMP, EXP2_CLAMP))).astype(BF16)

    tot = cum[0:1] if rev else cum[CHUNK - 1:CHUNK]
    qi[slot] = (q * jnp.exp2(cum)).astype(BF16)
    ks[slot] = (k * jnp.exp2(tot - cum)).astype(BF16)
    et[slot] = jnp.exp2(tot)


def _stage_b(bufs, slot, v_ref, st_ref, o_ref, r0, diag):
    qa, ka, qd, kd, qi, ks, et, ab = bufs
    for n in range(N_HEADS):
        wide = slice(n * A_WIDTH, (n + 1) * A_WIDTH)
        head = slice(n * HEAD_W, (n + 1) * HEAD_W)
        scores = _dot_nt(qa[slot, :, wide], ka[slot, :, wide])
        scores = scores + jnp.where(diag, _dot_nt(qd[slot, :, head], kd[slot, :, head]), 0.0)
        ab[slot, n] = scores.astype(BF16)
        st = st_ref[n]
        o_ref[0, r0:r0 + CHUNK, head] = _dot_nt(qi[slot, :, head], st.astype(BF16))
        st_ref[n] = st * et[slot, :, head] + _dot_tn(v_ref[0, r0:r0 + CHUNK, head],
                                                    ks[slot, :, head])


def _stage_c(bufs, slot, v_ref, o_ref, r0):
    ab = bufs[7]
    for n in range(N_HEADS):
        head = slice(n * HEAD_W, (n + 1) * HEAD_W)
        o_ref[0, r0:r0 + CHUNK, head] += _dot(ab[slot, n], v_ref[0, r0:r0 + CHUNK, head])


def _scan_kernel(qf_ref, qb_ref, kf_ref, kb_ref, vf_ref, vb_ref, gfh_ref, gfl_ref, gbh_ref,
                 gbl_ref, of_ref, ob_ref, sf_ref, sb_ref, *buf_refs):
    n_chunks = MIX_TILE // CHUNK
    bufs_f, bufs_b = buf_refs[:len(buf_refs) // 2], buf_refs[len(buf_refs) // 2:]

    @pl.when(pl.program_id(1) == 0)
    def _():
        sf_ref[...] = jnp.zeros_like(sf_ref)
        sb_ref[...] = jnp.zeros_like(sb_ref)
        for bufs in (bufs_f, bufs_b):
            bufs[0][...] = jnp.zeros_like(bufs[0])
            bufs[1][...] = jnp.zeros_like(bufs[1])

    ri = lax.broadcasted_iota(jnp.int32, (CHUNK, CHUNK), 0)
    ci = lax.broadcasted_iota(jnp.int32, (CHUNK, CHUNK), 1)
    tri_f = (ci <= ri).astype(BF16)
    tri_b = (ci >= ri).astype(BF16)
    same16 = (ri // SUB) == (ci // SUB)
    diag_f = same16 & (ci <= ri)
    diag_b = same16 & (ci >= ri)
    row_f = lambda c: c * CHUNK
    row_b = lambda c: (n_chunks - 1 - c) * CHUNK

    def stage_a(c):
        _stage_a(False, qf_ref, kf_ref, gfh_ref, gfl_ref, row_f(c), tri_f, bufs_f, c % 2)
        _stage_a(True, qb_ref, kb_ref, gbh_ref, gbl_ref, row_b(c), tri_b, bufs_b, c % 2)

    stage_a(0)
    for c in range(n_chunks):
        if c + 1 < n_chunks:
            stage_a(c + 1)
        _stage_b(bufs_f, c % 2, vf_ref, sf_ref, of_ref, row_f(c), diag_f)
        _stage_b(bufs_b, c % 2, vb_ref, sb_ref, ob_ref, row_b(c), diag_b)
        _stage_c(bufs_f, c % 2, vf_ref, of_ref, row_f(c))
        _stage_c(bufs_b, c % 2, vb_ref, ob_ref, row_b(c))


def _scan_bufs():
    wide = pltpu.VMEM((2, CHUNK, N_HEADS * A_WIDTH), BF16)
    narrow = pltpu.VMEM((2, CHUNK, D_MODEL), BF16)
    return [wide, wide, narrow, narrow, narrow, narrow,
            pltpu.VMEM((2, 1, D_MODEL), F32),
            pltpu.VMEM((2, N_HEADS, CHUNK, CHUNK), BF16)]


def _scan(layer, bsz, q, kf, kb, v, gfh, gfl, gbh, gbl):
    seq = q.shape[0] // bsz
    nt = seq // MIX_TILE
    to3 = lambda a: a.reshape(bsz, seq, D_MODEL)
    tile = lambda idx: pl.BlockSpec((1, MIX_TILE, D_MODEL), idx)
    fwd = lambda b, j: (b, j, 0)
    bwd = lambda b, j: (b, nt - 1 - j, 0)
    state = pltpu.VMEM((N_HEADS, HEAD_W, HEAD_W), F32)
    return pl.pallas_call(
        _scan_kernel,
        grid=(bsz, nt),
        in_specs=[tile(fwd), tile(bwd), tile(fwd), tile(bwd), tile(fwd), tile(bwd),
                  tile(fwd), tile(fwd), tile(bwd), tile(bwd)],
        out_specs=[tile(fwd), tile(bwd)],
        out_shape=[jax.ShapeDtypeStruct((bsz, seq, D_MODEL), F32),
                   jax.ShapeDtypeStruct((bsz, seq, D_MODEL), F32)],
        scratch_shapes=[state, state] + _scan_bufs() + _scan_bufs(),
        compiler_params=_params(2),
        name=f"scan_l{layer}",
    )(to3(q), to3(q), to3(kf), to3(kb), to3(v), to3(v), to3(gfh), to3(gfl), to3(gbh), to3(gbl))


def _post_kernel(of_ref, ob_ref, gate_ref, h_ref, ones_ref, onw_ref, wout_ref, out_ref):
    o = of_ref[...] + ob_ref[...]
    ms = _dot((o * o).astype(BF16), ones_ref[...]) * (1.0 / HEAD_W)
    y = o * lax.rsqrt(ms + HEAD_EPS) * onw_ref[...]
    g = gate_ref[...].astype(F32)
    y = y * (g * _sigmoid(g))
    out_ref[...] = h_ref[...] + _dot(y.astype(BF16), wout_ref[...])


def _post(layer, o_f, o_b, gates, h, gla_onorm, hg_onorm, w_out):
    t = h.shape[0]
    head_id = jnp.arange(D_MODEL) // HEAD_W
    ones = (head_id[:, None] == head_id[None, :]).astype(BF16)
    onw = jnp.concatenate([jnp.tile(gla_onorm, 4), jnp.tile(hg_onorm, 4)]).reshape(1, D_MODEL)
    row = lambda: pl.BlockSpec((ROW_TILE, D_MODEL), lambda i: (i, 0))
    return pl.pallas_call(
        _post_kernel,
        grid=(t // ROW_TILE,),
        in_specs=[row(), row(), row(), row(),
                  _const_spec((D_MODEL, D_MODEL)), _const_spec((1, D_MODEL)),
                  _const_spec((D_MODEL, D_MODEL))],
        out_specs=row(),
        out_shape=jax.ShapeDtypeStruct((t, D_MODEL), F32),
        compiler_params=_params(1),
        name=f"post_l{layer}",
    )(o_f, o_b, gates, h, ones, onw, w_out.astype(BF16))


def _ple_tail(h2, p, nple, wpp_ref, wpg_ref, nfinal, final):
    gate = _sigmoid(_dot(_rmsnorm(h2, nple).astype(BF16), wpg_ref[...]))
    h3 = h2 + _dot(p.astype(BF16), wpp_ref[...]) * gate
    if final:
        h3 = _rmsnorm(h3, nfinal)
    return h3


def _dense_kernel(final, h_ref, p_ref, nffn_ref, wg_ref, wu_ref, wd_ref, nple_ref,
                  wpp_ref, wpg_ref, nfin_ref, out_ref):
    h1 = h_ref[...]
    v = _rmsnorm(h1, nffn_ref[...]).astype(BF16)
    acc = jnp.zeros_like(h1)
    for c in range(0, D_FF_DENSE, FF_CHUNK):
        a = _dot(v, wg_ref[:, c:c + FF_CHUNK])
        b = _dot(v, wu_ref[:, c:c + FF_CHUNK])
        acc = acc + _dot((a * _sigmoid(a) * b).astype(BF16), wd_ref[c:c + FF_CHUNK, :])
    out_ref[...] = _ple_tail(h1 + acc, p_ref[...], nple_ref[...], wpp_ref, wpg_ref,
                             nfin_ref[...], final)


def _dense_ffn(layer, final, h, p, norm_ffn, wg, wu, wd, norm_ple, w_pp, w_pg, norm_final):
    t = h.shape[0]
    vec = lambda w: w.reshape(1, D_MODEL)
    return pl.pallas_call(
        functools.partial(_dense_kernel, final),
        grid=(t // ROW_TILE,),
        in_specs=[pl.BlockSpec((ROW_TILE, D_MODEL), lambda i: (i, 0)),
                  pl.BlockSpec((ROW_TILE, PLE_DIM), lambda i: (i, 0)),
                  _const_spec((1, D_MODEL)),
                  _const_spec((D_MODEL, D_FF_DENSE)), _const_spec((D_MODEL, D_FF_DENSE)),
                  _const_spec((D_FF_DENSE, D_MODEL)),
                  _const_spec((1, D_MODEL)),
                  _const_spec((PLE_DIM, D_MODEL)), _const_spec((D_MODEL, D_MODEL)),
                  _const_spec((1, D_MODEL))],
        out_specs=pl.BlockSpec((ROW_TILE, D_MODEL), lambda i: (i, 0)),
        out_shape=jax.ShapeDtypeStruct((t, D_MODEL), F32),
        compiler_params=_params(1),
        name=f"dense_l{layer}",
    )(h, p, vec(norm_ffn), wg.astype(BF16), wu.astype(BF16), wd.astype(BF16),
      vec(norm_ple), w_pp.astype(BF16), w_pg.astype(BF16), vec(norm_final))


META_G1, META_G2, META_E1, META_E2, META_R1, META_R2 = range(6)
HALF = D_MODEL // 2
HI_MASK = 0xFFFF0000


def _pack_rows(x):
    bits = pltpu.bitcast(x.astype(BF16).astype(F32), jnp.uint32)
    return (bits[:, :HALF] >> 16) | (bits[:, HALF:] & jnp.uint32(HI_MASK))


def _unpack_rows(w):
    lo = pltpu.bitcast(w << 16, F32)
    hi = pltpu.bitcast(w & jnp.uint32(HI_MASK), F32)
    return jnp.concatenate([lo, hi], axis=1)


def _router_kernel(h_ref, nffn_ref, wr_ref, tri_ref, xp_ref, meta_ref, cnt_ref, base_ref):
    @pl.when(pl.program_id(0) == 0)
    def _():
        base_ref[...] = jnp.zeros_like(base_ref)

    vf = _rmsnorm(h_ref[...], nffn_ref[...])
    xp_ref[...] = _pack_rows(vf)
    v_hi = vf.astype(BF16)
    v_lo = (vf - v_hi.astype(F32)).astype(BF16)
    w = wr_ref[...]
    w_hi = w.astype(BF16)
    w_lo = (w - w_hi.astype(F32)).astype(BF16)
    logits = _dot(v_hi, w_hi) + _dot(v_lo, w_hi) + _dot(v_hi, w_lo)
    lane = lax.broadcasted_iota(jnp.int32, logits.shape, 1)
    neg = jnp.float32(-jnp.inf)
    lg = jnp.where(lane < N_EXPERTS, logits, neg)
    m1 = jnp.max(lg, axis=-1, keepdims=True)
    i1 = jnp.min(jnp.where(lg == m1, lane, LANES), axis=-1, keepdims=True)
    first = lane == i1
    lg2 = jnp.where(first, neg, lg)
    m2 = jnp.max(lg2, axis=-1, keepdims=True)
    i2 = jnp.min(jnp.where(lg2 == m2, lane, LANES), axis=-1, keepdims=True)
    second = lane == i2
    e = jnp.exp(m2 - m1)
    g1 = 1.0 / (1.0 + e)
    g2 = e * g1
    cnt = jnp.where(first | second, 1.0, 0.0)
    rank = base_ref[...] + _dot(tri_ref[...], cnt.astype(BF16))
    r1 = jnp.sum(jnp.where(first, rank, 0.0), axis=-1, keepdims=True)
    r2 = jnp.sum(jnp.where(second, rank, 0.0), axis=-1, keepdims=True)
    base_ref[...] = base_ref[...] + jnp.sum(cnt, axis=0, keepdims=True)
    cnt_ref[...] = base_ref[...]
    meta = jnp.zeros(logits.shape, F32)
    for col, val in ((META_G1, g1), (META_G2, g2), (META_E1, i1.astype(F32)),
                     (META_E2, i2.astype(F32)), (META_R1, r1), (META_R2, r2)):
        meta = jnp.where(lane == col, val, meta)
    meta_ref[...] = meta


def _router(layer, h, norm_ffn, w_router):
    t = h.shape[0]
    wr = jnp.pad(w_router, ((0, 0), (0, LANES - N_EXPERTS)))
    ids = jnp.arange(ROW_TILE)
    tri = (ids[None, :] < ids[:, None]).astype(BF16)
    return pl.pallas_call(
        _router_kernel,
        grid=(t // ROW_TILE,),
        in_specs=[pl.BlockSpec((ROW_TILE, D_MODEL), lambda i: (i, 0)),
                  _const_spec((1, D_MODEL)), _const_spec((D_MODEL, LANES)),
                  _const_spec((ROW_TILE, ROW_TILE))],
        out_specs=[pl.BlockSpec((ROW_TILE, HALF), lambda i: (i, 0)),
                   pl.BlockSpec((ROW_TILE, LANES), lambda i: (i, 0)),
                   pl.BlockSpec((1, LANES), lambda i: (0, 0))],
        out_shape=[jax.ShapeDtypeStruct((t, HALF), jnp.uint32),
                   jax.ShapeDtypeStruct((t, LANES), F32),
                   jax.ShapeDtypeStruct((1, LANES), F32)],
        scratch_shapes=[pltpu.VMEM((1, LANES), F32)],
        compiler_params=_params(1),
        name=f"router_l{layer}",
    )(h, norm_ffn.reshape(1, D_MODEL), wr, tri)


SC_CORES, SC_SUBCORES = 2, 16
SC_WORKERS = SC_CORES * SC_SUBCORES
SC_WINDOW = 64


def _sc_gather(table, idx):
    n_out, d = idx.shape[0], table.shape[1]
    per_worker = n_out // SC_WORKERS
    n_win = per_worker // SC_WINDOW
    assert n_win * SC_WINDOW * SC_WORKERS == n_out and n_win % 2 == 0
    mesh = plsc.VectorSubcoreMesh(core_axis_name="c", subcore_axis_name="s")

    def body(table_hbm, idx_hbm, out_hbm, idx_a, idx_b, rows_a, rows_b, sem_a, sem_b):
        wid = lax.axis_index("s") * SC_CORES + lax.axis_index("c")

        def rows_of(w):
            return pl.ds(pl.multiple_of(wid * per_worker + w * SC_WINDOW, SC_WINDOW), SC_WINDOW)

        def start(w, idx_v, rows_v, sem):
            pltpu.sync_copy(idx_hbm.at[rows_of(w)], idx_v)
            pltpu.async_copy(table_hbm.at[idx_v], rows_v, sem)

        def finish(w, idx_v, rows_v, sem):
            pltpu.make_async_copy(table_hbm.at[idx_v], rows_v, sem).wait()
            pltpu.sync_copy(rows_v, out_hbm.at[rows_of(w)])

        start(0, idx_a, rows_a, sem_a)

        @pl.loop(0, n_win, step=2)
        def _(w):
            start(w + 1, idx_b, rows_b, sem_b)
            finish(w, idx_a, rows_a, sem_a)

            @pl.when(w + 2 < n_win)
            def _():
                start(w + 2, idx_a, rows_a, sem_a)

            finish(w + 1, idx_b, rows_b, sem_b)

    return pl.kernel(
        body, mesh=mesh,
        out_type=jax.ShapeDtypeStruct((n_out, d), table.dtype),
        scratch_types=[pltpu.VMEM((SC_WINDOW,), jnp.int32), pltpu.VMEM((SC_WINDOW,), jnp.int32),
                       pltpu.VMEM((SC_WINDOW, d), table.dtype),
                       pltpu.VMEM((SC_WINDOW, d), table.dtype),
                       pltpu.SemaphoreType.DMA, pltpu.SemaphoreType.DMA],
    )(table, idx)


EXP_ROWS = 1024


def _experts_kernel(be_ref, na_ref, xs_ref, wg_ref, wu_ref, wd_ref, ys_ref, x_ref, acc_ref):
    b = pl.program_id(0)
    f = pl.program_id(1)

    @pl.when(b < na_ref[0])
    def _():
        @pl.when(f == 0)
        def _():
            x_ref[...] = _unpack_rows(xs_ref[...]).astype(BF16)
            acc_ref[...] = jnp.zeros_like(acc_ref)

        x = x_ref[...]
        a = _dot(x, wg_ref[0, 0].astype(BF16))
        u = _dot(x, wu_ref[0, 0].astype(BF16))
        acc_ref[...] += _dot((a * _sigmoid(a) * u).astype(BF16), wd_ref[0, 0].astype(BF16))

        @pl.when(f == pl.num_programs(1) - 1)
        def _():
            ys_ref[...] = _pack_rows(acc_ref[...])


def _experts(layer, xs, blk_expert, n_active, wg, wu, wd):
    moe_idx = layer // 2
    rows = xs.shape[0]
    nf = D_FF_EXPERT // EXP_FF_CHUNK
    fsel = lambda b, f, na: jnp.where(b < na[0], f, nf - 1)
    grid_spec = pltpu.PrefetchScalarGridSpec(
        num_scalar_prefetch=2,
        grid=(rows // EXP_ROWS, nf),
        in_specs=[pl.BlockSpec((EXP_ROWS, HALF), lambda b, f, be, na: (b, 0)),
                  pl.BlockSpec((1, 1, D_MODEL, EXP_FF_CHUNK),
                               lambda b, f, be, na: (moe_idx, be[b], 0, fsel(b, f, na))),
                  pl.BlockSpec((1, 1, D_MODEL, EXP_FF_CHUNK),
                               lambda b, f, be, na: (moe_idx, be[b], 0, fsel(b, f, na))),
                  pl.BlockSpec((1, 1, EXP_FF_CHUNK, D_MODEL),
                               lambda b, f, be, na: (moe_idx, be[b], fsel(b, f, na), 0))],
        out_specs=pl.BlockSpec((EXP_ROWS, HALF), lambda b, f, be, na: (b, 0)),
        scratch_shapes=[pltpu.VMEM((EXP_ROWS, D_MODEL), BF16),
                        pltpu.VMEM((EXP_ROWS, D_MODEL), F32)])
    return pl.pallas_call(
        _experts_kernel,
        grid_spec=grid_spec,
        out_shape=jax.ShapeDtypeStruct((rows, HALF), jnp.uint32),
        compiler_params=_params(2),
        name=f"experts_l{layer}",
    )(blk_expert, n_active, xs, wg, wu, wd)


def _combine_kernel(final, h_ref, y1_ref, y2_ref, meta_ref, p_ref, nple_ref, wpp_ref, wpg_ref,
                    nfin_ref, out_ref):
    meta = meta_ref[...]
    g1 = meta[:, META_G1:META_G1 + 1]
    g2 = meta[:, META_G2:META_G2 + 1]
    h2 = h_ref[...] + g1 * _unpack_rows(y1_ref[...]) + g2 * _unpack_rows(y2_ref[...])
    out_ref[...] = _ple_tail(h2, p_ref[...], nple_ref[...], wpp_ref, wpg_ref, nfin_ref[...], final)


def _combine(layer, final, h, yg, meta, p, norm_ple, w_pp, w_pg, norm_final):
    t = h.shape[0]
    nt = t // ROW_TILE
    vec = lambda w: w.reshape(1, D_MODEL)
    return pl.pallas_call(
        functools.partial(_combine_kernel, final),
        grid=(nt,),
        in_specs=[pl.BlockSpec((ROW_TILE, D_MODEL), lambda i: (i, 0)),
                  pl.BlockSpec((ROW_TILE, HALF), lambda i: (i, 0)),
                  pl.BlockSpec((ROW_TILE, HALF), lambda i: (i + nt, 0)),
                  pl.BlockSpec((ROW_TILE, LANES), lambda i: (i, 0)),
                  pl.BlockSpec((ROW_TILE, PLE_DIM), lambda i: (i, 0)),
                  _const_spec((1, D_MODEL)),
                  _const_spec((PLE_DIM, D_MODEL)), _const_spec((D_MODEL, D_MODEL)),
                  _const_spec((1, D_MODEL))],
        out_specs=pl.BlockSpec((ROW_TILE, D_MODEL), lambda i: (i, 0)),
        out_shape=jax.ShapeDtypeStruct((t, D_MODEL), F32),
        compiler_params=_params(1),
        name=f"combine_l{layer}",
    )(h, yg, yg, meta, p, vec(norm_ple), w_pp.astype(BF16), w_pg.astype(BF16), vec(norm_final))


def _moe(layer, final, h1, p, norm_ffn, w_router, wg, wu, wd, norm_ple, w_pp, w_pg, norm_final):
    t = h1.shape[0]
    xp, meta, counts = _router(layer, h1, norm_ffn, w_router)
    cnt = counts[0, :N_EXPERTS].astype(jnp.int32)
    padded = ((cnt + EXP_ROWS - 1) // EXP_ROWS) * EXP_ROWS
    ends = jnp.cumsum(padded)
    offs = ends - padded
    e1 = meta[:, META_E1].astype(jnp.int32)
    e2 = meta[:, META_E2].astype(jnp.int32)
    pos1 = offs[e1] + meta[:, META_R1].astype(jnp.int32)
    pos2 = offs[e2] + meta[:, META_R2].astype(jnp.int32)
    rows = 2 * t + N_EXPERTS * EXP_ROWS
    tok = jnp.arange(t, dtype=jnp.int32)
    tok_sorted = jnp.zeros((rows,), jnp.int32).at[jnp.concatenate([pos1, pos2])].set(
        jnp.concatenate([tok, tok]), unique_indices=True)
    blk_start = jnp.arange(rows // EXP_ROWS, dtype=jnp.int32) * EXP_ROWS
    blk_expert = jnp.minimum(
        jnp.sum(blk_start[:, None] >= ends[None, :], axis=1), N_EXPERTS - 1).astype(jnp.int32)
    n_active = (ends[-1:] // EXP_ROWS).astype(jnp.int32)

    xs = _sc_gather(xp, tok_sorted)
    ys = _experts(layer, xs, blk_expert, n_active, wg, wu, wd)
    yg = _sc_gather(ys, jnp.concatenate([pos1, pos2]))
    return _combine(layer, final, h1, yg, meta, p, norm_ple, w_pp, w_pg, norm_final)


def kernel(x, p, w_in, gla_wg2, gla_bg, hg_lb_logits, gla_onorm, hg_onorm, w_out, norm_mix,
           norm_ffn, w_dense_gate, w_dense_up, w_dense_down, w_router, w_exp_gate, w_exp_up,
           w_exp_down, w_ple_proj, w_ple_gate, norm_ple, norm_final):
    bsz, seq, _ = x.shape
    t = bsz * seq
    h = x.reshape(t, D_MODEL)
    for i in range(DEPTH):
        final = i == DEPTH - 1
        q, kf, kb, v, gates, gfh, gfl, gbh, gbl = _proj(
            i, h, norm_mix[i], w_in[i], gla_wg2[i], gla_bg[i], hg_lb_logits)
        o_f, o_b = _scan(i, bsz, q, kf, kb, v, gfh, gfl, gbh, gbl)
        h1 = _post(i, o_f.reshape(t, D_MODEL), o_b.reshape(t, D_MODEL), gates, h,
                   gla_onorm[i], hg_onorm[i], w_out[i])
        p_i = p[i].reshape(t, PLE_DIM)
        j = i // 2
        if i % 2 == 0:
            h = _dense_ffn(i, final, h1, p_i, norm_ffn[i], w_dense_gate[j], w_dense_up[j],
                           w_dense_down[j], norm_ple[i], w_ple_proj[i], w_ple_gate[i],
                           norm_final)
        else:
            h = _moe(i, final, h1, p_i, norm_ffn[i], w_router[j], w_exp_gate, w_exp_up,
                     w_exp_down, norm_ple[i], w_ple_proj[i], w_ple_gate[i], norm_final)
    return h.reshape(bsz, seq, D_MODEL)
```

```python
import functools

import jax
import jax.numpy as jnp
from jax import lax
from jax.experimental import pallas as pl
from jax.experimental.pallas import tpu as pltpu
from jax.experimental.pallas import tpu_sc as plsc

F32 = jnp.float32
BF16 = jnp.bfloat16

D_MODEL = 1024
DEPTH = 4
N_HEADS = 8
HEAD_W = 128
GLA_DK = 64
GLA_RANK = 16
GLA_GATE_NORM = 16.0
HG_DK = 128
HG_K = 512
D_FF_DENSE = 2816
N_EXPERTS = 8
D_FF_EXPERT = 3584
PLE_DIM = 256
EPS = 1e-6
HEAD_EPS = 1e-5
F_MIN = 1e-6

CHUNK = 64
SUB = 16
EXP2_CLAMP = 86.0
LANES = 128
VMEM_LIMIT = 56 * 1024 * 1024

MIX_TILE = 256
ROW_TILE = 512
FF_CHUNK = 256
EXP_FF_CHUNK = 512

COL_HQ = 0
COL_HF = 512
COL_GLR = 1536
COL_GQ = 1792
COL_GK = 2304
COL_V = 2816
COL_GATE = 3840
NP_COLS = 4864
COL_GROUPS = ((0, COL_GQ), (COL_GQ, COL_V), (COL_V, NP_COLS))
LOG2E = 1.4426950408889634


def _dot(a, b):
    return jnp.dot(a, b, preferred_element_type=F32)


def _dot_nt(a, b):
    return lax.dot_general(a, b, (((1,), (1,)), ((), ())), preferred_element_type=F32)


def _dot_tn(a, b):
    return lax.dot_general(a, b, (((0,), (0,)), ((), ())), preferred_element_type=F32)


def _sigmoid(x):
    return 1.0 / (1.0 + jnp.exp(-x))


def _rmsnorm(x, w):
    ms = jnp.mean(x * x, axis=-1, keepdims=True)
    return x * lax.rsqrt(ms + EPS) * w


def _const_spec(shape):
    nd = len(shape)
    return pl.BlockSpec(shape, lambda *_: (0,) * nd, pipeline_mode=pl.Buffered(1))


def _params(n_grid):
    return pltpu.CompilerParams(
        dimension_semantics=("arbitrary",) * n_grid, vmem_limit_bytes=VMEM_LIMIT)


def _proj_kernel(layer, h_ref, nw_ref, w_ref, wg2_ref, bg_ref, lbl_ref,
                 q_ref, kf_ref, kb_ref, v_ref, gate_ref, gfh_ref, gfl_ref, gbh_ref, gbl_ref,
                 u_ref, p1_ref, p2_ref, p3_ref):
    u_ref[...] = _rmsnorm(h_ref[...], nw_ref[...]).astype(BF16)
    for p_ref, (c0, c1) in zip((p1_ref, p2_ref, p3_ref), COL_GROUPS):
        p_ref[...] = _dot(u_ref[...], w_ref[:, c0:c1])

    def split(hi_ref, lo_ref, cols, g):
        hi = g.astype(BF16)
        hi_ref[:, cols] = hi
        lo_ref[:, cols] = (g - hi.astype(F32)).astype(BF16)

    hq = p1_ref[:, COL_HQ:COL_HQ + 512]
    q_ref[:, 512:1024] = (hq * _sigmoid(hq) * (HG_DK ** -0.5)).astype(BF16)
    for direction, (k_ref, hi_ref, lo_ref) in enumerate(
            ((kf_ref, gfh_ref, gfl_ref), (kb_ref, gbh_ref, gbl_ref))):
        c0 = COL_GLR + direction * LANES
        x = _dot(p1_ref[:, c0:c0 + LANES].astype(BF16), wg2_ref[direction].astype(BF16))
        x = x + bg_ref[direction]
        log_sig = jnp.minimum(x, 0.0) - jnp.log(1.0 + jnp.exp(-jnp.abs(x)))
        split(hi_ref, lo_ref, slice(0, 512), log_sig * (LOG2E / GLA_GATE_NORM))
        rows = [lbl_ref[2 * d + direction:2 * d + direction + 1, :] for d in range(DEPTH)]
        mx = functools.reduce(jnp.maximum, rows)
        ex = [jnp.exp(r - mx) for r in rows]
        lb = sum(ex[1:layer + 1], jnp.zeros_like(mx)) / sum(ex)
        c0 = COL_HF + direction * HG_K
        z = p1_ref[:, c0:c0 + HG_K]
        e = jnp.exp(-jnp.abs(z))
        s_big = 1.0 / (1.0 + e)
        s_small = e * s_big
        sig = jnp.where(z >= 0, s_big, s_small)
        nsig = jnp.where(z >= 0, s_small, s_big)
        f = lb + (1.0 - lb) * sig
        split(hi_ref, lo_ref, slice(512, 1024), jnp.log(jnp.maximum(f, F_MIN)) * LOG2E)
        k_ref[:, 512:1024] = ((1.0 - lb) * nsig).astype(BF16)

    q_ref[:, 0:512] = (p2_ref[:, 0:512] * (GLA_DK ** -0.5)).astype(BF16)
    gk = p2_ref[:, 512:1024].astype(BF16)
    kf_ref[:, 0:512] = gk
    kb_ref[:, 0:512] = gk
    v_ref[...] = p3_ref[:, 0:1024].astype(BF16)
    gate_ref[...] = p3_ref[:, 1024:2048].astype(BF16)


def _pad_heads(w, n_heads, width):
    lead = w.shape[:-1]
    w = w.reshape(lead + (n_heads, width))
    w = jnp.pad(w, [(0, 0)] * len(lead) + [(0, 0), (0, HEAD_W - width)])
    return w.reshape(lead + (n_heads * HEAD_W,))


def _proj(layer, h, norm_w, w_in, wg2, bg, lb_logits):
    t = h.shape[0]
    gq, gk, gv, gr, glr_f, glr_b, hq, hf_f, hf_b, hv, hr = jnp.split(
        w_in, [256, 512, 1024, 1536, 1552, 1568, 2080, 2592, 3104, 3616], axis=-1)
    pad_r = lambda w: jnp.pad(w, ((0, 0), (0, LANES - GLA_RANK)))
    w = jnp.concatenate([hq, hf_f, hf_b, pad_r(glr_f), pad_r(glr_b), _pad_heads(gq, 4, GLA_DK),
                         _pad_heads(gk, 4, GLA_DK), gv, hv, gr, hr], -1).astype(BF16)
    wg2_p = jnp.pad(_pad_heads(wg2, 4, GLA_DK), ((0, 0), (0, LANES - GLA_RANK), (0, 0)))
    bg_p = _pad_heads(bg, 4, GLA_DK).reshape(2, 1, 512)
    lbl = lb_logits.reshape(DEPTH * 2, HG_K)
    row = lambda: pl.BlockSpec((ROW_TILE, D_MODEL), lambda i: (i, 0))
    return pl.pallas_call(
        functools.partial(_proj_kernel, layer),
        grid=(t // ROW_TILE,),
        in_specs=[row(), _const_spec((1, D_MODEL)), _const_spec((D_MODEL, NP_COLS)),
                  _const_spec((2, LANES, 512)), _const_spec((2, 1, 512)),
                  _const_spec((DEPTH * 2, HG_K))],
        out_specs=[row() for _ in range(9)],
        out_shape=[jax.ShapeDtypeStruct((t, D_MODEL), BF16) for _ in range(9)],
        scratch_shapes=[pltpu.VMEM((ROW_TILE, D_MODEL), BF16)] + [
            pltpu.VMEM((ROW_TILE, c1 - c0), F32) for c0, c1 in COL_GROUPS],
        compiler_params=_params(1),
        name=f"proj_l{layer}",
    )(h, norm_w.reshape(1, D_MODEL), w, wg2_p, bg_p, lbl)


A_GROUPS = 3
A_WIDTH = A_GROUPS * HEAD_W


def _stage_a(rev, q_ref, k_ref, gh_ref, gl_ref, r0, tri, bufs, slot):
    qa, ka, qd, kd, qi, ks, et, _ = bufs
    rows = slice(r0, r0 + CHUNK)
    cum = _dot(tri, gh_ref[0, rows, :]) + _dot(tri, gl_ref[0, rows, :])
    q = q_ref[0, rows, :].astype(F32)
    k = k_ref[0, rows, :].astype(F32)

    def put(ref, rows, group, val):
        for n in range(N_HEADS):
            c0 = n * A_WIDTH + group * HEAD_W
            ref[slot, rows, c0:c0 + HEAD_W] = val[:, n * HEAD_W:(n + 1) * HEAD_W]

    for group, (half, a) in enumerate(((2 * SUB, 0), (SUB, 0), (SUB, 2 * SUB))):
        lo, hi = slice(a, a + half), slice(a + half, a + 2 * half)
        if not rev:
            ref, k_rows, q_rows = cum[a + half - 1:a + half], lo, hi
        else:
            ref, q_rows, k_rows = cum[a + half:a + half + 1], lo, hi
        put(qa, q_rows, group, (q[q_rows] * jnp.exp2(cum[q_rows] - ref)).astype(BF16))
        put(ka, k_rows, group, (k[k_rows] * jnp.exp2(ref - cum[k_rows])).astype(BF16))

    mids = []
    for a in range(0, CHUNK, SUB):
        m = 0.5 * (cum[a:a + 1] + cum[a + SUB - 1:a + SUB])
        mids.append(jnp.broadcast_to(m, (SUB, D_MODEL)))
    dd = cum - jnp.concatenate(mids, 0)
    qd[slot] = (q * jnp.exp2(jnp.clip(dd, -EXP2_CLAMP, EXP2_CLAMP))).astype(BF16)
    kd[slot] = (k * jnp.exp2(jnp.clip(-dd, -EXP2_CLAMP, EXP2_CLAMP))).astype(BF16)

    tot = cum[0:1] if rev else cum[CHUNK - 1:CHUNK]
    qi[slot] = (q * jnp.exp2(cum)).astype(BF16)
    ks[slot] = (k * jnp.exp2(tot - cum)).astype(BF16)
    et[slot] = jnp.exp2(tot)


def _stage_b(bufs, slot, v_ref, st_ref, o_ref, r0, diag):
    qa, ka, qd, kd, qi, ks, et, ab = bufs
    for n in range(N_HEADS):
        wide = slice(n * A_WIDTH, (n + 1) * A_WIDTH)
        head = slice(n * HEAD_W, (n + 1) * HEAD_W)
        scores = _dot_nt(qa[slot, :, wide], ka[slot, :, wide])
        scores = scores + jnp.where(diag, _dot_nt(qd[slot, :, head], kd[slot, :, head]), 0.0)
        ab[slot, n] = scores.astype(BF16)
        st = st_ref[n]
        o_ref[0, r0:r0 + CHUNK, head] = _dot_nt(qi[slot, :, head], st.astype(BF16))
        st_ref[n] = st * et[slot, :, head] + _dot_tn(v_ref[0, r0:r0 + CHUNK, head],
                                                    ks[slot, :, head])


def _stage_c(bufs, slot, v_ref, o_ref, r0):
    ab = bufs[7]
    for n in range(N_HEADS):
        head = slice(n * HEAD_W, (n + 1) * HEAD_W)
        o_ref[0, r0:r0 + CHUNK, head] += _dot(ab[slot, n], v_ref[0, r0:r0 + CHUNK, head])


def _scan_kernel(qf_ref, qb_ref, kf_ref, kb_ref, vf_ref, vb_ref, gfh_ref, gfl_ref, gbh_ref,
                 gbl_ref, of_ref, ob_ref, sf_ref, sb_ref, *buf_refs):
    n_chunks = MIX_TILE // CHUNK
    bufs_f, bufs_b = buf_refs[:len(buf_refs) // 2], buf_refs[len(buf_refs) // 2:]

    @pl.when(pl.program_id(1) == 0)
    def _():
        sf_ref[...] = jnp.zeros_like(sf_ref)
        sb_ref[...] = jnp.zeros_like(sb_ref)
        for bufs in (bufs_f, bufs_b):
            bufs[0][...] = jnp.zeros_like(bufs[0])
            bufs[1][...] = jnp.zeros_like(bufs[1])

    ri = lax.broadcasted_iota(jnp.int32, (CHUNK, CHUNK), 0)
    ci = lax.broadcasted_iota(jnp.int32, (CHUNK, CHUNK), 1)
    tri_f = (ci <= ri).astype(BF16)
    tri_b = (ci >= ri).astype(BF16)
    same16 = (ri // SUB) == (ci // SUB)
    diag_f = same16 & (ci <= ri)
    diag_b = same16 & (ci >= ri)
    row_f = lambda c: c * CHUNK
    row_b = lambda c: (n_chunks - 1 - c) * CHUNK

    def stage_a(c):
        _stage_a(False, qf_ref, kf_ref, gfh_ref, gfl_ref, row_f(c), tri_f, bufs_f, c % 2)
        _stage_a(True, qb_ref, kb_ref, gbh_ref, gbl_ref, row_b(c), tri_b, bufs_b, c % 2)

    stage_a(0)
    for c in range(n_chunks):
        if c + 1 < n_chunks:
            stage_a(c + 1)
        _stage_b(bufs_f, c % 2, vf_ref, sf_ref, of_ref, row_f(c), diag_f)
        _stage_b(bufs_b, c % 2, vb_ref, sb_ref, ob_ref, row_b(c), diag_b)
        _stage_c(bufs_f, c % 2, vf_ref, of_ref, row_f(c))
        _stage_c(bufs_b, c % 2, vb_ref, ob_ref, row_b(c))


def _scan_bufs():
    wide = pltpu.VMEM((2, CHUNK, N_HEADS * A_WIDTH), BF16)
    narrow = pltpu.VMEM((2, CHUNK, D_MODEL), BF16)
    return [wide, wide, narrow, narrow, narrow, narrow,
            pltpu.VMEM((2, 1, D_MODEL), F32),
            pltpu.VMEM((2, N_HEADS, CHUNK, CHUNK), BF16)]


def _scan(layer, bsz, q, kf, kb, v, gfh, gfl, gbh, gbl):
    seq = q.shape[0] // bsz
    nt = seq // MIX_TILE
    to3 = lambda a: a.reshape(bsz, seq, D_MODEL)
    tile = lambda idx: pl.BlockSpec((1, MIX_TILE, D_MODEL), idx)
    fwd = lambda b, j: (b, j, 0)
    bwd = lambda b, j: (b, nt - 1 - j, 0)
    state = pltpu.VMEM((N_HEADS, HEAD_W, HEAD_W), F32)
    return pl.pallas_call(
        _scan_kernel,
        grid=(bsz, nt),
        in_specs=[tile(fwd), tile(bwd), tile(fwd), tile(bwd), tile(fwd), tile(bwd),
                  tile(fwd), tile(fwd), tile(bwd), tile(bwd)],
        out_specs=[tile(fwd), tile(bwd)],
        out_shape=[jax.ShapeDtypeStruct((bsz, seq, D_MODEL), F32),
                   jax.ShapeDtypeStruct((bsz, seq, D_MODEL), F32)],
        scratch_shapes=[state, state] + _scan_bufs() + _scan_bufs(),
        compiler_params=_params(2),
        name=f"scan_l{layer}",
    )(to3(q), to3(q), to3(kf), to3(kb), to3(v), to3(v), to3(gfh), to3(gfl), to3(gbh), to3(gbl))


def _post_kernel(of_ref, ob_ref, gate_ref, h_ref, ones_ref, onw_ref, wout_ref, out_ref):
    o = of_ref[...] + ob_ref[...]
    ms = _dot((o * o).astype(BF16), ones_ref[...]) * (1.0 / HEAD_W)
    y = o * lax.rsqrt(ms + HEAD_EPS) * onw_ref[...]
    g = gate_ref[...].astype(F32)
    y = y * (g * _sigmoid(g))
    out_ref[...] = h_ref[...] + _dot(y.astype(BF16), wout_ref[...])


def _post(layer, o_f, o_b, gates, h, gla_onorm, hg_onorm, w_out):
    t = h.shape[0]
    head_id = jnp.arange(D_MODEL) // HEAD_W
    ones = (head_id[:, None] == head_id[None, :]).astype(BF16)
    onw = jnp.concatenate([jnp.tile(gla_onorm, 4), jnp.tile(hg_onorm, 4)]).reshape(1, D_MODEL)
    row = lambda: pl.BlockSpec((ROW_TILE, D_MODEL), lambda i: (i, 0))
    return pl.pallas_call(
        _post_kernel,
        grid=(t // ROW_TILE,),
        in_specs=[row(), row(), row(), row(),
                  _const_spec((D_MODEL, D_MODEL)), _const_spec((1, D_MODEL)),
                  _const_spec((D_MODEL, D_MODEL))],
        out_specs=row(),
        out_shape=jax.ShapeDtypeStruct((t, D_MODEL), F32),
        compiler_params=_params(1),
        name=f"post_l{layer}",
    )(o_f, o_b, gates, h, ones, onw, w_out.astype(BF16))


def _ple_tail(h2, p, nple, wpp_ref, wpg_ref, nfinal, final):
    gate = _sigmoid(_dot(_rmsnorm(h2, nple).astype(BF16), wpg_ref[...]))
    h3 = h2 + _dot(p.astype(BF16), wpp_ref[...]) * gate
    if final:
        h3 = _rmsnorm(h3, nfinal)
    return h3


def _dense_kernel(final, h_ref, p_ref, nffn_ref, wg_ref, wu_ref, wd_ref, nple_ref,
                  wpp_ref, wpg_ref, nfin_ref, out_ref):
    h1 = h_ref[...]
    v = _rmsnorm(h1, nffn_ref[...]).astype(BF16)
    acc = jnp.zeros_like(h1)
    for c in range(0, D_FF_DENSE, FF_CHUNK):
        a = _dot(v, wg_ref[:, c:c + FF_CHUNK])
        b = _dot(v, wu_ref[:, c:c + FF_CHUNK])
        acc = acc + _dot((a * _sigmoid(a) * b).astype(BF16), wd_ref[c:c + FF_CHUNK, :])
    out_ref[...] = _ple_tail(h1 + acc, p_ref[...], nple_ref[...], wpp_ref, wpg_ref,
                             nfin_ref[...], final)


def _dense_ffn(layer, final, h, p, norm_ffn, wg, wu, wd, norm_ple, w_pp, w_pg, norm_final):
    t = h.shape[0]
    vec = lambda w: w.reshape(1, D_MODEL)
    return pl.pallas_call(
        functools.partial(_dense_kernel, final),
        grid=(t // ROW_TILE,),
        in_specs=[pl.BlockSpec((ROW_TILE, D_MODEL), lambda i: (i, 0)),
                  pl.BlockSpec((ROW_TILE, PLE_DIM), lambda i: (i, 0)),
                  _const_spec((1, D_MODEL)),
                  _const_spec((D_MODEL, D_FF_DENSE)), _const_spec((D_MODEL, D_FF_DENSE)),
                  _const_spec((D_FF_DENSE, D_MODEL)),
                  _const_spec((1, D_MODEL)),
                  _const_spec((PLE_DIM, D_MODEL)), _const_spec((D_MODEL, D_MODEL)),
                  _const_spec((1, D_MODEL))],
        out_specs=pl.BlockSpec((ROW_TILE, D_MODEL), lambda i: (i, 0)),
        out_shape=jax.ShapeDtypeStruct((t, D_MODEL), F32),
        compiler_params=_params(1),
        name=f"dense_l{layer}",
    )(h, p, vec(norm_ffn), wg.astype(BF16), wu.astype(BF16), wd.astype(BF16),
      vec(norm_ple), w_pp.astype(BF16), w_pg.astype(BF16), vec(norm_final))


META_G1, META_G2, META_E1, META_E2, META_R1, META_R2 = range(6)
HALF = D_MODEL // 2
HI_MASK = 0xFFFF0000


def _pack_rows(x):
    bits = pltpu.bitcast(x.astype(BF16).astype(F32), jnp.uint32)
    return (bits[:, :HALF] >> 16) | (bits[:, HALF:] & jnp.uint32(HI_MASK))


def _unpack_rows(w):
    lo = pltpu.bitcast(w << 16, F32)
    hi = pltpu.bitcast(w & jnp.uint32(HI_MASK), F32)
    return jnp.concatenate([lo, hi], axis=1)


def _router_kernel(h_ref, nffn_ref, wr_ref, tri_ref, xp_ref, meta_ref, cnt_ref, base_ref):
    @pl.when(pl.program_id(0) == 0)
    def _():
        base_ref[...] = jnp.zeros_like(base_ref)

    vf = _rmsnorm(h_ref[...], nffn_ref[...])
    xp_ref[...] = _pack_rows(vf)
    v_hi = vf.astype(BF16)
    v_lo = (vf - v_hi.astype(F32)).astype(BF16)
    w = wr_ref[...]
    w_hi = w.astype(BF16)
    w_lo = (w - w_hi.astype(F32)).astype(BF16)
    logits = _dot(v_hi, w_hi) + _dot(v_lo, w_hi) + _dot(v_hi, w_lo)
    lane = lax.broadcasted_iota(jnp.int32, logits.shape, 1)
    neg = jnp.float32(-jnp.inf)
    lg = jnp.where(lane < N_EXPERTS, logits, neg)
    m1 = jnp.max(lg, axis=-1, keepdims=True)
    i1 = jnp.min(jnp.where(lg == m1, lane, LANES), axis=-1, keepdims=True)
    first = lane == i1
    lg2 = jnp.where(first, neg, lg)
    m2 = jnp.max(lg2, axis=-1, keepdims=True)
    i2 = jnp.min(jnp.where(lg2 == m2, lane, LANES), axis=-1, keepdims=True)
    second = lane == i2
    e = jnp.exp(m2 - m1)
    g1 = 1.0 / (1.0 + e)
    g2 = e * g1
    cnt = jnp.where(first | second, 1.0, 0.0)
    rank = base_ref[...] + _dot(tri_ref[...], cnt.astype(BF16))
    r1 = jnp.sum(jnp.where(first, rank, 0.0), axis=-1, keepdims=True)
    r2 = jnp.sum(jnp.where(second, rank, 0.0), axis=-1, keepdims=True)
    base_ref[...] = base_ref[...] + jnp.sum(cnt, axis=0, keepdims=True)
    cnt_ref[...] = base_ref[...]
    meta = jnp.zeros(logits.shape, F32)
    for col, val in ((META_G1, g1), (META_G2, g2), (META_E1, i1.astype(F32)),
                     (META_E2, i2.astype(F32)), (META_R1, r1), (META_R2, r2)):
        meta = jnp.where(lane == col, val, meta)
    meta_ref[...] = meta


def _router(layer, h, norm_ffn, w_router):
    t = h.shape[0]
    wr = jnp.pad(w_router, ((0, 0), (0, LANES - N_EXPERTS)))
    ids = jnp.arange(ROW_TILE)
    tri = (ids[None, :] < ids[:, None]).astype(BF16)
    return pl.pallas_call(
        _router_kernel,
        grid=(t // ROW_TILE,),
        in_specs=[pl.BlockSpec((ROW_TILE, D_MODEL), lambda i: (i, 0)),
                  _const_spec((1, D_MODEL)), _const_spec((D_MODEL, LANES)),
                  _const_spec((ROW_TILE, ROW_TILE))],
        out_specs=[pl.BlockSpec((ROW_TILE, HALF), lambda i: (i, 0)),
                   pl.BlockSpec((ROW_TILE, LANES), lambda i: (i, 0)),
                   pl.BlockSpec((1, LANES), lambda i: (0, 0))],
        out_shape=[jax.ShapeDtypeStruct((t, HALF), jnp.uint32),
                   jax.ShapeDtypeStruct((t, LANES), F32),
                   jax.ShapeDtypeStruct((1, LANES), F32)],
        scratch_shapes=[pltpu.VMEM((1, LANES), F32)],
        compiler_params=_params(1),
        name=f"router_l{layer}",
    )(h, norm_ffn.reshape(1, D_MODEL), wr, tri)


SC_CORES, SC_SUBCORES = 2, 16
SC_WORKERS = SC_CORES * SC_SUBCORES
SC_WINDOW = 64


def _sc_gather(table, idx):
    n_out, d = idx.shape[0], table.shape[1]
    per_worker = n_out // SC_WORKERS
    n_win = per_worker // SC_WINDOW
    assert n_win * SC_WINDOW * SC_WORKERS == n_out and n_win % 2 == 0
    mesh = plsc.VectorSubcoreMesh(core_axis_name="c", subcore_axis_name="s")

    def body(table_hbm, idx_hbm, out_hbm, idx_a, idx_b, rows_a, rows_b, sem_a, sem_b):
        wid = lax.axis_index("s") * SC_CORES + lax.axis_index("c")

        def rows_of(w):
            return pl.ds(pl.multiple_of(wid * per_worker + w * SC_WINDOW, SC_WINDOW), SC_WINDOW)

        def start(w, idx_v, rows_v, sem):
            pltpu.sync_copy(idx_hbm.at[rows_of(w)], idx_v)
            pltpu.async_copy(table_hbm.at[idx_v], rows_v, sem)

        def finish(w, idx_v, rows_v, sem):
            pltpu.make_async_copy(table_hbm.at[idx_v], rows_v, sem).wait()
            pltpu.sync_copy(rows_v, out_hbm.at[rows_of(w)])

        start(0, idx_a, rows_a, sem_a)

        @pl.loop(0, n_win, step=2)
        def _(w):
            start(w + 1, idx_b, rows_b, sem_b)
            finish(w, idx_a, rows_a, sem_a)

            @pl.when(w + 2 < n_win)
            def _():
                start(w + 2, idx_a, rows_a, sem_a)

            finish(w + 1, idx_b, rows_b, sem_b)

    return pl.kernel(
        body, mesh=mesh,
        out_type=jax.ShapeDtypeStruct((n_out, d), table.dtype),
        scratch_types=[pltpu.VMEM((SC_WINDOW,), jnp.int32), pltpu.VMEM((SC_WINDOW,), jnp.int32),
                       pltpu.VMEM((SC_WINDOW, d), table.dtype),
                       pltpu.VMEM((SC_WINDOW, d), table.dtype),
                       pltpu.SemaphoreType.DMA, pltpu.SemaphoreType.DMA],
    )(table, idx)


EXP_ROWS = 1024


def _experts_kernel(be_ref, na_ref, xs_ref, wg_ref, wu_ref, wd_ref, ys_ref, x_ref, acc_ref):
    b = pl.program_id(0)
    f = pl.program_id(1)

    @pl.when(b < na_ref[0])
    def _():
        @pl.when(f == 0)
        def _():
            x_ref[...] = _unpack_rows(xs_ref[...]).astype(BF16)
            acc_ref[...] = jnp.zeros_like(acc_ref)

        x = x_ref[...]
        a = _dot(x, wg_ref[0, 0].astype(BF16))
        u = _dot(x, wu_ref[0, 0].astype(BF16))
        acc_ref[...] += _dot((a * _sigmoid(a) * u).astype(BF16), wd_ref[0, 0].astype(BF16))

        @pl.when(f == pl.num_programs(1) - 1)
        def _():
            ys_ref[...] = _pack_rows(acc_ref[...])


def _experts(layer, xs, blk_expert, n_active, wg, wu, wd):
    moe_idx = layer // 2
    rows = xs.shape[0]
    nf = D_FF_EXPERT // EXP_FF_CHUNK
    fsel = lambda b, f, na: jnp.where(b < na[0], f, nf - 1)
    grid_spec = pltpu.PrefetchScalarGridSpec(
        num_scalar_prefetch=2,
        grid=(rows // EXP_ROWS, nf),
        in_specs=[pl.BlockSpec((EXP_ROWS, HALF), lambda b, f, be, na: (b, 0)),
                  pl.BlockSpec((1, 1, D_MODEL, EXP_FF_CHUNK),
                               lambda b, f, be, na: (moe_idx, be[b], 0, fsel(b, f, na))),
                  pl.BlockSpec((1, 1, D_MODEL, EXP_FF_CHUNK),
                               lambda b, f, be, na: (moe_idx, be[b], 0, fsel(b, f, na))),
                  pl.BlockSpec((1, 1, EXP_FF_CHUNK, D_MODEL),
                               lambda b, f, be, na: (moe_idx, be[b], fsel(b, f, na), 0))],
        out_specs=pl.BlockSpec((EXP_ROWS, HALF), lambda b, f, be, na: (b, 0)),
        scratch_shapes=[pltpu.VMEM((EXP_ROWS, D_MODEL), BF16),
                        pltpu.VMEM((EXP_ROWS, D_MODEL), F32)])
    return pl.pallas_call(
        _experts_kernel,
        grid_spec=grid_spec,
        out_shape=jax.ShapeDtypeStruct((rows, HALF), jnp.uint32),
        compiler_params=_params(2),
        name=f"experts_l{layer}",
    )(blk_expert, n_active, xs, wg, wu, wd)


def _combine_kernel(final, h_ref, y1_ref, y2_ref, meta_ref, p_ref, nple_ref, wpp_ref, wpg_ref,
                    nfin_ref, out_ref):
    meta = meta_ref[...]
    g1 = meta[:, META_G1:META_G1 + 1]
    g2 = meta[:, META_G2:META_G2 + 1]
    h2 = h_ref[...] + g1 * _unpack_rows(y1_ref[...]) + g2 * _unpack_rows(y2_ref[...])
    out_ref[...] = _ple_tail(h2, p_ref[...], nple_ref[...], wpp_ref, wpg_ref, nfin_ref[...], final)


def _combine(layer, final, h, yg, meta, p, norm_ple, w_pp, w_pg, norm_final):
    t = h.shape[0]
    nt = t // ROW_TILE
    vec = lambda w: w.reshape(1, D_MODEL)
    return pl.pallas_call(
        functools.partial(_combine_kernel, final),
        grid=(nt,),
        in_specs=[pl.BlockSpec((ROW_TILE, D_MODEL), lambda i: (i, 0)),
                  pl.BlockSpec((ROW_TILE, HALF), lambda i: (i, 0)),
                  pl.BlockSpec((ROW_TILE, HALF), lambda i: (i + nt, 0)),
                  pl.BlockSpec((ROW_TILE, LANES), lambda i: (i, 0)),
                  pl.BlockSpec((ROW_TILE, PLE_DIM), lambda i: (i, 0)),
                  _const_spec((1, D_MODEL)),
                  _const_spec((PLE_DIM, D_MODEL)), _const_spec((D_MODEL, D_MODEL)),
                  _const_spec((1, D_MODEL))],
        out_specs=pl.BlockSpec((ROW_TILE, D_MODEL), lambda i: (i, 0)),
        out_shape=jax.ShapeDtypeStruct((t, D_MODEL), F32),
        compiler_params=_params(1),
        name=f"combine_l{layer}",
    )(h, yg, yg, meta, p, vec(norm_ple), w_pp.astype(BF16), w_pg.astype(BF16), vec(norm_final))


def _moe(layer, final, h1, p, norm_ffn, w_router, wg, wu, wd, norm_ple, w_pp, w_pg, norm_final):
    t = h1.shape[0]
    xp, meta, counts = _router(layer, h1, norm_ffn, w_router)
    cnt = counts[0, :N_EXPERTS].astype(jnp.int32)
    padded = ((cnt + EXP_ROWS - 1) // EXP_ROWS) * EXP_ROWS
    ends = jnp.cumsum(padded)
    offs = ends - padded
    e1 = meta[:, META_E1].astype(jnp.int32)
    e2 = meta[:, META_E2].astype(jnp.int32)
    pos1 = offs[e1] + meta[:, META_R1].astype(jnp.int32)
    pos2 = offs[e2] + meta[:, META_R2].astype(jnp.int32)
    rows = 2 * t + N_EXPERTS * EXP_ROWS
    tok = jnp.arange(t, dtype=jnp.int32)
    tok_sorted = (jnp.arange(rows, dtype=jnp.int32) % t).at[jnp.concatenate([pos1, pos2])].set(
        jnp.concatenate([tok, tok]), unique_indices=True)
    blk_start = jnp.arange(rows // EXP_ROWS, dtype=jnp.int32) * EXP_ROWS
    blk_expert = jnp.minimum(
        jnp.sum(blk_start[:, None] >= ends[None, :], axis=1), N_EXPERTS - 1).astype(jnp.int32)
    n_active = (ends[-1:] // EXP_ROWS).astype(jnp.int32)

    xs = _sc_gather(xp, tok_sorted)
    ys = _experts(layer, xs, blk_expert, n_active, wg, wu, wd)
    yg = _sc_gather(ys, jnp.concatenate([pos1, pos2]))
    return _combine(layer, final, h1, yg, meta, p, norm_ple, w_pp, w_pg, norm_final)


def kernel(x, p, w_in, gla_wg2, gla_bg, hg_lb_logits, gla_onorm, hg_onorm, w_out, norm_mix,
           norm_ffn, w_dense_gate, w_dense_up, w_dense_down, w_router, w_exp_gate, w_exp_up,
           w_exp_down, w_ple_proj, w_ple_gate, norm_ple, norm_final):
    bsz, seq, _ = x.shape
    t = bsz * seq
    h = x.reshape(t, D_MODEL)
    for i in range(DEPTH):
        final = i == DEPTH - 1
        q, kf, kb, v, gates, gfh, gfl, gbh, gbl = _proj(
            i, h, norm_mix[i], w_in[i], gla_wg2[i], gla_bg[i], hg_lb_logits)
        o_f, o_b = _scan(i, bsz, q, kf, kb, v, gfh, gfl, gbh, gbl)
        h1 = _post(i, o_f.reshape(t, D_MODEL), o_b.reshape(t, D_MODEL), gates, h,
                   gla_onorm[i], hg_onorm[i], w_out[i])
        p_i = p[i].reshape(t, PLE_DIM)
        j = i // 2
        if i % 2 == 0:
            h = _dense_ffn(i, final, h1, p_i, norm_ffn[i], w_dense_gate[j], w_dense_up[j],
                           w_dense_down[j], norm_ple[i], w_ple_proj[i], w_ple_gate[i],
                           norm_final)
        else:
            h = _moe(i, final, h1, p_i, norm_ffn[i], w_router[j], w_exp_gate, w_exp_up,
                     w_exp_down, norm_ple[i], w_ple_proj[i], w_ple_gate[i], norm_final)
    return h.reshape(bsz, seq, D_MODEL)
```

```python
import functools

import jax
import jax.numpy as jnp
from jax import lax
from jax.experimental import pallas as pl
from jax.experimental.pallas import tpu as pltpu
from jax.experimental.pallas import tpu_sc as plsc

F32 = jnp.float32
BF16 = jnp.bfloat16

D_MODEL = 1024
DEPTH = 4
N_HEADS = 8
HEAD_W = 128
GLA_DK = 64
GLA_RANK = 16
GLA_GATE_NORM = 16.0
HG_DK = 128
HG_K = 512
D_FF_DENSE = 2816
N_EXPERTS = 8
D_FF_EXPERT = 3584
PLE_DIM = 256
EPS = 1e-6
HEAD_EPS = 1e-5
F_MIN = 1e-6

CHUNK = 64
SUB = 16
EXP2_CLAMP = 86.0
LANES = 128
VMEM_LIMIT = 56 * 1024 * 1024

MIX_TILE = 256
ROW_TILE = 512
FF_CHUNK = 256
EXP_FF_CHUNK = 512

COL_HQ = 0
COL_HF = 512
COL_GLR = 1536
COL_GQ = 1792
COL_GK = 2304
COL_V = 2816
COL_GATE = 3840
NP_COLS = 4864
COL_GROUPS = ((0, COL_GQ), (COL_GQ, COL_V), (COL_V, NP_COLS))
LOG2E = 1.4426950408889634


def _dot(a, b):
    return jnp.dot(a, b, preferred_element_type=F32)


def _dot_nt(a, b):
    return lax.dot_general(a, b, (((1,), (1,)), ((), ())), preferred_element_type=F32)


def _dot_tn(a, b):
    return lax.dot_general(a, b, (((0,), (0,)), ((), ())), preferred_element_type=F32)


def _sigmoid(x):
    return 1.0 / (1.0 + jnp.exp(-x))


def _rmsnorm(x, w):
    ms = jnp.mean(x * x, axis=-1, keepdims=True)
    return x * lax.rsqrt(ms + EPS) * w


def _const_spec(shape):
    nd = len(shape)
    return pl.BlockSpec(shape, lambda *_: (0,) * nd, pipeline_mode=pl.Buffered(1))


def _layer_spec(shape, layer):
    nd = len(shape)
    return pl.BlockSpec((1,) + tuple(shape), lambda *_: (layer,) + (0,) * nd,
                        pipeline_mode=pl.Buffered(1))


def _params(n_grid):
    return pltpu.CompilerParams(
        dimension_semantics=("arbitrary",) * n_grid, vmem_limit_bytes=VMEM_LIMIT)


def _proj_kernel(layer, h_ref, nw_ref, w_ref, wg2_ref, bg_ref, lbl_ref,
                 q_ref, kf_ref, kb_ref, v_ref, gate_ref, gfh_ref, gfl_ref, gbh_ref, gbl_ref,
                 u_ref, p1_ref, p2_ref, p3_ref):
    u_ref[...] = _rmsnorm(h_ref[...], nw_ref[...]).astype(BF16)
    for p_ref, (c0, c1) in zip((p1_ref, p2_ref, p3_ref), COL_GROUPS):
        p_ref[...] = _dot(u_ref[...], w_ref[0, :, c0:c1])

    def split(hi_ref, lo_ref, cols, g):
        hi = g.astype(BF16)
        hi_ref[:, cols] = hi
        lo_ref[:, cols] = (g - hi.astype(F32)).astype(BF16)

    hq = p1_ref[:, COL_HQ:COL_HQ + 512]
    q_ref[:, 512:1024] = (hq * _sigmoid(hq) * (HG_DK ** -0.5)).astype(BF16)
    for direction, (k_ref, hi_ref, lo_ref) in enumerate(
            ((kf_ref, gfh_ref, gfl_ref), (kb_ref, gbh_ref, gbl_ref))):
        c0 = COL_GLR + direction * LANES
        x = _dot(p1_ref[:, c0:c0 + LANES].astype(BF16), wg2_ref[direction].astype(BF16))
        x = x + bg_ref[direction]
        log_sig = jnp.minimum(x, 0.0) - jnp.log(1.0 + jnp.exp(-jnp.abs(x)))
        split(hi_ref, lo_ref, slice(0, 512), log_sig * (LOG2E / GLA_GATE_NORM))
        rows = [lbl_ref[2 * d + direction:2 * d + direction + 1, :] for d in range(DEPTH)]
        mx = functools.reduce(jnp.maximum, rows)
        ex = [jnp.exp(r - mx) for r in rows]
        lb = sum(ex[1:layer + 1], jnp.zeros_like(mx)) / sum(ex)
        c0 = COL_HF + direction * HG_K
        z = p1_ref[:, c0:c0 + HG_K]
        e = jnp.exp(-jnp.abs(z))
        s_big = 1.0 / (1.0 + e)
        s_small = e * s_big
        sig = jnp.where(z >= 0, s_big, s_small)
        nsig = jnp.where(z >= 0, s_small, s_big)
        f = lb + (1.0 - lb) * sig
        split(hi_ref, lo_ref, slice(512, 1024), jnp.log(jnp.maximum(f, F_MIN)) * LOG2E)
        k_ref[:, 512:1024] = ((1.0 - lb) * nsig).astype(BF16)

    q_ref[:, 0:512] = (p2_ref[:, 0:512] * (GLA_DK ** -0.5)).astype(BF16)
    gk = p2_ref[:, 512:1024].astype(BF16)
    kf_ref[:, 0:512] = gk
    kb_ref[:, 0:512] = gk
    v_ref[...] = p3_ref[:, 0:1024].astype(BF16)
    gate_ref[...] = p3_ref[:, 1024:2048].astype(BF16)


def _pad_heads(w, n_heads, width):
    lead = w.shape[:-1]
    w = w.reshape(lead + (n_heads, width))
    w = jnp.pad(w, [(0, 0)] * len(lead) + [(0, 0), (0, HEAD_W - width)])
    return w.reshape(lead + (n_heads * HEAD_W,))


def _proj_weights(w_in):
    gq, gk, gv, gr, glr_f, glr_b, hq, hf_f, hf_b, hv, hr = jnp.split(
        w_in, [256, 512, 1024, 1536, 1552, 1568, 2080, 2592, 3104, 3616], axis=-1)
    pad_r = lambda w: jnp.pad(w, ((0, 0), (0, 0), (0, LANES - GLA_RANK)))
    return jnp.concatenate([hq, hf_f, hf_b, pad_r(glr_f), pad_r(glr_b), _pad_heads(gq, 4, GLA_DK),
                            _pad_heads(gk, 4, GLA_DK), gv, hv, gr, hr], -1).astype(BF16)


def _proj(layer, h, norm_w, w_all, wg2, bg, lb_logits):
    t = h.shape[0]
    wg2_p = jnp.pad(_pad_heads(wg2, 4, GLA_DK), ((0, 0), (0, LANES - GLA_RANK), (0, 0)))
    bg_p = _pad_heads(bg, 4, GLA_DK).reshape(2, 1, 512)
    lbl = lb_logits.reshape(DEPTH * 2, HG_K)
    row = lambda: pl.BlockSpec((ROW_TILE, D_MODEL), lambda i: (i, 0))
    return pl.pallas_call(
        functools.partial(_proj_kernel, layer),
        grid=(t // ROW_TILE,),
        in_specs=[row(), _const_spec((1, D_MODEL)), _layer_spec((D_MODEL, NP_COLS), layer),
                  _const_spec((2, LANES, 512)), _const_spec((2, 1, 512)),
                  _const_spec((DEPTH * 2, HG_K))],
        out_specs=[row() for _ in range(9)],
        out_shape=[jax.ShapeDtypeStruct((t, D_MODEL), BF16) for _ in range(9)],
        scratch_shapes=[pltpu.VMEM((ROW_TILE, D_MODEL), BF16)] + [
            pltpu.VMEM((ROW_TILE, c1 - c0), F32) for c0, c1 in COL_GROUPS],
        compiler_params=_params(1),
        name=f"proj_l{layer}",
    )(h, norm_w.reshape(1, D_MODEL), w_all, wg2_p, bg_p, lbl)


A_GROUPS = 3
A_WIDTH = A_GROUPS * HEAD_W


def _stage_a(rev, q_ref, k_ref, gh_ref, gl_ref, r0, tri, bufs, slot):
    qa, ka, qd, kd, qi, ks, et, _, _ = bufs
    rows = slice(r0, r0 + CHUNK)
    cum = _dot(tri, jnp.concatenate([gh_ref[0, rows, :], gl_ref[0, rows, :]], axis=0))
    q = q_ref[0, rows, :]
    k = k_ref[0, rows, :]
    factor = lambda x: jnp.exp2(x).astype(BF16)

    def put(ref, rows, group, val):
        for n in range(N_HEADS):
            c0 = n * A_WIDTH + group * HEAD_W
            ref[slot, rows, c0:c0 + HEAD_W] = val[:, n * HEAD_W:(n + 1) * HEAD_W]

    for group, (half, a) in enumerate(((2 * SUB, 0), (SUB, 0), (SUB, 2 * SUB))):
        lo, hi = slice(a, a + half), slice(a + half, a + 2 * half)
        if not rev:
            ref, k_rows, q_rows = cum[a + half - 1:a + half], lo, hi
        else:
            ref, q_rows, k_rows = cum[a + half:a + half + 1], lo, hi
        put(qa, q_rows, group, q[q_rows] * factor(cum[q_rows] - ref))
        put(ka, k_rows, group, k[k_rows] * factor(ref - cum[k_rows]))

    mids = []
    for a in range(0, CHUNK, SUB):
        m = 0.5 * (cum[a:a + 1] + cum[a + SUB - 1:a + SUB])
        mids.append(jnp.broadcast_to(m, (SUB, D_MODEL)))
    dd = cum - jnp.concatenate(mids, 0)
    qd[slot] = q * factor(jnp.clip(dd, -EXP2_CLAMP, EXP2_CLAMP))
    kd[slot] = k * factor(jnp.clip(-dd, -EXP2_CLAMP, EXP2_CLAMP))

    tot = cum[0:1] if rev else cum[CHUNK - 1:CHUNK]
    qi[slot] = q * factor(cum)
    ks[slot] = k * factor(tot - cum)
    et[slot] = jnp.exp2(tot)


def _stage_b(bufs, slot, v_ref, st_ref, r0, diag):
    qa, ka, qd, kd, qi, ks, et, ab, oi = bufs
    for n in range(N_HEADS):
        wide = slice(n * A_WIDTH, (n + 1) * A_WIDTH)
        head = slice(n * HEAD_W, (n + 1) * HEAD_W)
        scores = _dot_nt(qa[slot, :, wide], ka[slot, :, wide])
        scores = scores + jnp.where(diag, _dot_nt(qd[slot, :, head], kd[slot, :, head]), 0.0)
        ab[slot, n] = scores.astype(BF16)
        st = st_ref[n]
        oi[slot, :, head] = _dot_nt(qi[slot, :, head], st.astype(BF16))
        st_ref[n] = st * et[slot, :, head] + _dot_tn(v_ref[0, r0:r0 + CHUNK, head],
                                                    ks[slot, :, head])


def _stage_c(bufs, slot, v_ref, o_ref, r0):
    ab, oi = bufs[7], bufs[8]
    for n in range(N_HEADS):
        head = slice(n * HEAD_W, (n + 1) * HEAD_W)
        o = oi[slot, :, head] + _dot(ab[slot, n], v_ref[0, r0:r0 + CHUNK, head])
        o_ref[0, r0:r0 + CHUNK, head] = o.astype(o_ref.dtype)


def _scan_kernel(qf_ref, qb_ref, kf_ref, kb_ref, vf_ref, vb_ref, gfh_ref, gfl_ref, gbh_ref,
                 gbl_ref, of_ref, ob_ref, sf_ref, sb_ref, *buf_refs):
    n_chunks = MIX_TILE // CHUNK
    bufs_f, bufs_b = buf_refs[:len(buf_refs) // 2], buf_refs[len(buf_refs) // 2:]

    @pl.when(pl.program_id(1) == 0)
    def _():
        sf_ref[...] = jnp.zeros_like(sf_ref)
        sb_ref[...] = jnp.zeros_like(sb_ref)
        for bufs in (bufs_f, bufs_b):
            bufs[0][...] = jnp.zeros_like(bufs[0])
            bufs[1][...] = jnp.zeros_like(bufs[1])

    ri = lax.broadcasted_iota(jnp.int32, (CHUNK, CHUNK), 0)
    ci = lax.broadcasted_iota(jnp.int32, (CHUNK, CHUNK), 1)
    tri_f = (ci <= ri).astype(BF16)
    tri_b = (ci >= ri).astype(BF16)
    tri_f = jnp.concatenate([tri_f, tri_f], axis=1)
    tri_b = jnp.concatenate([tri_b, tri_b], axis=1)
    same16 = (ri // SUB) == (ci // SUB)
    diag_f = same16 & (ci <= ri)
    diag_b = same16 & (ci >= ri)
    row_f = lambda c: c * CHUNK
    row_b = lambda c: (n_chunks - 1 - c) * CHUNK

    def stage_a(c):
        _stage_a(False, qf_ref, kf_ref, gfh_ref, gfl_ref, row_f(c), tri_f, bufs_f, c % 2)
        _stage_a(True, qb_ref, kb_ref, gbh_ref, gbl_ref, row_b(c), tri_b, bufs_b, c % 2)

    stage_a(0)
    for c in range(n_chunks):
        if c + 1 < n_chunks:
            stage_a(c + 1)
        _stage_b(bufs_f, c % 2, vf_ref, sf_ref, row_f(c), diag_f)
        _stage_b(bufs_b, c % 2, vb_ref, sb_ref, row_b(c), diag_b)
        _stage_c(bufs_f, c % 2, vf_ref, of_ref, row_f(c))
        _stage_c(bufs_b, c % 2, vb_ref, ob_ref, row_b(c))


def _scan_bufs():
    wide = pltpu.VMEM((2, CHUNK, N_HEADS * A_WIDTH), BF16)
    narrow = pltpu.VMEM((2, CHUNK, D_MODEL), BF16)
    return [wide, wide, narrow, narrow, narrow, narrow,
            pltpu.VMEM((2, 1, D_MODEL), F32),
            pltpu.VMEM((2, N_HEADS, CHUNK, CHUNK), BF16),
            pltpu.VMEM((2, CHUNK, D_MODEL), F32)]


def _scan(layer, bsz, q, kf, kb, v, gfh, gfl, gbh, gbl):
    seq = q.shape[0] // bsz
    nt = seq // MIX_TILE
    to3 = lambda a: a.reshape(bsz, seq, D_MODEL)
    tile = lambda idx: pl.BlockSpec((1, MIX_TILE, D_MODEL), idx)
    fwd = lambda b, j: (b, j, 0)
    bwd = lambda b, j: (b, nt - 1 - j, 0)
    state = pltpu.VMEM((N_HEADS, HEAD_W, HEAD_W), F32)
    return pl.pallas_call(
        _scan_kernel,
        grid=(bsz, nt),
        in_specs=[tile(fwd), tile(bwd), tile(fwd), tile(bwd), tile(fwd), tile(bwd),
                  tile(fwd), tile(fwd), tile(bwd), tile(bwd)],
        out_specs=[tile(fwd), tile(bwd)],
        out_shape=[jax.ShapeDtypeStruct((bsz, seq, D_MODEL), BF16),
                   jax.ShapeDtypeStruct((bsz, seq, D_MODEL), BF16)],
        scratch_shapes=[state, state] + _scan_bufs() + _scan_bufs(),
        compiler_params=_params(2),
        name=f"scan_l{layer}",
    )(to3(q), to3(q), to3(kf), to3(kb), to3(v), to3(v), to3(gfh), to3(gfl), to3(gbh), to3(gbl))


def _post_math(of_ref, ob_ref, gate_ref, ones_ref, onw_ref, wout_ref, h):
    o = of_ref[...].astype(F32) + ob_ref[...].astype(F32)
    sq = (o * o).astype(BF16)
    ms = jnp.concatenate([_dot(sq[:, n * HEAD_W:(n + 1) * HEAD_W], ones_ref[...])
                          for n in range(N_HEADS)], axis=1) * (1.0 / HEAD_W)
    y = o * lax.rsqrt(ms + HEAD_EPS) * onw_ref[...]
    g = gate_ref[...].astype(F32)
    y = y * (g * _sigmoid(g))
    return h + _dot(y.astype(BF16), wout_ref[0])


def _post_inputs(layer, o_f, o_b, gates, gla_onorm, hg_onorm, w_out_all):
    onw = jnp.concatenate([jnp.tile(gla_onorm, 4), jnp.tile(hg_onorm, 4)]).reshape(1, D_MODEL)
    row = lambda: pl.BlockSpec((ROW_TILE, D_MODEL), lambda i: (i, 0))
    specs = [row(), row(), row(), _const_spec((HEAD_W, HEAD_W)), _const_spec((1, D_MODEL)),
             _layer_spec((D_MODEL, D_MODEL), layer)]
    return (o_f, o_b, gates, jnp.ones((HEAD_W, HEAD_W), BF16), onw, w_out_all), specs


def _ple_tail(h2, p, nple, wpp_ref, wpg_ref, nfinal, final):
    gate = _sigmoid(_dot(_rmsnorm(h2, nple).astype(BF16), wpg_ref[0]))
    h3 = h2 + _dot(p.astype(BF16), wpp_ref[0]) * gate
    if final:
        h3 = _rmsnorm(h3, nfinal)
    return h3


def _dense_kernel(final, of_ref, ob_ref, gate_ref, ones_ref, onw_ref, wout_ref, h_ref, p_ref,
                  nffn_ref, wg_ref, wu_ref, wd_ref, nple_ref, wpp_ref, wpg_ref, nfin_ref, out_ref):
    h1 = _post_math(of_ref, ob_ref, gate_ref, ones_ref, onw_ref, wout_ref, h_ref[...])
    v = _rmsnorm(h1, nffn_ref[...]).astype(BF16)
    acc = jnp.zeros_like(h1)
    for c in range(0, D_FF_DENSE, FF_CHUNK):
        a = _dot(v, wg_ref[0, :, c:c + FF_CHUNK])
        b = _dot(v, wu_ref[0, :, c:c + FF_CHUNK])
        acc = acc + _dot((a * _sigmoid(a) * b).astype(BF16), wd_ref[0, c:c + FF_CHUNK, :])
    out_ref[...] = _ple_tail(h1 + acc, p_ref[0], nple_ref[...], wpp_ref, wpg_ref,
                             nfin_ref[...], final)


def _dense_ffn(layer, final, post, h, p, norm_ffn, wg, wu, wd, norm_ple, w_pp, w_pg, norm_final):
    t = h.shape[0]
    vec = lambda w: w.reshape(1, D_MODEL)
    j = layer // 2
    return pl.pallas_call(
        functools.partial(_dense_kernel, final),
        grid=(t // ROW_TILE,),
        in_specs=post[1] + [
            pl.BlockSpec((ROW_TILE, D_MODEL), lambda i: (i, 0)),
            pl.BlockSpec((1, ROW_TILE, PLE_DIM), lambda i: (layer, i, 0)),
            _const_spec((1, D_MODEL)),
            _layer_spec((D_MODEL, D_FF_DENSE), j), _layer_spec((D_MODEL, D_FF_DENSE), j),
            _layer_spec((D_FF_DENSE, D_MODEL), j),
            _const_spec((1, D_MODEL)),
            _layer_spec((PLE_DIM, D_MODEL), layer), _layer_spec((D_MODEL, D_MODEL), layer),
            _const_spec((1, D_MODEL))],
        out_specs=pl.BlockSpec((ROW_TILE, D_MODEL), lambda i: (i, 0)),
        out_shape=jax.ShapeDtypeStruct((t, D_MODEL), F32),
        compiler_params=_params(1),
        name=f"dense_l{layer}",
    )(*post[0], h, p, vec(norm_ffn), wg, wu, wd, vec(norm_ple), w_pp, w_pg, vec(norm_final))


META_G1, META_G2, META_E1, META_E2, META_R1, META_R2 = range(6)
HALF = D_MODEL // 2
HI_MASK = 0xFFFF0000


def _pack_rows(x):
    bits = pltpu.bitcast(x.astype(BF16).astype(F32), jnp.uint32)
    return (bits[:, :HALF] >> 16) | (bits[:, HALF:] & jnp.uint32(HI_MASK))


def _unpack_rows(w):
    lo = pltpu.bitcast(w << 16, F32)
    hi = pltpu.bitcast(w & jnp.uint32(HI_MASK), F32)
    return jnp.concatenate([lo, hi], axis=1)


def _router_kernel(of_ref, ob_ref, gate_ref, ones_ref, onw_ref, wout_ref, h_ref, nffn_ref, wr_ref,
                   tri_ref, h1_ref, xp_ref, meta_ref, cnt_ref, base_ref):
    @pl.when(pl.program_id(0) == 0)
    def _():
        base_ref[...] = jnp.zeros_like(base_ref)

    h1 = _post_math(of_ref, ob_ref, gate_ref, ones_ref, onw_ref, wout_ref, h_ref[...])
    h1_ref[...] = h1
    vf = _rmsnorm(h1, nffn_ref[...])
    xp_ref[...] = _pack_rows(vf)
    v_hi = vf.astype(BF16)
    v_lo = (vf - v_hi.astype(F32)).astype(BF16)
    w = wr_ref[...]
    w_hi = w.astype(BF16)
    w_lo = (w - w_hi.astype(F32)).astype(BF16)
    logits = _dot(v_hi, w_hi) + _dot(v_lo, w_hi) + _dot(v_hi, w_lo)
    lane = lax.broadcasted_iota(jnp.int32, logits.shape, 1)
    neg = jnp.float32(-jnp.inf)
    lg = jnp.where(lane < N_EXPERTS, logits, neg)
    m1 = jnp.max(lg, axis=-1, keepdims=True)
    i1 = jnp.min(jnp.where(lg == m1, lane, LANES), axis=-1, keepdims=True)
    first = lane == i1
    lg2 = jnp.where(first, neg, lg)
    m2 = jnp.max(lg2, axis=-1, keepdims=True)
    i2 = jnp.min(jnp.where(lg2 == m2, lane, LANES), axis=-1, keepdims=True)
    second = lane == i2
    e = jnp.exp(m2 - m1)
    g1 = 1.0 / (1.0 + e)
    g2 = e * g1
    cnt = jnp.where(first | second, 1.0, 0.0)
    rank = base_ref[...] + _dot(tri_ref[...], cnt.astype(BF16))
    r1 = jnp.sum(jnp.where(first, rank, 0.0), axis=-1, keepdims=True)
    r2 = jnp.sum(jnp.where(second, rank, 0.0), axis=-1, keepdims=True)
    base_ref[...] = base_ref[...] + jnp.sum(cnt, axis=0, keepdims=True)
    cnt_ref[...] = base_ref[...]
    meta = jnp.zeros(logits.shape, F32)
    for col, val in ((META_G1, g1), (META_G2, g2), (META_E1, i1.astype(F32)),
                     (META_E2, i2.astype(F32)), (META_R1, r1), (META_R2, r2)):
        meta = jnp.where(lane == col, val, meta)
    meta_ref[...] = meta


def _router(layer, post, h, norm_ffn, w_router):
    t = h.shape[0]
    wr = jnp.pad(w_router, ((0, 0), (0, LANES - N_EXPERTS)))
    ids = jnp.arange(ROW_TILE)
    tri = (ids[None, :] < ids[:, None]).astype(BF16)
    return pl.pallas_call(
        _router_kernel,
        grid=(t // ROW_TILE,),
        in_specs=post[1] + [
            pl.BlockSpec((ROW_TILE, D_MODEL), lambda i: (i, 0)),
            _const_spec((1, D_MODEL)), _const_spec((D_MODEL, LANES)),
            _const_spec((ROW_TILE, ROW_TILE))],
        out_specs=[pl.BlockSpec((ROW_TILE, D_MODEL), lambda i: (i, 0)),
                   pl.BlockSpec((ROW_TILE, HALF), lambda i: (i, 0)),
                   pl.BlockSpec((ROW_TILE, LANES), lambda i: (i, 0)),
                   pl.BlockSpec((1, LANES), lambda i: (0, 0))],
        out_shape=[jax.ShapeDtypeStruct((t, D_MODEL), F32),
                   jax.ShapeDtypeStruct((t, HALF), jnp.uint32),
                   jax.ShapeDtypeStruct((t, LANES), F32),
                   jax.ShapeDtypeStruct((1, LANES), F32)],
        scratch_shapes=[pltpu.VMEM((1, LANES), F32)],
        compiler_params=_params(1),
        name=f"router_l{layer}",
    )(*post[0], h, norm_ffn.reshape(1, D_MODEL), wr, tri)


SC_CORES, SC_SUBCORES = 2, 16
SC_WORKERS = SC_CORES * SC_SUBCORES
SC_WINDOW = 64


def _sc_gather(table, idx):
    n_out, d = idx.shape[0], table.shape[1]
    per_worker = n_out // SC_WORKERS
    n_win = per_worker // SC_WINDOW
    assert n_win * SC_WINDOW * SC_WORKERS == n_out and n_win % 2 == 0
    mesh = plsc.VectorSubcoreMesh(core_axis_name="c", subcore_axis_name="s")

    def body(table_hbm, idx_hbm, out_hbm, idx_a, idx_b, rows_a, rows_b, sem_a, sem_b):
        wid = lax.axis_index("s") * SC_CORES + lax.axis_index("c")

        def rows_of(w):
            return pl.ds(pl.multiple_of(wid * per_worker + w * SC_WINDOW, SC_WINDOW), SC_WINDOW)

        def start(w, idx_v, rows_v, sem):
            pltpu.sync_copy(idx_hbm.at[rows_of(w)], idx_v)
            pltpu.async_copy(table_hbm.at[idx_v], rows_v, sem)

        def finish(w, idx_v, rows_v, sem):
            pltpu.make_async_copy(table_hbm.at[idx_v], rows_v, sem).wait()
            pltpu.sync_copy(rows_v, out_hbm.at[rows_of(w)])

        start(0, idx_a, rows_a, sem_a)

        @pl.loop(0, n_win, step=2)
        def _(w):
            start(w + 1, idx_b, rows_b, sem_b)
            finish(w, idx_a, rows_a, sem_a)

            @pl.when(w + 2 < n_win)
            def _():
                start(w + 2, idx_a, rows_a, sem_a)

            finish(w + 1, idx_b, rows_b, sem_b)

    return pl.kernel(
        body, mesh=mesh,
        out_type=jax.ShapeDtypeStruct((n_out, d), table.dtype),
        scratch_types=[pltpu.VMEM((SC_WINDOW,), jnp.int32), pltpu.VMEM((SC_WINDOW,), jnp.int32),
                       pltpu.VMEM((SC_WINDOW, d), table.dtype),
                       pltpu.VMEM((SC_WINDOW, d), table.dtype),
                       pltpu.SemaphoreType.DMA, pltpu.SemaphoreType.DMA],
    )(table, idx)


EXP_ROWS = 1024


def _experts_kernel(be_ref, na_ref, xs_ref, wg_ref, wu_ref, wd_ref, ys_ref, x_ref, acc_ref):
    b = pl.program_id(0)
    f = pl.program_id(1)

    @pl.when(b < na_ref[0])
    def _():
        @pl.when(f == 0)
        def _():
            x_ref[...] = _unpack_rows(xs_ref[...]).astype(BF16)
            acc_ref[...] = jnp.zeros_like(acc_ref)

        x = x_ref[...]
        a = _dot(x, wg_ref[0, 0].astype(BF16))
        u = _dot(x, wu_ref[0, 0].astype(BF16))
        acc_ref[...] += _dot((a * _sigmoid(a) * u).astype(BF16), wd_ref[0, 0].astype(BF16))

        @pl.when(f == pl.num_programs(1) - 1)
        def _():
            ys_ref[...] = _pack_rows(acc_ref[...])


def _experts(layer, xs, blk_expert, n_active, wg, wu, wd):
    moe_idx = layer // 2
    rows = xs.shape[0]
    nf = D_FF_EXPERT // EXP_FF_CHUNK
    fsel = lambda b, f, na: jnp.where(b < na[0], f, nf - 1)
    grid_spec = pltpu.PrefetchScalarGridSpec(
        num_scalar_prefetch=2,
        grid=(rows // EXP_ROWS, nf),
        in_specs=[pl.BlockSpec((EXP_ROWS, HALF), lambda b, f, be, na: (b, 0)),
                  pl.BlockSpec((1, 1, D_MODEL, EXP_FF_CHUNK),
                               lambda b, f, be, na: (moe_idx, be[b], 0, fsel(b, f, na))),
                  pl.BlockSpec((1, 1, D_MODEL, EXP_FF_CHUNK),
                               lambda b, f, be, na: (moe_idx, be[b], 0, fsel(b, f, na))),
                  pl.BlockSpec((1, 1, EXP_FF_CHUNK, D_MODEL),
                               lambda b, f, be, na: (moe_idx, be[b], fsel(b, f, na), 0))],
        out_specs=pl.BlockSpec((EXP_ROWS, HALF), lambda b, f, be, na: (b, 0)),
        scratch_shapes=[pltpu.VMEM((EXP_ROWS, D_MODEL), BF16),
                        pltpu.VMEM((EXP_ROWS, D_MODEL), F32)])
    return pl.pallas_call(
        _experts_kernel,
        grid_spec=grid_spec,
        out_shape=jax.ShapeDtypeStruct((rows, HALF), jnp.uint32),
        compiler_params=_params(2),
        name=f"experts_l{layer}",
    )(blk_expert, n_active, xs, wg, wu, wd)


def _combine_kernel(final, h_ref, y1_ref, y2_ref, meta_ref, p_ref, nple_ref, wpp_ref, wpg_ref,
                    nfin_ref, out_ref):
    meta = meta_ref[...]
    g1 = meta[:, META_G1:META_G1 + 1]
    g2 = meta[:, META_G2:META_G2 + 1]
    h2 = h_ref[...] + g1 * _unpack_rows(y1_ref[...]) + g2 * _unpack_rows(y2_ref[...])
    out_ref[...] = _ple_tail(h2, p_ref[0], nple_ref[...], wpp_ref, wpg_ref, nfin_ref[...], final)


def _combine(layer, final, h, yg, meta, p, norm_ple, w_pp, w_pg, norm_final):
    t = h.shape[0]
    nt = t // ROW_TILE
    vec = lambda w: w.reshape(1, D_MODEL)
    return pl.pallas_call(
        functools.partial(_combine_kernel, final),
        grid=(nt,),
        in_specs=[pl.BlockSpec((ROW_TILE, D_MODEL), lambda i: (i, 0)),
                  pl.BlockSpec((ROW_TILE, HALF), lambda i: (i, 0)),
                  pl.BlockSpec((ROW_TILE, HALF), lambda i: (i + nt, 0)),
                  pl.BlockSpec((ROW_TILE, LANES), lambda i: (i, 0)),
                  pl.BlockSpec((1, ROW_TILE, PLE_DIM), lambda i: (layer, i, 0)),
                  _const_spec((1, D_MODEL)),
                  _layer_spec((PLE_DIM, D_MODEL), layer), _layer_spec((D_MODEL, D_MODEL), layer),
                  _const_spec((1, D_MODEL))],
        out_specs=pl.BlockSpec((ROW_TILE, D_MODEL), lambda i: (i, 0)),
        out_shape=jax.ShapeDtypeStruct((t, D_MODEL), F32),
        compiler_params=_params(1),
        name=f"combine_l{layer}",
    )(h, yg, yg, meta, p, vec(norm_ple), w_pp, w_pg, vec(norm_final))


def _moe(layer, final, post, h, p, norm_ffn, w_router, wg, wu, wd, norm_ple, w_pp, w_pg,
         norm_final):
    t = h.shape[0]
    h1, xp, meta, counts = _router(layer, post, h, norm_ffn, w_router)
    cnt = counts[0, :N_EXPERTS].astype(jnp.int32)
    padded = ((cnt + EXP_ROWS - 1) // EXP_ROWS) * EXP_ROWS
    ends = jnp.cumsum(padded)
    offs = ends - padded
    e1 = meta[:, META_E1].astype(jnp.int32)
    e2 = meta[:, META_E2].astype(jnp.int32)
    pos1 = offs[e1] + meta[:, META_R1].astype(jnp.int32)
    pos2 = offs[e2] + meta[:, META_R2].astype(jnp.int32)
    rows = 2 * t + N_EXPERTS * EXP_ROWS
    tok = jnp.arange(t, dtype=jnp.int32)
    tok_sorted = (jnp.arange(rows, dtype=jnp.int32) % t).at[jnp.concatenate([pos1, pos2])].set(
        jnp.concatenate([tok, tok]), unique_indices=True)
    blk_start = jnp.arange(rows // EXP_ROWS, dtype=jnp.int32) * EXP_ROWS
    blk_expert = jnp.minimum(
        jnp.sum(blk_start[:, None] >= ends[None, :], axis=1), N_EXPERTS - 1).astype(jnp.int32)
    n_active = (ends[-1:] // EXP_ROWS).astype(jnp.int32)

    xs = _sc_gather(xp, tok_sorted)
    ys = _experts(layer, xs, blk_expert, n_active, wg, wu, wd)
    yg = _sc_gather(ys, jnp.concatenate([pos1, pos2]))
    return _combine(layer, final, h1, yg, meta, p, norm_ple, w_pp, w_pg, norm_final)


def kernel(x, p, w_in, gla_wg2, gla_bg, hg_lb_logits, gla_onorm, hg_onorm, w_out, norm_mix,
           norm_ffn, w_dense_gate, w_dense_up, w_dense_down, w_router, w_exp_gate, w_exp_up,
           w_exp_down, w_ple_proj, w_ple_gate, norm_ple, norm_final):
    bsz, seq, _ = x.shape
    t = bsz * seq
    w_proj = _proj_weights(w_in)
    w_out_b, w_pp, w_pg = w_out.astype(BF16), w_ple_proj.astype(BF16), w_ple_gate.astype(BF16)
    w_dg, w_du, w_dd = (w.astype(BF16) for w in (w_dense_gate, w_dense_up, w_dense_down))
    p_all = p.reshape(DEPTH, t, PLE_DIM)
    h = x.reshape(t, D_MODEL)
    for i in range(DEPTH):
        final = i == DEPTH - 1
        q, kf, kb, v, gates, gfh, gfl, gbh, gbl = _proj(
            i, h, norm_mix[i], w_proj, gla_wg2[i], gla_bg[i], hg_lb_logits)
        o_f, o_b = _scan(i, bsz, q, kf, kb, v, gfh, gfl, gbh, gbl)
        post = _post_inputs(i, o_f.reshape(t, D_MODEL), o_b.reshape(t, D_MODEL), gates,
                            gla_onorm[i], hg_onorm[i], w_out_b)
        if i % 2 == 0:
            h = _dense_ffn(i, final, post, h, p_all, norm_ffn[i], w_dg, w_du, w_dd, norm_ple[i],
                           w_pp, w_pg, norm_final)
        else:
            h = _moe(i, final, post, h, p_all, norm_ffn[i], w_router[i // 2], w_exp_gate, w_exp_up,
                     w_exp_down, norm_ple[i], w_pp, w_pg, norm_final)
    return h.reshape(bsz, seq, D_MODEL)
```

```python
import functools

import jax
import jax.numpy as jnp
from jax import lax
from jax.experimental import pallas as pl
from jax.experimental.pallas import tpu as pltpu
from jax.experimental.pallas import tpu_sc as plsc

F32 = jnp.float32
BF16 = jnp.bfloat16

D_MODEL = 1024
DEPTH = 4
N_HEADS = 8
HEAD_W = 128
GLA_DK = 64
GLA_RANK = 16
GLA_GATE_NORM = 16.0
HG_DK = 128
HG_K = 512
D_FF_DENSE = 2816
N_EXPERTS = 8
D_FF_EXPERT = 3584
PLE_DIM = 256
EPS = 1e-6
HEAD_EPS = 1e-5
F_MIN = 1e-6

CHUNK = 64
SUB = 16
EXP2_CLAMP = 86.0
LANES = 128
VMEM_LIMIT = 56 * 1024 * 1024

MIX_TILE = 256
ROW_TILE = 512
FF_CHUNK = 256
EXP_FF_CHUNK = 512

COL_HQ = 0
COL_HF = 512
COL_GLR = 1536
COL_GQ = 1792
COL_GK = 2304
COL_V = 2816
COL_GATE = 3840
NP_COLS = 4864
COL_GROUPS = ((0, COL_GQ), (COL_GQ, COL_V), (COL_V, NP_COLS))
LOG2E = 1.4426950408889634


def _dot(a, b):
    return jnp.dot(a, b, preferred_element_type=F32)


def _dot_nt(a, b):
    return lax.dot_general(a, b, (((1,), (1,)), ((), ())), preferred_element_type=F32)


def _dot_tn(a, b):
    return lax.dot_general(a, b, (((0,), (0,)), ((), ())), preferred_element_type=F32)


def _sigmoid(x):
    return 1.0 / (1.0 + jnp.exp(-x))


def _rmsnorm(x, w):
    ms = jnp.mean(x * x, axis=-1, keepdims=True)
    return x * lax.rsqrt(ms + EPS) * w


def _const_spec(shape):
    nd = len(shape)
    return pl.BlockSpec(shape, lambda *_: (0,) * nd, pipeline_mode=pl.Buffered(1))


def _layer_spec(shape, layer):
    nd = len(shape)
    return pl.BlockSpec((1,) + tuple(shape), lambda *_: (layer,) + (0,) * nd,
                        pipeline_mode=pl.Buffered(1))


def _params(n_grid):
    return pltpu.CompilerParams(
        dimension_semantics=("arbitrary",) * n_grid, vmem_limit_bytes=VMEM_LIMIT)


def _proj_kernel(layer, h_ref, nw_ref, w_ref, wg2_ref, bg_ref, lbl_ref,
                 q_ref, kf_ref, kb_ref, v_ref, gate_ref, gfh_ref, gfl_ref, gbh_ref, gbl_ref,
                 u_ref, p1_ref, p2_ref, p3_ref):
    u_ref[...] = _rmsnorm(h_ref[...], nw_ref[...]).astype(BF16)
    for p_ref, (c0, c1) in zip((p1_ref, p2_ref, p3_ref), COL_GROUPS):
        p_ref[...] = _dot(u_ref[...], w_ref[0, :, c0:c1])

    def split(hi_ref, lo_ref, cols, g):
        hi = g.astype(BF16)
        hi_ref[:, cols] = hi
        lo_ref[:, cols] = (g - hi.astype(F32)).astype(BF16)

    hq = p1_ref[:, COL_HQ:COL_HQ + 512]
    q_ref[:, 512:1024] = (hq * _sigmoid(hq) * (HG_DK ** -0.5)).astype(BF16)
    for direction, (k_ref, hi_ref, lo_ref) in enumerate(
            ((kf_ref, gfh_ref, gfl_ref), (kb_ref, gbh_ref, gbl_ref))):
        c0 = COL_GLR + direction * LANES
        x = _dot(p1_ref[:, c0:c0 + LANES].astype(BF16), wg2_ref[direction].astype(BF16))
        x = x + bg_ref[direction]
        log_sig = jnp.minimum(x, 0.0) - jnp.log(1.0 + jnp.exp(-jnp.abs(x)))
        split(hi_ref, lo_ref, slice(0, 512), log_sig * (LOG2E / GLA_GATE_NORM))
        rows = [lbl_ref[2 * d + direction:2 * d + direction + 1, :] for d in range(DEPTH)]
        mx = functools.reduce(jnp.maximum, rows)
        ex = [jnp.exp(r - mx) for r in rows]
        lb = sum(ex[1:layer + 1], jnp.zeros_like(mx)) / sum(ex)
        c0 = COL_HF + direction * HG_K
        z = p1_ref[:, c0:c0 + HG_K]
        e = jnp.exp(-jnp.abs(z))
        s_big = 1.0 / (1.0 + e)
        s_small = e * s_big
        sig = jnp.where(z >= 0, s_big, s_small)
        nsig = jnp.where(z >= 0, s_small, s_big)
        f = lb + (1.0 - lb) * sig
        split(hi_ref, lo_ref, slice(512, 1024), jnp.log(jnp.maximum(f, F_MIN)) * LOG2E)
        k_ref[:, 512:1024] = ((1.0 - lb) * nsig).astype(BF16)

    q_ref[:, 0:512] = (p2_ref[:, 0:512] * (GLA_DK ** -0.5)).astype(BF16)
    gk = p2_ref[:, 512:1024].astype(BF16)
    kf_ref[:, 0:512] = gk
    kb_ref[:, 0:512] = gk
    v_ref[...] = p3_ref[:, 0:1024].astype(BF16)
    gate_ref[...] = p3_ref[:, 1024:2048].astype(BF16)


def _pad_heads(w, n_heads, width):
    lead = w.shape[:-1]
    w = w.reshape(lead + (n_heads, width))
    w = jnp.pad(w, [(0, 0)] * len(lead) + [(0, 0), (0, HEAD_W - width)])
    return w.reshape(lead + (n_heads * HEAD_W,))


def _proj_weights(w_in):
    gq, gk, gv, gr, glr_f, glr_b, hq, hf_f, hf_b, hv, hr = jnp.split(
        w_in, [256, 512, 1024, 1536, 1552, 1568, 2080, 2592, 3104, 3616], axis=-1)
    pad_r = lambda w: jnp.pad(w, ((0, 0), (0, 0), (0, LANES - GLA_RANK)))
    return jnp.concatenate([hq, hf_f, hf_b, pad_r(glr_f), pad_r(glr_b), _pad_heads(gq, 4, GLA_DK),
                            _pad_heads(gk, 4, GLA_DK), gv, hv, gr, hr], -1).astype(BF16)


def _proj(layer, h, norm_w, w_all, wg2, bg, lb_logits):
    t = h.shape[0]
    wg2_p = jnp.pad(_pad_heads(wg2, 4, GLA_DK), ((0, 0), (0, LANES - GLA_RANK), (0, 0)))
    bg_p = _pad_heads(bg, 4, GLA_DK).reshape(2, 1, 512)
    lbl = lb_logits.reshape(DEPTH * 2, HG_K)
    row = lambda: pl.BlockSpec((ROW_TILE, D_MODEL), lambda i: (i, 0))
    return pl.pallas_call(
        functools.partial(_proj_kernel, layer),
        grid=(t // ROW_TILE,),
        in_specs=[row(), _const_spec((1, D_MODEL)), _layer_spec((D_MODEL, NP_COLS), layer),
                  _const_spec((2, LANES, 512)), _const_spec((2, 1, 512)),
                  _const_spec((DEPTH * 2, HG_K))],
        out_specs=[row() for _ in range(9)],
        out_shape=[jax.ShapeDtypeStruct((t, D_MODEL), BF16) for _ in range(9)],
        scratch_shapes=[pltpu.VMEM((ROW_TILE, D_MODEL), BF16)] + [
            pltpu.VMEM((ROW_TILE, c1 - c0), F32) for c0, c1 in COL_GROUPS],
        compiler_params=_params(1),
        name=f"proj_l{layer}",
    )(h, norm_w.reshape(1, D_MODEL), w_all, wg2_p, bg_p, lbl)


A_GROUPS = 3
A_WIDTH = A_GROUPS * HEAD_W


def _stage_a(rev, q_ref, k_ref, gh_ref, gl_ref, r0, tri, bufs, slot):
    qa, ka, qd, kd, qi, ks, et, _, _ = bufs
    rows = slice(r0, r0 + CHUNK)
    cum = _dot(tri, jnp.concatenate([gh_ref[0, rows, :], gl_ref[0, rows, :]], axis=0))
    q = q_ref[0, rows, :]
    k = k_ref[0, rows, :]
    factor = lambda x: jnp.exp2(x).astype(BF16)

    def put(ref, rows, group, val):
        for n in range(N_HEADS):
            c0 = n * A_WIDTH + group * HEAD_W
            ref[slot, rows, c0:c0 + HEAD_W] = val[:, n * HEAD_W:(n + 1) * HEAD_W]

    for group, (half, a) in enumerate(((2 * SUB, 0), (SUB, 0), (SUB, 2 * SUB))):
        lo, hi = slice(a, a + half), slice(a + half, a + 2 * half)
        if not rev:
            ref, k_rows, q_rows = cum[a + half - 1:a + half], lo, hi
        else:
            ref, q_rows, k_rows = cum[a + half:a + half + 1], lo, hi
        put(qa, q_rows, group, q[q_rows] * factor(cum[q_rows] - ref))
        put(ka, k_rows, group, k[k_rows] * factor(ref - cum[k_rows]))

    mids = []
    for a in range(0, CHUNK, SUB):
        m = 0.5 * (cum[a:a + 1] + cum[a + SUB - 1:a + SUB])
        mids.append(jnp.broadcast_to(m, (SUB, D_MODEL)))
    dd = cum - jnp.concatenate(mids, 0)
    qd[slot] = q * factor(jnp.clip(dd, -EXP2_CLAMP, EXP2_CLAMP))
    kd[slot] = k * factor(jnp.clip(-dd, -EXP2_CLAMP, EXP2_CLAMP))

    tot = cum[0:1] if rev else cum[CHUNK - 1:CHUNK]
    qi[slot] = q * factor(cum)
    ks[slot] = k * factor(tot - cum)
    et[slot] = jnp.exp2(tot)


def _stage_b(bufs, slot, v_ref, st_ref, r0, diag):
    qa, ka, qd, kd, qi, ks, et, ab, oi = bufs
    for n in range(N_HEADS):
        wide = slice(n * A_WIDTH, (n + 1) * A_WIDTH)
        head = slice(n * HEAD_W, (n + 1) * HEAD_W)
        scores = _dot_nt(qa[slot, :, wide], ka[slot, :, wide])
        scores = scores + jnp.where(diag, _dot_nt(qd[slot, :, head], kd[slot, :, head]), 0.0)
        ab[slot, n] = scores.astype(BF16)
        st = st_ref[n]
        oi[slot, :, head] = _dot_nt(qi[slot, :, head], st.astype(BF16))
        st_ref[n] = st * et[slot, :, head] + _dot_tn(v_ref[0, r0:r0 + CHUNK, head],
                                                    ks[slot, :, head])


def _stage_c(bufs, slot, v_ref, o_ref, r0):
    ab, oi = bufs[7], bufs[8]
    for n in range(N_HEADS):
        head = slice(n * HEAD_W, (n + 1) * HEAD_W)
        o = oi[slot, :, head] + _dot(ab[slot, n], v_ref[0, r0:r0 + CHUNK, head])
        o_ref[0, r0:r0 + CHUNK, head] = o.astype(o_ref.dtype)


def _scan_kernel(qf_ref, qb_ref, kf_ref, kb_ref, vf_ref, vb_ref, gfh_ref, gfl_ref, gbh_ref,
                 gbl_ref, of_ref, ob_ref, sf_ref, sb_ref, *buf_refs):
    n_chunks = MIX_TILE // CHUNK
    bufs_f, bufs_b = buf_refs[:len(buf_refs) // 2], buf_refs[len(buf_refs) // 2:]

    @pl.when(pl.program_id(1) == 0)
    def _():
        sf_ref[...] = jnp.zeros_like(sf_ref)
        sb_ref[...] = jnp.zeros_like(sb_ref)
        for bufs in (bufs_f, bufs_b):
            bufs[0][...] = jnp.zeros_like(bufs[0])
            bufs[1][...] = jnp.zeros_like(bufs[1])

    ri = lax.broadcasted_iota(jnp.int32, (CHUNK, CHUNK), 0)
    ci = lax.broadcasted_iota(jnp.int32, (CHUNK, CHUNK), 1)
    tri_f = (ci <= ri).astype(BF16)
    tri_b = (ci >= ri).astype(BF16)
    tri_f = jnp.concatenate([tri_f, tri_f], axis=1)
    tri_b = jnp.concatenate([tri_b, tri_b], axis=1)
    same16 = (ri // SUB) == (ci // SUB)
    diag_f = same16 & (ci <= ri)
    diag_b = same16 & (ci >= ri)
    row_f = lambda c: c * CHUNK
    row_b = lambda c: (n_chunks - 1 - c) * CHUNK

    def stage_a(c):
        _stage_a(False, qf_ref, kf_ref, gfh_ref, gfl_ref, row_f(c), tri_f, bufs_f, c % 2)
        _stage_a(True, qb_ref, kb_ref, gbh_ref, gbl_ref, row_b(c), tri_b, bufs_b, c % 2)

    stage_a(0)
    for c in range(n_chunks):
        if c + 1 < n_chunks:
            stage_a(c + 1)
        _stage_b(bufs_f, c % 2, vf_ref, sf_ref, row_f(c), diag_f)
        _stage_b(bufs_b, c % 2, vb_ref, sb_ref, row_b(c), diag_b)
        _stage_c(bufs_f, c % 2, vf_ref, of_ref, row_f(c))
        _stage_c(bufs_b, c % 2, vb_ref, ob_ref, row_b(c))


def _scan_bufs():
    wide = pltpu.VMEM((2, CHUNK, N_HEADS * A_WIDTH), BF16)
    narrow = pltpu.VMEM((2, CHUNK, D_MODEL), BF16)
    return [wide, wide, narrow, narrow, narrow, narrow,
            pltpu.VMEM((2, 1, D_MODEL), F32),
            pltpu.VMEM((2, N_HEADS, CHUNK, CHUNK), BF16),
            pltpu.VMEM((2, CHUNK, D_MODEL), F32)]


def _scan(layer, bsz, q, kf, kb, v, gfh, gfl, gbh, gbl):
    seq = q.shape[0] // bsz
    nt = seq // MIX_TILE
    to3 = lambda a: a.reshape(bsz, seq, D_MODEL)
    tile = lambda idx: pl.BlockSpec((1, MIX_TILE, D_MODEL), idx)
    fwd = lambda b, j: (b, j, 0)
    bwd = lambda b, j: (b, nt - 1 - j, 0)
    state = pltpu.VMEM((N_HEADS, HEAD_W, HEAD_W), F32)
    return pl.pallas_call(
        _scan_kernel,
        grid=(bsz, nt),
        in_specs=[tile(fwd), tile(bwd), tile(fwd), tile(bwd), tile(fwd), tile(bwd),
                  tile(fwd), tile(fwd), tile(bwd), tile(bwd)],
        out_specs=[tile(fwd), tile(bwd)],
        out_shape=[jax.ShapeDtypeStruct((bsz, seq, D_MODEL), BF16),
                   jax.ShapeDtypeStruct((bsz, seq, D_MODEL), BF16)],
        scratch_shapes=[state, state] + _scan_bufs() + _scan_bufs(),
        compiler_params=_params(2),
        name=f"scan_l{layer}",
    )(to3(q), to3(q), to3(kf), to3(kb), to3(v), to3(v), to3(gfh), to3(gfl), to3(gbh), to3(gbl))


def _post_math(of_ref, ob_ref, gate_ref, ones_ref, onw_ref, wout_ref, h):
    o = of_ref[...].astype(F32) + ob_ref[...].astype(F32)
    sq = (o * o).astype(BF16)
    ms = jnp.concatenate([_dot(sq[:, n * HEAD_W:(n + 1) * HEAD_W], ones_ref[...])
                          for n in range(N_HEADS)], axis=1) * (1.0 / HEAD_W)
    y = o * lax.rsqrt(ms + HEAD_EPS) * onw_ref[...]
    g = gate_ref[...].astype(F32)
    y = y * (g * _sigmoid(g))
    return h + _dot(y.astype(BF16), wout_ref[0])


def _post_inputs(layer, o_f, o_b, gates, gla_onorm, hg_onorm, w_out_all):
    onw = jnp.concatenate([jnp.tile(gla_onorm, 4), jnp.tile(hg_onorm, 4)]).reshape(1, D_MODEL)
    row = lambda: pl.BlockSpec((ROW_TILE, D_MODEL), lambda i: (i, 0))
    specs = [row(), row(), row(), _const_spec((HEAD_W, HEAD_W)), _const_spec((1, D_MODEL)),
             _layer_spec((D_MODEL, D_MODEL), layer)]
    return (o_f, o_b, gates, jnp.ones((HEAD_W, HEAD_W), BF16), onw, w_out_all), specs


def _ple_tail(h2, p, nple, wpp_ref, wpg_ref, nfinal, final):
    gate = _sigmoid(_dot(_rmsnorm(h2, nple).astype(BF16), wpg_ref[0]))
    h3 = h2 + _dot(p.astype(BF16), wpp_ref[0]) * gate
    if final:
        h3 = _rmsnorm(h3, nfinal)
    return h3


def _dense_kernel(final, of_ref, ob_ref, gate_ref, ones_ref, onw_ref, wout_ref, h_ref, p_ref,
                  nffn_ref, wg_ref, wu_ref, wd_ref, nple_ref, wpp_ref, wpg_ref, nfin_ref, out_ref):
    h1 = _post_math(of_ref, ob_ref, gate_ref, ones_ref, onw_ref, wout_ref, h_ref[...])
    v = _rmsnorm(h1, nffn_ref[...]).astype(BF16)
    acc = jnp.zeros_like(h1)
    for c in range(0, D_FF_DENSE, FF_CHUNK):
        a = _dot(v, wg_ref[0, :, c:c + FF_CHUNK])
        b = _dot(v, wu_ref[0, :, c:c + FF_CHUNK])
        acc = acc + _dot((a * _sigmoid(a) * b).astype(BF16), wd_ref[0, c:c + FF_CHUNK, :])
    out_ref[...] = _ple_tail(h1 + acc, p_ref[0], nple_ref[...], wpp_ref, wpg_ref,
                             nfin_ref[...], final)


def _dense_ffn(layer, final, post, h, p, norm_ffn, wg, wu, wd, norm_ple, w_pp, w_pg, norm_final):
    t = h.shape[0]
    vec = lambda w: w.reshape(1, D_MODEL)
    j = layer // 2
    return pl.pallas_call(
        functools.partial(_dense_kernel, final),
        grid=(t // ROW_TILE,),
        in_specs=post[1] + [
            pl.BlockSpec((ROW_TILE, D_MODEL), lambda i: (i, 0)),
            pl.BlockSpec((1, ROW_TILE, PLE_DIM), lambda i: (layer, i, 0)),
            _const_spec((1, D_MODEL)),
            _layer_spec((D_MODEL, D_FF_DENSE), j), _layer_spec((D_MODEL, D_FF_DENSE), j),
            _layer_spec((D_FF_DENSE, D_MODEL), j),
            _const_spec((1, D_MODEL)),
            _layer_spec((PLE_DIM, D_MODEL), layer), _layer_spec((D_MODEL, D_MODEL), layer),
            _const_spec((1, D_MODEL))],
        out_specs=pl.BlockSpec((ROW_TILE, D_MODEL), lambda i: (i, 0)),
        out_shape=jax.ShapeDtypeStruct((t, D_MODEL), F32),
        compiler_params=_params(1),
        name=f"dense_l{layer}",
    )(*post[0], h, p, vec(norm_ffn), wg, wu, wd, vec(norm_ple), w_pp, w_pg, vec(norm_final))


META_G1, META_G2, META_E1, META_E2, META_R1, META_R2 = range(6)
HALF = D_MODEL // 2
HI_MASK = 0xFFFF0000


def _pack_rows(x):
    bits = pltpu.bitcast(x.astype(BF16).astype(F32), jnp.uint32)
    return (bits[:, :HALF] >> 16) | (bits[:, HALF:] & jnp.uint32(HI_MASK))


def _unpack_rows(w):
    lo = pltpu.bitcast(w << 16, F32)
    hi = pltpu.bitcast(w & jnp.uint32(HI_MASK), F32)
    return jnp.concatenate([lo, hi], axis=1)


def _router_kernel(of_ref, ob_ref, gate_ref, ones_ref, onw_ref, wout_ref, h_ref, nffn_ref, wr_ref,
                   tri_ref, h1_ref, xp_ref, meta_ref, cnt_ref, base_ref):
    @pl.when(pl.program_id(0) == 0)
    def _():
        base_ref[...] = jnp.zeros_like(base_ref)

    h1 = _post_math(of_ref, ob_ref, gate_ref, ones_ref, onw_ref, wout_ref, h_ref[...])
    h1_ref[...] = h1
    vf = _rmsnorm(h1, nffn_ref[...])
    xp_ref[...] = _pack_rows(vf)
    v_hi = vf.astype(BF16)
    v_lo = (vf - v_hi.astype(F32)).astype(BF16)
    w = wr_ref[...]
    w_hi = w.astype(BF16)
    w_lo = (w - w_hi.astype(F32)).astype(BF16)
    logits = _dot(v_hi, w_hi) + _dot(v_lo, w_hi) + _dot(v_hi, w_lo)
    lane = lax.broadcasted_iota(jnp.int32, logits.shape, 1)
    neg = jnp.float32(-jnp.inf)
    lg = jnp.where(lane < N_EXPERTS, logits, neg)
    m1 = jnp.max(lg, axis=-1, keepdims=True)
    i1 = jnp.min(jnp.where(lg == m1, lane, LANES), axis=-1, keepdims=True)
    first = lane == i1
    lg2 = jnp.where(first, neg, lg)
    m2 = jnp.max(lg2, axis=-1, keepdims=True)
    i2 = jnp.min(jnp.where(lg2 == m2, lane, LANES), axis=-1, keepdims=True)
    second = lane == i2
    e = jnp.exp(m2 - m1)
    g1 = 1.0 / (1.0 + e)
    g2 = e * g1
    cnt = jnp.where(first | second, 1.0, 0.0)
    rank = base_ref[...] + _dot(tri_ref[...], cnt.astype(BF16))
    r1 = jnp.sum(jnp.where(first, rank, 0.0), axis=-1, keepdims=True)
    r2 = jnp.sum(jnp.where(second, rank, 0.0), axis=-1, keepdims=True)
    base_ref[...] = base_ref[...] + jnp.sum(cnt, axis=0, keepdims=True)
    cnt_ref[...] = base_ref[...]
    meta = jnp.zeros(logits.shape, F32)
    for col, val in ((META_G1, g1), (META_G2, g2), (META_E1, i1.astype(F32)),
                     (META_E2, i2.astype(F32)), (META_R1, r1), (META_R2, r2)):
        meta = jnp.where(lane == col, val, meta)
    meta_ref[...] = meta


def _router(layer, post, h, norm_ffn, w_router):
    t = h.shape[0]
    wr = jnp.pad(w_router, ((0, 0), (0, LANES - N_EXPERTS)))
    ids = jnp.arange(ROW_TILE)
    tri = (ids[None, :] < ids[:, None]).astype(BF16)
    return pl.pallas_call(
        _router_kernel,
        grid=(t // ROW_TILE,),
        in_specs=post[1] + [
            pl.BlockSpec((ROW_TILE, D_MODEL), lambda i: (i, 0)),
            _const_spec((1, D_MODEL)), _const_spec((D_MODEL, LANES)),
            _const_spec((ROW_TILE, ROW_TILE))],
        out_specs=[pl.BlockSpec((ROW_TILE, D_MODEL), lambda i: (i, 0)),
                   pl.BlockSpec((ROW_TILE, HALF), lambda i: (i, 0)),
                   pl.BlockSpec((ROW_TILE, LANES), lambda i: (i, 0)),
                   pl.BlockSpec((1, LANES), lambda i: (0, 0))],
        out_shape=[jax.ShapeDtypeStruct((t, D_MODEL), F32),
                   jax.ShapeDtypeStruct((t, HALF), jnp.uint32),
                   jax.ShapeDtypeStruct((t, LANES), F32),
                   jax.ShapeDtypeStruct((1, LANES), F32)],
        scratch_shapes=[pltpu.VMEM((1, LANES), F32)],
        compiler_params=_params(1),
        name=f"router_l{layer}",
    )(*post[0], h, norm_ffn.reshape(1, D_MODEL), wr, tri)


SC_CORES, SC_SUBCORES = 2, 16
SC_WORKERS = SC_CORES * SC_SUBCORES
SC_WINDOW = 64


def _sc_gather(table, idx):
    n_out, d = idx.shape[0], table.shape[1]
    per_worker = n_out // SC_WORKERS
    n_win = per_worker // SC_WINDOW
    assert n_win * SC_WINDOW * SC_WORKERS == n_out and n_win % 2 == 0
    mesh = plsc.VectorSubcoreMesh(core_axis_name="c", subcore_axis_name="s")

    def body(table_hbm, idx_hbm, out_hbm, idx_a, idx_b, rows_a, rows_b, sem_a, sem_b):
        wid = lax.axis_index("s") * SC_CORES + lax.axis_index("c")

        def rows_of(w):
            return pl.ds(pl.multiple_of(wid * per_worker + w * SC_WINDOW, SC_WINDOW), SC_WINDOW)

        def start(w, idx_v, rows_v, sem):
            pltpu.sync_copy(idx_hbm.at[rows_of(w)], idx_v)
            pltpu.async_copy(table_hbm.at[idx_v], rows_v, sem)

        def finish(w, idx_v, rows_v, sem):
            pltpu.make_async_copy(table_hbm.at[idx_v], rows_v, sem).wait()
            pltpu.sync_copy(rows_v, out_hbm.at[rows_of(w)])

        start(0, idx_a, rows_a, sem_a)

        @pl.loop(0, n_win, step=2)
        def _(w):
            start(w + 1, idx_b, rows_b, sem_b)
            finish(w, idx_a, rows_a, sem_a)

            @pl.when(w + 2 < n_win)
            def _():
                start(w + 2, idx_a, rows_a, sem_a)

            finish(w + 1, idx_b, rows_b, sem_b)

    return pl.kernel(
        body, mesh=mesh,
        out_type=jax.ShapeDtypeStruct((n_out, d), table.dtype),
        scratch_types=[pltpu.VMEM((SC_WINDOW,), jnp.int32), pltpu.VMEM((SC_WINDOW,), jnp.int32),
                       pltpu.VMEM((SC_WINDOW, d), table.dtype),
                       pltpu.VMEM((SC_WINDOW, d), table.dtype),
                       pltpu.SemaphoreType.DMA, pltpu.SemaphoreType.DMA],
    )(table, idx)


def _sc_scatter2(x, pos_a, pos_b, n_out):
    t, d = x.shape
    per_worker = t // SC_WORKERS
    n_win = per_worker // SC_WINDOW
    assert n_win * SC_WINDOW * SC_WORKERS == t and n_win % 2 == 0
    mesh = plsc.VectorSubcoreMesh(core_axis_name="c", subcore_axis_name="s")

    def body(x_hbm, pa_hbm, pb_hbm, out_hbm, ia0, ib0, ia1, ib1, rows0, rows1, sem0, sem1):
        wid = lax.axis_index("s") * SC_CORES + lax.axis_index("c")

        def rows_of(w):
            return pl.ds(pl.multiple_of(wid * per_worker + w * SC_WINDOW, SC_WINDOW), SC_WINDOW)

        def start(w, ia, ib, rows_v, sem):
            pltpu.sync_copy(x_hbm.at[rows_of(w)], rows_v)
            pltpu.sync_copy(pa_hbm.at[rows_of(w)], ia)
            pltpu.sync_copy(pb_hbm.at[rows_of(w)], ib)
            pltpu.async_copy(rows_v, out_hbm.at[ia], sem)
            pltpu.async_copy(rows_v, out_hbm.at[ib], sem)

        def finish(ia, ib, rows_v, sem):
            pltpu.make_async_copy(rows_v, out_hbm.at[ia], sem).wait()
            pltpu.make_async_copy(rows_v, out_hbm.at[ib], sem).wait()

        start(0, ia0, ib0, rows0, sem0)

        @pl.loop(0, n_win, step=2)
        def _(w):
            start(w + 1, ia1, ib1, rows1, sem1)
            finish(ia0, ib0, rows0, sem0)

            @pl.when(w + 2 < n_win)
            def _():
                start(w + 2, ia0, ib0, rows0, sem0)

            finish(ia1, ib1, rows1, sem1)

    idx = lambda: pltpu.VMEM((SC_WINDOW,), jnp.int32)
    return pl.kernel(
        body, mesh=mesh,
        out_type=jax.ShapeDtypeStruct((n_out, d), x.dtype),
        scratch_types=[idx(), idx(), idx(), idx(),
                       pltpu.VMEM((SC_WINDOW, d), x.dtype), pltpu.VMEM((SC_WINDOW, d), x.dtype),
                       pltpu.SemaphoreType.DMA, pltpu.SemaphoreType.DMA],
    )(x, pos_a, pos_b)


EXP_ROWS = 1024


def _experts_kernel(be_ref, na_ref, nv_ref, xs_ref, wg_ref, wu_ref, wd_ref, ys_ref, x_ref,
                    acc_ref):
    b = pl.program_id(0)
    f = pl.program_id(1)

    @pl.when(b < na_ref[0])
    def _():
        @pl.when(f == 0)
        def _():
            row = lax.broadcasted_iota(jnp.int32, (EXP_ROWS, D_MODEL), 0)
            x = jnp.where(row < nv_ref[b], _unpack_rows(xs_ref[...]), 0.0)
            x_ref[...] = x.astype(BF16)
            acc_ref[...] = jnp.zeros_like(acc_ref)

        x = x_ref[...]
        a = _dot(x, wg_ref[0, 0].astype(BF16))
        u = _dot(x, wu_ref[0, 0].astype(BF16))
        acc_ref[...] += _dot((a * _sigmoid(a) * u).astype(BF16), wd_ref[0, 0].astype(BF16))

        @pl.when(f == pl.num_programs(1) - 1)
        def _():
            ys_ref[...] = _pack_rows(acc_ref[...])


def _experts(layer, xs, blk_expert, n_active, blk_valid, wg, wu, wd):
    moe_idx = layer // 2
    rows = xs.shape[0]
    nf = D_FF_EXPERT // EXP_FF_CHUNK
    fsel = lambda b, f, na: jnp.where(b < na[0], f, nf - 1)
    grid_spec = pltpu.PrefetchScalarGridSpec(
        num_scalar_prefetch=3,
        grid=(rows // EXP_ROWS, nf),
        in_specs=[pl.BlockSpec((EXP_ROWS, HALF), lambda b, f, be, na, nv: (b, 0)),
                  pl.BlockSpec((1, 1, D_MODEL, EXP_FF_CHUNK),
                               lambda b, f, be, na, nv: (moe_idx, be[b], 0, fsel(b, f, na))),
                  pl.BlockSpec((1, 1, D_MODEL, EXP_FF_CHUNK),
                               lambda b, f, be, na, nv: (moe_idx, be[b], 0, fsel(b, f, na))),
                  pl.BlockSpec((1, 1, EXP_FF_CHUNK, D_MODEL),
                               lambda b, f, be, na, nv: (moe_idx, be[b], fsel(b, f, na), 0))],
        out_specs=pl.BlockSpec((EXP_ROWS, HALF), lambda b, f, be, na, nv: (b, 0)),
        scratch_shapes=[pltpu.VMEM((EXP_ROWS, D_MODEL), BF16),
                        pltpu.VMEM((EXP_ROWS, D_MODEL), F32)])
    return pl.pallas_call(
        _experts_kernel,
        grid_spec=grid_spec,
        out_shape=jax.ShapeDtypeStruct((rows, HALF), jnp.uint32),
        compiler_params=_params(2),
        name=f"experts_l{layer}",
    )(blk_expert, n_active, blk_valid, xs, wg, wu, wd)


def _combine_kernel(final, h_ref, y1_ref, y2_ref, meta_ref, p_ref, nple_ref, wpp_ref, wpg_ref,
                    nfin_ref, out_ref):
    meta = meta_ref[...]
    g1 = meta[:, META_G1:META_G1 + 1]
    g2 = meta[:, META_G2:META_G2 + 1]
    h2 = h_ref[...] + g1 * _unpack_rows(y1_ref[...]) + g2 * _unpack_rows(y2_ref[...])
    out_ref[...] = _ple_tail(h2, p_ref[0], nple_ref[...], wpp_ref, wpg_ref, nfin_ref[...], final)


def _combine(layer, final, h, yg, meta, p, norm_ple, w_pp, w_pg, norm_final):
    t = h.shape[0]
    nt = t // ROW_TILE
    vec = lambda w: w.reshape(1, D_MODEL)
    return pl.pallas_call(
        functools.partial(_combine_kernel, final),
        grid=(nt,),
        in_specs=[pl.BlockSpec((ROW_TILE, D_MODEL), lambda i: (i, 0)),
                  pl.BlockSpec((ROW_TILE, HALF), lambda i: (i, 0)),
                  pl.BlockSpec((ROW_TILE, HALF), lambda i: (i + nt, 0)),
                  pl.BlockSpec((ROW_TILE, LANES), lambda i: (i, 0)),
                  pl.BlockSpec((1, ROW_TILE, PLE_DIM), lambda i: (layer, i, 0)),
                  _const_spec((1, D_MODEL)),
                  _layer_spec((PLE_DIM, D_MODEL), layer), _layer_spec((D_MODEL, D_MODEL), layer),
                  _const_spec((1, D_MODEL))],
        out_specs=pl.BlockSpec((ROW_TILE, D_MODEL), lambda i: (i, 0)),
        out_shape=jax.ShapeDtypeStruct((t, D_MODEL), F32),
        compiler_params=_params(1),
        name=f"combine_l{layer}",
    )(h, yg, yg, meta, p, vec(norm_ple), w_pp, w_pg, vec(norm_final))


def _moe(layer, final, post, h, p, norm_ffn, w_router, wg, wu, wd, norm_ple, w_pp, w_pg,
         norm_final):
    t = h.shape[0]
    h1, xp, meta, counts = _router(layer, post, h, norm_ffn, w_router)
    cnt = counts[0, :N_EXPERTS].astype(jnp.int32)
    padded = ((cnt + EXP_ROWS - 1) // EXP_ROWS) * EXP_ROWS
    ends = jnp.cumsum(padded)
    offs = ends - padded
    e1 = meta[:, META_E1].astype(jnp.int32)
    e2 = meta[:, META_E2].astype(jnp.int32)
    pos1 = offs[e1] + meta[:, META_R1].astype(jnp.int32)
    pos2 = offs[e2] + meta[:, META_R2].astype(jnp.int32)
    rows = 2 * t + N_EXPERTS * EXP_ROWS
    blk_start = jnp.arange(rows // EXP_ROWS, dtype=jnp.int32) * EXP_ROWS
    blk_expert = jnp.minimum(
        jnp.sum(blk_start[:, None] >= ends[None, :], axis=1), N_EXPERTS - 1).astype(jnp.int32)
    n_active = (ends[-1:] // EXP_ROWS).astype(jnp.int32)
    blk_valid = jnp.clip((offs + cnt)[blk_expert] - blk_start, 0, EXP_ROWS).astype(jnp.int32)

    xs = _sc_scatter2(xp, pos1, pos2, rows)
    ys = _experts(layer, xs, blk_expert, n_active, blk_valid, wg, wu, wd)
    yg = _sc_gather(ys, jnp.concatenate([pos1, pos2]))
    return _combine(layer, final, h1, yg, meta, p, norm_ple, w_pp, w_pg, norm_final)


def kernel(x, p, w_in, gla_wg2, gla_bg, hg_lb_logits, gla_onorm, hg_onorm, w_out, norm_mix,
           norm_ffn, w_dense_gate, w_dense_up, w_dense_down, w_router, w_exp_gate, w_exp_up,
           w_exp_down, w_ple_proj, w_ple_gate, norm_ple, norm_final):
    bsz, seq, _ = x.shape
    t = bsz * seq
    w_proj = _proj_weights(w_in)
    w_out_b, w_pp, w_pg = w_out.astype(BF16), w_ple_proj.astype(BF16), w_ple_gate.astype(BF16)
    w_dg, w_du, w_dd = (w.astype(BF16) for w in (w_dense_gate, w_dense_up, w_dense_down))
    p_all = p.reshape(DEPTH, t, PLE_DIM)
    h = x.reshape(t, D_MODEL)
    for i in range(DEPTH):
        final = i == DEPTH - 1
        q, kf, kb, v, gates, gfh, gfl, gbh, gbl = _proj(
            i, h, norm_mix[i], w_proj, gla_wg2[i], gla_bg[i], hg_lb_logits)
        o_f, o_b = _scan(i, bsz, q, kf, kb, v, gfh, gfl, gbh, gbl)
        post = _post_inputs(i, o_f.reshape(t, D_MODEL), o_b.reshape(t, D_MODEL), gates,
                            gla_onorm[i], hg_onorm[i], w_out_b)
        if i % 2 == 0:
            h = _dense_ffn(i, final, post, h, p_all, norm_ffn[i], w_dg, w_du, w_dd, norm_ple[i],
                           w_pp, w_pg, norm_final)
        else:
            h = _moe(i, final, post, h, p_all, norm_ffn[i], w_router[i // 2], w_exp_gate, w_exp_up,
                     w_exp_down, norm_ple[i], w_pp, w_pg, norm_final)
    return h.reshape(bsz, seq, D_MODEL)
```

```python
import functools

import jax
import jax.numpy as jnp
from jax import lax
from jax.experimental import pallas as pl
from jax.experimental.pallas import tpu as pltpu
from jax.experimental.pallas import tpu_sc as plsc

F32 = jnp.float32
BF16 = jnp.bfloat16

D_MODEL = 1024
DEPTH = 4
N_HEADS = 8
HEAD_W = 128
GLA_DK = 64
GLA_RANK = 16
GLA_GATE_NORM = 16.0
HG_DK = 128
HG_K = 512
D_FF_DENSE = 2816
N_EXPERTS = 8
D_FF_EXPERT = 3584
PLE_DIM = 256
EPS = 1e-6
HEAD_EPS = 1e-5
F_MIN = 1e-6

CHUNK = 64
SUB = 16
EXP2_CLAMP = 86.0
LANES = 128
VMEM_LIMIT = 56 * 1024 * 1024

MIX_TILE = 256
ROW_TILE = 512
FF_CHUNK = 256
EXP_FF_CHUNK = 512

COL_HQ = 0
COL_HF = 512
COL_GLR = 1536
COL_GQ = 1664
COL_GK = 1920
COL_V = 2176
COL_GATE = 3200
NP_COLS = 4224
GLA_QK = 4 * GLA_DK
LOG2E = 1.4426950408889634


def _dot(a, b):
    return jnp.dot(a, b, preferred_element_type=F32)


def _dot_nt(a, b):
    return lax.dot_general(a, b, (((1,), (1,)), ((), ())), preferred_element_type=F32)


def _dot_tn(a, b):
    return lax.dot_general(a, b, (((0,), (0,)), ((), ())), preferred_element_type=F32)


def _sigmoid(x):
    return 1.0 / (1.0 + jnp.exp(-x))


def _rmsnorm(x, w):
    ms = jnp.mean(x * x, axis=-1, keepdims=True)
    return x * lax.rsqrt(ms + EPS) * w


def _const_spec(shape):
    nd = len(shape)
    return pl.BlockSpec(shape, lambda *_: (0,) * nd, pipeline_mode=pl.Buffered(1))


def _layer_spec(shape, layer):
    nd = len(shape)
    return pl.BlockSpec((1,) + tuple(shape), lambda *_: (layer,) + (0,) * nd,
                        pipeline_mode=pl.Buffered(1))


def _params(n_grid):
    return pltpu.CompilerParams(
        dimension_semantics=("arbitrary",) * n_grid, vmem_limit_bytes=VMEM_LIMIT)


PROJ_CHUNK = 256


def _proj_kernel(layer, h_ref, nw_ref, w_ref, wg2_ref, bg_ref, lbl_ref, spread_ref,
                 q_ref, kf_ref, kb_ref, v_ref, gate_ref, gfh_ref, gfl_ref, gbh_ref, gbl_ref,
                 u_ref):
    u_ref[...] = _rmsnorm(h_ref[...], nw_ref[...]).astype(BF16)
    proj = lambda c0, width: _dot(u_ref[...], w_ref[0, :, c0:c0 + width])
    chunks = lambda width: range(0, width, PROJ_CHUNK)

    def split(hi_ref, lo_ref, c0, g):
        hi = g.astype(BF16)
        hi_ref[:, c0:c0 + PROJ_CHUNK] = hi
        lo_ref[:, c0:c0 + PROJ_CHUNK] = (g - hi.astype(F32)).astype(BF16)

    dirs = ((kf_ref, gfh_ref, gfl_ref), (kb_ref, gbh_ref, gbl_ref))
    heavy, light = [], []

    glr = proj(COL_GLR, LANES).astype(BF16)

    def gla_decay(direction, c):
        _, hi_ref, lo_ref = dirs[direction]
        x = _dot(glr, wg2_ref[direction, :, c:c + PROJ_CHUNK].astype(BF16))
        x = x + bg_ref[direction, :, c:c + PROJ_CHUNK]
        log_sig = jnp.minimum(x, 0.0) - jnp.log(1.0 + jnp.exp(-jnp.abs(x)))
        split(hi_ref, lo_ref, c, log_sig * (LOG2E / GLA_GATE_NORM))

    def hgrn_forget(direction, c):
        k_ref, hi_ref, lo_ref = dirs[direction]
        rows = [lbl_ref[2 * d + direction:2 * d + direction + 1, c:c + PROJ_CHUNK]
                for d in range(DEPTH)]
        mx = functools.reduce(jnp.maximum, rows)
        ex = [jnp.exp(r - mx) for r in rows]
        lb = sum(ex[1:layer + 1], jnp.zeros_like(mx)) / sum(ex)
        z = proj(COL_HF + direction * HG_K + c, PROJ_CHUNK)
        e = jnp.exp(-jnp.abs(z))
        s_big = 1.0 / (1.0 + e)
        s_small = e * s_big
        sig = jnp.where(z >= 0, s_big, s_small)
        nsig = jnp.where(z >= 0, s_small, s_big)
        f = lb + (1.0 - lb) * sig
        split(hi_ref, lo_ref, 512 + c, jnp.log(jnp.maximum(f, F_MIN)) * LOG2E)
        k_ref[:, 512 + c:512 + c + PROJ_CHUNK] = ((1.0 - lb) * nsig).astype(BF16)

    def hgrn_query(c):
        hq = proj(COL_HQ + c, PROJ_CHUNK)
        q_ref[:, 512 + c:512 + c + PROJ_CHUNK] = (hq * _sigmoid(hq) * (HG_DK ** -0.5)).astype(BF16)

    def gla_query():
        gq = (proj(COL_GQ, GLA_QK) * (GLA_DK ** -0.5)).astype(BF16)
        q_ref[:, 0:512] = _dot(gq, spread_ref[...]).astype(BF16)

    def gla_key():
        gk = _dot(proj(COL_GK, GLA_QK).astype(BF16), spread_ref[...]).astype(BF16)
        kf_ref[:, 0:512] = gk
        kb_ref[:, 0:512] = gk

    def plain(out_ref, col, c):
        out_ref[:, c:c + PROJ_CHUNK] = proj(col + c, PROJ_CHUNK).astype(BF16)

    for direction in range(2):
        heavy += [functools.partial(hgrn_forget, direction, c) for c in chunks(HG_K)]
        heavy += [functools.partial(gla_decay, direction, c) for c in chunks(512)]
    heavy += [functools.partial(hgrn_query, c) for c in chunks(512)]
    light += [gla_query, gla_key]
    light += [functools.partial(plain, v_ref, COL_V, c) for c in chunks(D_MODEL)]
    light += [functools.partial(plain, gate_ref, COL_GATE, c) for c in chunks(D_MODEL)]
    for pair in zip(heavy, light):
        for chunk in pair:
            chunk()


def _pad_heads(w, n_heads, width):
    lead = w.shape[:-1]
    w = w.reshape(lead + (n_heads, width))
    w = jnp.pad(w, [(0, 0)] * len(lead) + [(0, 0), (0, HEAD_W - width)])
    return w.reshape(lead + (n_heads * HEAD_W,))


def _proj_weights(w_in):
    gq, gk, gv, gr, glr_f, glr_b, hq, hf_f, hf_b, hv, hr = jnp.split(
        w_in, [256, 512, 1024, 1536, 1552, 1568, 2080, 2592, 3104, 3616], axis=-1)
    glr = jnp.pad(jnp.concatenate([glr_f, glr_b], -1), ((0, 0), (0, 0), (0, LANES - 2 * GLA_RANK)))
    return jnp.concatenate([hq, hf_f, hf_b, glr, gq, gk, gv, hv, gr, hr], -1).astype(BF16)


def _proj(layer, h, norm_w, w_all, wg2, bg, lb_logits):
    t = h.shape[0]
    wg2_h = _pad_heads(wg2, 4, GLA_DK)
    wg2_p = jnp.stack([jnp.pad(wg2_h[0], ((0, LANES - GLA_RANK), (0, 0))),
                       jnp.pad(wg2_h[1], ((GLA_RANK, LANES - 2 * GLA_RANK), (0, 0)))])
    bg_p = _pad_heads(bg, 4, GLA_DK).reshape(2, 1, 512)
    lbl = lb_logits.reshape(DEPTH * 2, HG_K)
    src = jnp.arange(GLA_QK)
    spread = (jnp.arange(512)[None, :] == (src // GLA_DK * HEAD_W + src % GLA_DK)[:, None])
    row = lambda: pl.BlockSpec((ROW_TILE, D_MODEL), lambda i: (i, 0))
    return pl.pallas_call(
        functools.partial(_proj_kernel, layer),
        grid=(t // ROW_TILE,),
        in_specs=[row(), _const_spec((1, D_MODEL)), _layer_spec((D_MODEL, NP_COLS), layer),
                  _const_spec((2, LANES, 512)), _const_spec((2, 1, 512)),
                  _const_spec((DEPTH * 2, HG_K)), _const_spec((GLA_QK, 512))],
        out_specs=[row() for _ in range(9)],
        out_shape=[jax.ShapeDtypeStruct((t, D_MODEL), BF16) for _ in range(9)],
        scratch_shapes=[pltpu.VMEM((ROW_TILE, D_MODEL), BF16)],
        compiler_params=_params(1),
        name=f"proj_l{layer}",
    )(h, norm_w.reshape(1, D_MODEL), w_all, wg2_p, bg_p, lbl, spread.astype(BF16))


A_GROUPS = 3
A_WIDTH = A_GROUPS * HEAD_W


def _stage_a(rev, q_ref, k_ref, gh_ref, gl_ref, r0, tri, bufs, slot):
    qa, ka, qd, kd, qi, ks, et, _, _ = bufs
    rows = slice(r0, r0 + CHUNK)
    cum = _dot(tri, jnp.concatenate([gh_ref[0, rows, :], gl_ref[0, rows, :]], axis=0))
    q = q_ref[0, rows, :]
    k = k_ref[0, rows, :]
    factor = lambda x: jnp.exp2(x).astype(BF16)

    def put(ref, rows, group, val):
        for n in range(N_HEADS):
            c0 = n * A_WIDTH + group * HEAD_W
            ref[slot, rows, c0:c0 + HEAD_W] = val[:, n * HEAD_W:(n + 1) * HEAD_W]

    for group, (half, a) in enumerate(((2 * SUB, 0), (SUB, 0), (SUB, 2 * SUB))):
        lo, hi = slice(a, a + half), slice(a + half, a + 2 * half)
        if not rev:
            ref, k_rows, q_rows = cum[a + half - 1:a + half], lo, hi
        else:
            ref, q_rows, k_rows = cum[a + half:a + half + 1], lo, hi
        put(qa, q_rows, group, q[q_rows] * factor(cum[q_rows] - ref))
        put(ka, k_rows, group, k[k_rows] * factor(ref - cum[k_rows]))

    mids = []
    for a in range(0, CHUNK, SUB):
        m = 0.5 * (cum[a:a + 1] + cum[a + SUB - 1:a + SUB])
        mids.append(jnp.broadcast_to(m, (SUB, D_MODEL)))
    dd = cum - jnp.concatenate(mids, 0)
    qd[slot] = q * factor(jnp.clip(dd, -EXP2_CLAMP, EXP2_CLAMP))
    kd[slot] = k * factor(jnp.clip(-dd, -EXP2_CLAMP, EXP2_CLAMP))

    tot = cum[0:1] if rev else cum[CHUNK - 1:CHUNK]
    qi[slot] = q * factor(cum)
    ks[slot] = k * factor(tot - cum)
    et[slot] = jnp.exp2(tot)


def _stage_b(bufs, slot, v_ref, st_ref, r0, diag):
    qa, ka, qd, kd, qi, ks, et, ab, oi = bufs
    for n in range(N_HEADS):
        wide = slice(n * A_WIDTH, (n + 1) * A_WIDTH)
        head = slice(n * HEAD_W, (n + 1) * HEAD_W)
        scores = _dot_nt(qa[slot, :, wide], ka[slot, :, wide])
        scores = scores + jnp.where(diag, _dot_nt(qd[slot, :, head], kd[slot, :, head]), 0.0)
        ab[slot, n] = scores.astype(BF16)
        st = st_ref[n]
        oi[slot, :, head] = _dot_nt(qi[slot, :, head], st.astype(BF16))
        st_ref[n] = st * et[slot, :, head] + _dot_tn(v_ref[0, r0:r0 + CHUNK, head],
                                                    ks[slot, :, head])


def _stage_c(bufs, slot, v_ref, o_ref, r0):
    ab, oi = bufs[7], bufs[8]
    for n in range(N_HEADS):
        head = slice(n * HEAD_W, (n + 1) * HEAD_W)
        o = oi[slot, :, head] + _dot(ab[slot, n], v_ref[0, r0:r0 + CHUNK, head])
        o_ref[0, r0:r0 + CHUNK, head] = o.astype(o_ref.dtype)


def _scan_kernel(qf_ref, qb_ref, kf_ref, kb_ref, vf_ref, vb_ref, gfh_ref, gfl_ref, gbh_ref,
                 gbl_ref, of_ref, ob_ref, sf_ref, sb_ref, *buf_refs):
    n_chunks = MIX_TILE // CHUNK
    bufs_f, bufs_b = buf_refs[:len(buf_refs) // 2], buf_refs[len(buf_refs) // 2:]

    @pl.when(pl.program_id(1) == 0)
    def _():
        sf_ref[...] = jnp.zeros_like(sf_ref)
        sb_ref[...] = jnp.zeros_like(sb_ref)
        for bufs in (bufs_f, bufs_b):
            bufs[0][...] = jnp.zeros_like(bufs[0])
            bufs[1][...] = jnp.zeros_like(bufs[1])

    ri = lax.broadcasted_iota(jnp.int32, (CHUNK, CHUNK), 0)
    ci = lax.broadcasted_iota(jnp.int32, (CHUNK, CHUNK), 1)
    tri_f = (ci <= ri).astype(BF16)
    tri_b = (ci >= ri).astype(BF16)
    tri_f = jnp.concatenate([tri_f, tri_f], axis=1)
    tri_b = jnp.concatenate([tri_b, tri_b], axis=1)
    same16 = (ri // SUB) == (ci // SUB)
    diag_f = same16 & (ci <= ri)
    diag_b = same16 & (ci >= ri)
    row_f = lambda c: c * CHUNK
    row_b = lambda c: (n_chunks - 1 - c) * CHUNK

    def stage_a(c):
        _stage_a(False, qf_ref, kf_ref, gfh_ref, gfl_ref, row_f(c), tri_f, bufs_f, c % 2)
        _stage_a(True, qb_ref, kb_ref, gbh_ref, gbl_ref, row_b(c), tri_b, bufs_b, c % 2)

    stage_a(0)
    for c in range(n_chunks):
        if c + 1 < n_chunks:
            stage_a(c + 1)
        _stage_b(bufs_f, c % 2, vf_ref, sf_ref, row_f(c), diag_f)
        _stage_b(bufs_b, c % 2, vb_ref, sb_ref, row_b(c), diag_b)
        _stage_c(bufs_f, c % 2, vf_ref, of_ref, row_f(c))
        _stage_c(bufs_b, c % 2, vb_ref, ob_ref, row_b(c))


def _scan_bufs():
    wide = pltpu.VMEM((2, CHUNK, N_HEADS * A_WIDTH), BF16)
    narrow = pltpu.VMEM((2, CHUNK, D_MODEL), BF16)
    return [wide, wide, narrow, narrow, narrow, narrow,
            pltpu.VMEM((2, 1, D_MODEL), F32),
            pltpu.VMEM((2, N_HEADS, CHUNK, CHUNK), BF16),
            pltpu.VMEM((2, CHUNK, D_MODEL), F32)]


def _scan(layer, bsz, q, kf, kb, v, gfh, gfl, gbh, gbl):
    seq = q.shape[0] // bsz
    nt = seq // MIX_TILE
    to3 = lambda a: a.reshape(bsz, seq, D_MODEL)
    tile = lambda idx: pl.BlockSpec((1, MIX_TILE, D_MODEL), idx)
    fwd = lambda b, j: (b, j, 0)
    bwd = lambda b, j: (b, nt - 1 - j, 0)
    state = pltpu.VMEM((N_HEADS, HEAD_W, HEAD_W), F32)
    return pl.pallas_call(
        _scan_kernel,
        grid=(bsz, nt),
        in_specs=[tile(fwd), tile(bwd), tile(fwd), tile(bwd), tile(fwd), tile(bwd),
                  tile(fwd), tile(fwd), tile(bwd), tile(bwd)],
        out_specs=[tile(fwd), tile(bwd)],
        out_shape=[jax.ShapeDtypeStruct((bsz, seq, D_MODEL), BF16),
                   jax.ShapeDtypeStruct((bsz, seq, D_MODEL), BF16)],
        scratch_shapes=[state, state] + _scan_bufs() + _scan_bufs(),
        compiler_params=_params(2),
        name=f"scan_l{layer}",
    )(to3(q), to3(q), to3(kf), to3(kb), to3(v), to3(v), to3(gfh), to3(gfl), to3(gbh), to3(gbl))


def _post_math(of_ref, ob_ref, gate_ref, ones_ref, onw_ref, wout_ref, h):
    o = of_ref[...].astype(F32) + ob_ref[...].astype(F32)
    sq = (o * o).astype(BF16)
    ms = jnp.concatenate([_dot(sq[:, n * HEAD_W:(n + 1) * HEAD_W], ones_ref[...])
                          for n in range(N_HEADS)], axis=1) * (1.0 / HEAD_W)
    y = o * lax.rsqrt(ms + HEAD_EPS) * onw_ref[...]
    g = gate_ref[...].astype(F32)
    y = y * (g * _sigmoid(g))
    return h + _dot(y.astype(BF16), wout_ref[0])


def _post_inputs(layer, o_f, o_b, gates, gla_onorm, hg_onorm, w_out_all):
    onw = jnp.concatenate([jnp.tile(gla_onorm, 4), jnp.tile(hg_onorm, 4)]).reshape(1, D_MODEL)
    row = lambda: pl.BlockSpec((ROW_TILE, D_MODEL), lambda i: (i, 0))
    specs = [row(), row(), row(), _const_spec((HEAD_W, HEAD_W)), _const_spec((1, D_MODEL)),
             _layer_spec((D_MODEL, D_MODEL), layer)]
    return (o_f, o_b, gates, jnp.ones((HEAD_W, HEAD_W), BF16), onw, w_out_all), specs


def _ple_tail(h2, p, nple, wpp_ref, wpg_ref, nfinal, final):
    gate = _sigmoid(_dot(_rmsnorm(h2, nple).astype(BF16), wpg_ref[0]))
    h3 = h2 + _dot(p.astype(BF16), wpp_ref[0]) * gate
    if final:
        h3 = _rmsnorm(h3, nfinal)
    return h3


def _dense_kernel(final, of_ref, ob_ref, gate_ref, ones_ref, onw_ref, wout_ref, h_ref, p_ref,
                  nffn_ref, wg_ref, wu_ref, wd_ref, nple_ref, wpp_ref, wpg_ref, nfin_ref, out_ref):
    h1 = _post_math(of_ref, ob_ref, gate_ref, ones_ref, onw_ref, wout_ref, h_ref[...])
    v = _rmsnorm(h1, nffn_ref[...]).astype(BF16)
    acc = jnp.zeros_like(h1)
    for c in range(0, D_FF_DENSE, FF_CHUNK):
        a = _dot(v, wg_ref[0, :, c:c + FF_CHUNK])
        b = _dot(v, wu_ref[0, :, c:c + FF_CHUNK])
        acc = acc + _dot((a * _sigmoid(a) * b).astype(BF16), wd_ref[0, c:c + FF_CHUNK, :])
    out_ref[...] = _ple_tail(h1 + acc, p_ref[0], nple_ref[...], wpp_ref, wpg_ref,
                             nfin_ref[...], final)


def _dense_ffn(layer, final, post, h, p, norm_ffn, wg, wu, wd, norm_ple, w_pp, w_pg, norm_final):
    t = h.shape[0]
    vec = lambda w: w.reshape(1, D_MODEL)
    j = layer // 2
    return pl.pallas_call(
        functools.partial(_dense_kernel, final),
        grid=(t // ROW_TILE,),
        in_specs=post[1] + [
            pl.BlockSpec((ROW_TILE, D_MODEL), lambda i: (i, 0)),
            pl.BlockSpec((1, ROW_TILE, PLE_DIM), lambda i: (layer, i, 0)),
            _const_spec((1, D_MODEL)),
            _layer_spec((D_MODEL, D_FF_DENSE), j), _layer_spec((D_MODEL, D_FF_DENSE), j),
            _layer_spec((D_FF_DENSE, D_MODEL), j),
            _const_spec((1, D_MODEL)),
            _layer_spec((PLE_DIM, D_MODEL), layer), _layer_spec((D_MODEL, D_MODEL), layer),
            _const_spec((1, D_MODEL))],
        out_specs=pl.BlockSpec((ROW_TILE, D_MODEL), lambda i: (i, 0)),
        out_shape=jax.ShapeDtypeStruct((t, D_MODEL), F32),
        compiler_params=_params(1),
        name=f"dense_l{layer}",
    )(*post[0], h, p, vec(norm_ffn), wg, wu, wd, vec(norm_ple), w_pp, w_pg, vec(norm_final))


META_G1, META_G2, META_E1, META_E2, META_R1, META_R2 = range(6)
HALF = D_MODEL // 2
HI_MASK = 0xFFFF0000


def _pack_rows(x):
    bits = pltpu.bitcast(x.astype(BF16).astype(F32), jnp.uint32)
    return (bits[:, :HALF] >> 16) | (bits[:, HALF:] & jnp.uint32(HI_MASK))


def _unpack_rows(w):
    lo = pltpu.bitcast(w << 16, F32)
    hi = pltpu.bitcast(w & jnp.uint32(HI_MASK), F32)
    return jnp.concatenate([lo, hi], axis=1)


def _router_kernel(of_ref, ob_ref, gate_ref, ones_ref, onw_ref, wout_ref, h_ref, nffn_ref, wr_ref,
                   tri_ref, h1_ref, xp_ref, meta_ref, cnt_ref, base_ref):
    @pl.when(pl.program_id(0) == 0)
    def _():
        base_ref[...] = jnp.zeros_like(base_ref)

    h1 = _post_math(of_ref, ob_ref, gate_ref, ones_ref, onw_ref, wout_ref, h_ref[...])
    h1_ref[...] = h1
    vf = _rmsnorm(h1, nffn_ref[...])
    xp_ref[...] = _pack_rows(vf)
    v_hi = vf.astype(BF16)
    v_lo = (vf - v_hi.astype(F32)).astype(BF16)
    w = wr_ref[...]
    w_hi = w.astype(BF16)
    w_lo = (w - w_hi.astype(F32)).astype(BF16)
    logits = _dot(v_hi, w_hi) + _dot(v_lo, w_hi) + _dot(v_hi, w_lo)
    lane = lax.broadcasted_iota(jnp.int32, logits.shape, 1)
    neg = jnp.float32(-jnp.inf)
    lg = jnp.where(lane < N_EXPERTS, logits, neg)
    m1 = jnp.max(lg, axis=-1, keepdims=True)
    i1 = jnp.min(jnp.where(lg == m1, lane, LANES), axis=-1, keepdims=True)
    first = lane == i1
    lg2 = jnp.where(first, neg, lg)
    m2 = jnp.max(lg2, axis=-1, keepdims=True)
    i2 = jnp.min(jnp.where(lg2 == m2, lane, LANES), axis=-1, keepdims=True)
    second = lane == i2
    e = jnp.exp(m2 - m1)
    g1 = 1.0 / (1.0 + e)
    g2 = e * g1
    cnt = jnp.where(first | second, 1.0, 0.0)
    rank = base_ref[...] + _dot(tri_ref[...], cnt.astype(BF16))
    r1 = jnp.sum(jnp.where(first, rank, 0.0), axis=-1, keepdims=True)
    r2 = jnp.sum(jnp.where(second, rank, 0.0), axis=-1, keepdims=True)
    base_ref[...] = base_ref[...] + jnp.sum(cnt, axis=0, keepdims=True)
    cnt_ref[...] = base_ref[...]
    meta = jnp.zeros(logits.shape, F32)
    for col, val in ((META_G1, g1), (META_G2, g2), (META_E1, i1.astype(F32)),
                     (META_E2, i2.astype(F32)), (META_R1, r1), (META_R2, r2)):
        meta = jnp.where(lane == col, val, meta)
    meta_ref[...] = meta


def _router(layer, post, h, norm_ffn, w_router):
    t = h.shape[0]
    wr = jnp.pad(w_router, ((0, 0), (0, LANES - N_EXPERTS)))
    ids = jnp.arange(ROW_TILE)
    tri = (ids[None, :] < ids[:, None]).astype(BF16)
    return pl.pallas_call(
        _router_kernel,
        grid=(t // ROW_TILE,),
        in_specs=post[1] + [
            pl.BlockSpec((ROW_TILE, D_MODEL), lambda i: (i, 0)),
            _const_spec((1, D_MODEL)), _const_spec((D_MODEL, LANES)),
            _const_spec((ROW_TILE, ROW_TILE))],
        out_specs=[pl.BlockSpec((ROW_TILE, D_MODEL), lambda i: (i, 0)),
                   pl.BlockSpec((ROW_TILE, HALF), lambda i: (i, 0)),
                   pl.BlockSpec((ROW_TILE, LANES), lambda i: (i, 0)),
                   pl.BlockSpec((1, LANES), lambda i: (0, 0))],
        out_shape=[jax.ShapeDtypeStruct((t, D_MODEL), F32),
                   jax.ShapeDtypeStruct((t, HALF), jnp.uint32),
                   jax.ShapeDtypeStruct((t, LANES), F32),
                   jax.ShapeDtypeStruct((1, LANES), F32)],
        scratch_shapes=[pltpu.VMEM((1, LANES), F32)],
        compiler_params=_params(1),
        name=f"router_l{layer}",
    )(*post[0], h, norm_ffn.reshape(1, D_MODEL), wr, tri)


SC_CORES, SC_SUBCORES = 2, 16
SC_WORKERS = SC_CORES * SC_SUBCORES
SC_WINDOW = 64


def _sc_gather(table, idx):
    n_out, d = idx.shape[0], table.shape[1]
    per_worker = n_out // SC_WORKERS
    n_win = per_worker // SC_WINDOW
    assert n_win * SC_WINDOW * SC_WORKERS == n_out and n_win % 2 == 0
    mesh = plsc.VectorSubcoreMesh(core_axis_name="c", subcore_axis_name="s")

    def body(table_hbm, idx_hbm, out_hbm, idx_a, idx_b, rows_a, rows_b, sem_a, sem_b):
        wid = lax.axis_index("s") * SC_CORES + lax.axis_index("c")

        def rows_of(w):
            return pl.ds(pl.multiple_of(wid * per_worker + w * SC_WINDOW, SC_WINDOW), SC_WINDOW)

        def start(w, idx_v, rows_v, sem):
            pltpu.sync_copy(idx_hbm.at[rows_of(w)], idx_v)
            pltpu.async_copy(table_hbm.at[idx_v], rows_v, sem)

        def finish(w, idx_v, rows_v, sem):
            pltpu.make_async_copy(table_hbm.at[idx_v], rows_v, sem).wait()
            pltpu.sync_copy(rows_v, out_hbm.at[rows_of(w)])

        start(0, idx_a, rows_a, sem_a)

        @pl.loop(0, n_win, step=2)
        def _(w):
            start(w + 1, idx_b, rows_b, sem_b)
            finish(w, idx_a, rows_a, sem_a)

            @pl.when(w + 2 < n_win)
            def _():
                start(w + 2, idx_a, rows_a, sem_a)

            finish(w + 1, idx_b, rows_b, sem_b)

    return pl.kernel(
        body, mesh=mesh,
        out_type=jax.ShapeDtypeStruct((n_out, d), table.dtype),
        scratch_types=[pltpu.VMEM((SC_WINDOW,), jnp.int32), pltpu.VMEM((SC_WINDOW,), jnp.int32),
                       pltpu.VMEM((SC_WINDOW, d), table.dtype),
                       pltpu.VMEM((SC_WINDOW, d), table.dtype),
                       pltpu.SemaphoreType.DMA, pltpu.SemaphoreType.DMA],
    )(table, idx)


def _sc_scatter2(x, pos_a, pos_b, n_out):
    t, d = x.shape
    per_worker = t // SC_WORKERS
    n_win = per_worker // SC_WINDOW
    assert n_win * SC_WINDOW * SC_WORKERS == t and n_win % 2 == 0
    mesh = plsc.VectorSubcoreMesh(core_axis_name="c", subcore_axis_name="s")

    def body(x_hbm, pa_hbm, pb_hbm, out_hbm, ia0, ib0, ia1, ib1, rows0, rows1, sem0, sem1):
        wid = lax.axis_index("s") * SC_CORES + lax.axis_index("c")

        def rows_of(w):
            return pl.ds(pl.multiple_of(wid * per_worker + w * SC_WINDOW, SC_WINDOW), SC_WINDOW)

        def start(w, ia, ib, rows_v, sem):
            pltpu.sync_copy(x_hbm.at[rows_of(w)], rows_v)
            pltpu.sync_copy(pa_hbm.at[rows_of(w)], ia)
            pltpu.sync_copy(pb_hbm.at[rows_of(w)], ib)
            pltpu.async_copy(rows_v, out_hbm.at[ia], sem)
            pltpu.async_copy(rows_v, out_hbm.at[ib], sem)

        def finish(ia, ib, rows_v, sem):
            pltpu.make_async_copy(rows_v, out_hbm.at[ia], sem).wait()
            pltpu.make_async_copy(rows_v, out_hbm.at[ib], sem).wait()

        start(0, ia0, ib0, rows0, sem0)

        @pl.loop(0, n_win, step=2)
        def _(w):
            start(w + 1, ia1, ib1, rows1, sem1)
            finish(ia0, ib0, rows0, sem0)

            @pl.when(w + 2 < n_win)
            def _():
                start(w + 2, ia0, ib0, rows0, sem0)

            finish(ia1, ib1, rows1, sem1)

    idx = lambda: pltpu.VMEM((SC_WINDOW,), jnp.int32)
    return pl.kernel(
        body, mesh=mesh,
        out_type=jax.ShapeDtypeStruct((n_out, d), x.dtype),
        scratch_types=[idx(), idx(), idx(), idx(),
                       pltpu.VMEM((SC_WINDOW, d), x.dtype), pltpu.VMEM((SC_WINDOW, d), x.dtype),
                       pltpu.SemaphoreType.DMA, pltpu.SemaphoreType.DMA],
    )(x, pos_a, pos_b)


EXP_ROWS = 1024


def _experts_kernel(be_ref, na_ref, nv_ref, xs_ref, wg_ref, wu_ref, wd_ref, ys_ref, x_ref,
                    acc_ref):
    b = pl.program_id(0)
    f = pl.program_id(1)

    @pl.when(b < na_ref[0])
    def _():
        @pl.when(f == 0)
        def _():
            row = lax.broadcasted_iota(jnp.int32, (EXP_ROWS, D_MODEL), 0)
            x = jnp.where(row < nv_ref[b], _unpack_rows(xs_ref[...]), 0.0)
            x_ref[...] = x.astype(BF16)
            acc_ref[...] = jnp.zeros_like(acc_ref)

        x = x_ref[...]
        a = _dot(x, wg_ref[0, 0].astype(BF16))
        u = _dot(x, wu_ref[0, 0].astype(BF16))
        acc_ref[...] += _dot((a * _sigmoid(a) * u).astype(BF16), wd_ref[0, 0].astype(BF16))

        @pl.when(f == pl.num_programs(1) - 1)
        def _():
            ys_ref[...] = _pack_rows(acc_ref[...])


def _experts(layer, xs, blk_expert, n_active, blk_valid, wg, wu, wd):
    moe_idx = layer // 2
    rows = xs.shape[0]
    nf = D_FF_EXPERT // EXP_FF_CHUNK
    fsel = lambda b, f, na: jnp.where(b < na[0], f, nf - 1)
    grid_spec = pltpu.PrefetchScalarGridSpec(
        num_scalar_prefetch=3,
        grid=(rows // EXP_ROWS, nf),
        in_specs=[pl.BlockSpec((EXP_ROWS, HALF), lambda b, f, be, na, nv: (b, 0)),
                  pl.BlockSpec((1, 1, D_MODEL, EXP_FF_CHUNK),
                               lambda b, f, be, na, nv: (moe_idx, be[b], 0, fsel(b, f, na))),
                  pl.BlockSpec((1, 1, D_MODEL, EXP_FF_CHUNK),
                               lambda b, f, be, na, nv: (moe_idx, be[b], 0, fsel(b, f, na))),
                  pl.BlockSpec((1, 1, EXP_FF_CHUNK, D_MODEL),
                               lambda b, f, be, na, nv: (moe_idx, be[b], fsel(b, f, na), 0))],
        out_specs=pl.BlockSpec((EXP_ROWS, HALF), lambda b, f, be, na, nv: (b, 0)),
        scratch_shapes=[pltpu.VMEM((EXP_ROWS, D_MODEL), BF16),
                        pltpu.VMEM((EXP_ROWS, D_MODEL), F32)])
    return pl.pallas_call(
        _experts_kernel,
        grid_spec=grid_spec,
        out_shape=jax.ShapeDtypeStruct((rows, HALF), jnp.uint32),
        compiler_params=_params(2),
        name=f"experts_l{layer}",
    )(blk_expert, n_active, blk_valid, xs, wg, wu, wd)


def _combine_kernel(final, h_ref, y1_ref, y2_ref, meta_ref, p_ref, nple_ref, wpp_ref, wpg_ref,
                    nfin_ref, out_ref):
    meta = meta_ref[...]
    g1 = meta[:, META_G1:META_G1 + 1]
    g2 = meta[:, META_G2:META_G2 + 1]
    h2 = h_ref[...] + g1 * _unpack_rows(y1_ref[...]) + g2 * _unpack_rows(y2_ref[...])
    out_ref[...] = _ple_tail(h2, p_ref[0], nple_ref[...], wpp_ref, wpg_ref, nfin_ref[...], final)


def _combine(layer, final, h, yg, meta, p, norm_ple, w_pp, w_pg, norm_final):
    t = h.shape[0]
    nt = t // ROW_TILE
    vec = lambda w: w.reshape(1, D_MODEL)
    return pl.pallas_call(
        functools.partial(_combine_kernel, final),
        grid=(nt,),
        in_specs=[pl.BlockSpec((ROW_TILE, D_MODEL), lambda i: (i, 0)),
                  pl.BlockSpec((ROW_TILE, HALF), lambda i: (i, 0)),
                  pl.BlockSpec((ROW_TILE, HALF), lambda i: (i + nt, 0)),
                  pl.BlockSpec((ROW_TILE, LANES), lambda i: (i, 0)),
                  pl.BlockSpec((1, ROW_TILE, PLE_DIM), lambda i: (layer, i, 0)),
                  _const_spec((1, D_MODEL)),
                  _layer_spec((PLE_DIM, D_MODEL), layer), _layer_spec((D_MODEL, D_MODEL), layer),
                  _const_spec((1, D_MODEL))],
        out_specs=pl.BlockSpec((ROW_TILE, D_MODEL), lambda i: (i, 0)),
        out_shape=jax.ShapeDtypeStruct((t, D_MODEL), F32),
        compiler_params=_params(1),
        name=f"combine_l{layer}",
    )(h, yg, yg, meta, p, vec(norm_ple), w_pp, w_pg, vec(norm_final))


def _moe(layer, final, post, h, p, norm_ffn, w_router, wg, wu, wd, norm_ple, w_pp, w_pg,
         norm_final):
    t = h.shape[0]
    h1, xp, meta, counts = _router(layer, post, h, norm_ffn, w_router)
    cnt = counts[0, :N_EXPERTS].astype(jnp.int32)
    padded = ((cnt + EXP_ROWS - 1) // EXP_ROWS) * EXP_ROWS
    ends = jnp.cumsum(padded)
    offs = ends - padded
    e1 = meta[:, META_E1].astype(jnp.int32)
    e2 = meta[:, META_E2].astype(jnp.int32)
    pos1 = offs[e1] + meta[:, META_R1].astype(jnp.int32)
    pos2 = offs[e2] + meta[:, META_R2].astype(jnp.int32)
    rows = 2 * t + N_EXPERTS * EXP_ROWS
    blk_start = jnp.arange(rows // EXP_ROWS, dtype=jnp.int32) * EXP_ROWS
    blk_expert = jnp.minimum(
        jnp.sum(blk_start[:, None] >= ends[None, :], axis=1), N_EXPERTS - 1).astype(jnp.int32)
    n_active = (ends[-1:] // EXP_ROWS).astype(jnp.int32)
    blk_valid = jnp.clip((offs + cnt)[blk_expert] - blk_start, 0, EXP_ROWS).astype(jnp.int32)

    xs = _sc_scatter2(xp, pos1, pos2, rows)
    ys = _experts(layer, xs, blk_expert, n_active, blk_valid, wg, wu, wd)
    yg = _sc_gather(ys, jnp.concatenate([pos1, pos2]))
    return _combine(layer, final, h1, yg, meta, p, norm_ple, w_pp, w_pg, norm_final)


def kernel(x, p, w_in, gla_wg2, gla_bg, hg_lb_logits, gla_onorm, hg_onorm, w_out, norm_mix,
           norm_ffn, w_dense_gate, w_dense_up, w_dense_down, w_router, w_exp_gate, w_exp_up,
           w_exp_down, w_ple_proj, w_ple_gate, norm_ple, norm_final):
    bsz, seq, _ = x.shape
    t = bsz * seq
    w_proj = _proj_weights(w_in)
    w_out_b, w_pp, w_pg = w_out.astype(BF16), w_ple_proj.astype(BF16), w_ple_gate.astype(BF16)
    w_dg, w_du, w_dd = (w.astype(BF16) for w in (w_dense_gate, w_dense_up, w_dense_down))
    p_all = p.reshape(DEPTH, t, PLE_DIM)
    h = x.reshape(t, D_MODEL)
    for i in range(DEPTH):
        final = i == DEPTH - 1
        q, kf, kb, v, gates, gfh, gfl, gbh, gbl = _proj(
            i, h, norm_mix[i], w_proj, gla_wg2[i], gla_bg[i], hg_lb_logits)
        o_f, o_b = _scan(i, bsz, q, kf, kb, v, gfh, gfl, gbh, gbl)
        post = _post_inputs(i, o_f.reshape(t, D_MODEL), o_b.reshape(t, D_MODEL), gates,
                            gla_onorm[i], hg_onorm[i], w_out_b)
        if i % 2 == 0:
            h = _dense_ffn(i, final, post, h, p_all, norm_ffn[i], w_dg, w_du, w_dd, norm_ple[i],
                           w_pp, w_pg, norm_final)
        else:
            h = _moe(i, final, post, h, p_all, norm_ffn[i], w_router[i // 2], w_exp_gate, w_exp_up,
                     w_exp_down, norm_ple[i], w_pp, w_pg, norm_final)
    return h.reshape(bsz, seq, D_MODEL)
```

```python
import functools

import jax
import jax.numpy as jnp
from jax import lax
from jax.experimental import pallas as pl
from jax.experimental.pallas import tpu as pltpu
from jax.experimental.pallas import tpu_sc as plsc

F32 = jnp.float32
BF16 = jnp.bfloat16

D_MODEL = 1024
DEPTH = 4
N_HEADS = 8
HEAD_W = 128
GLA_DK = 64
GLA_RANK = 16
GLA_GATE_NORM = 16.0
HG_DK = 128
HG_K = 512
D_FF_DENSE = 2816
N_EXPERTS = 8
D_FF_EXPERT = 3584
PLE_DIM = 256
EPS = 1e-6
HEAD_EPS = 1e-5
F_MIN = 1e-6

CHUNK = 64
SUB = 16
EXP2_CLAMP = 86.0
LANES = 128
VMEM_LIMIT = 56 * 1024 * 1024

MIX_TILE = 256
ROW_TILE = 512
FF_CHUNK = 256
EXP_FF_CHUNK = 512

COL_HQ = 0
COL_HF = 512
COL_GLR = 1536
COL_GQ = 1664
COL_GK = 1920
COL_V = 2176
COL_GATE = 3200
NP_COLS = 4224
GLA_QK = 4 * GLA_DK
LOG2E = 1.4426950408889634


def _dot(a, b):
    return jnp.dot(a, b, preferred_element_type=F32)


def _dot_nt(a, b):
    return lax.dot_general(a, b, (((1,), (1,)), ((), ())), preferred_element_type=F32)


def _dot_tn(a, b):
    return lax.dot_general(a, b, (((0,), (0,)), ((), ())), preferred_element_type=F32)


def _sigmoid(x):
    return 1.0 / (1.0 + jnp.exp(-x))


def _rmsnorm(x, w):
    ms = jnp.mean(x * x, axis=-1, keepdims=True)
    return x * lax.rsqrt(ms + EPS) * w


def _const_spec(shape):
    nd = len(shape)
    return pl.BlockSpec(shape, lambda *_: (0,) * nd, pipeline_mode=pl.Buffered(1))


def _layer_spec(shape, layer):
    nd = len(shape)
    return pl.BlockSpec((1,) + tuple(shape), lambda *_: (layer,) + (0,) * nd,
                        pipeline_mode=pl.Buffered(1))


def _params(n_grid):
    return pltpu.CompilerParams(
        dimension_semantics=("arbitrary",) * n_grid, vmem_limit_bytes=VMEM_LIMIT)


PROJ_CHUNK = 256


def _proj_kernel(layer, h_ref, nw_ref, w_ref, wg2_ref, bg_ref, lbl_ref, spread_ref,
                 q_ref, kf_ref, kb_ref, v_ref, gate_ref, gfh_ref, gfl_ref, gbh_ref, gbl_ref,
                 u_ref):
    u_ref[...] = _rmsnorm(h_ref[...], nw_ref[...]).astype(BF16)
    proj = lambda c0, width: _dot(u_ref[...], w_ref[0, :, c0:c0 + width])
    chunks = lambda width: range(0, width, PROJ_CHUNK)

    def split(hi_ref, lo_ref, c0, g):
        hi = g.astype(BF16)
        hi_ref[:, c0:c0 + PROJ_CHUNK] = hi
        lo_ref[:, c0:c0 + PROJ_CHUNK] = (g - hi.astype(F32)).astype(BF16)

    dirs = ((kf_ref, gfh_ref, gfl_ref), (kb_ref, gbh_ref, gbl_ref))
    heavy, light = [], []

    glr = proj(COL_GLR, LANES).astype(BF16)

    def gla_decay(direction, c):
        _, hi_ref, lo_ref = dirs[direction]
        x = _dot(glr, wg2_ref[direction, :, c:c + PROJ_CHUNK].astype(BF16))
        x = x + bg_ref[direction, :, c:c + PROJ_CHUNK]
        log_sig = jnp.minimum(x, 0.0) - jnp.log(1.0 + jnp.exp(-jnp.abs(x)))
        split(hi_ref, lo_ref, c, log_sig * (LOG2E / GLA_GATE_NORM))

    def hgrn_forget(direction, c):
        k_ref, hi_ref, lo_ref = dirs[direction]
        rows = [lbl_ref[2 * d + direction:2 * d + direction + 1, c:c + PROJ_CHUNK]
                for d in range(DEPTH)]
        mx = functools.reduce(jnp.maximum, rows)
        ex = [jnp.exp(r - mx) for r in rows]
        lb = sum(ex[1:layer + 1], jnp.zeros_like(mx)) / sum(ex)
        z = proj(COL_HF + direction * HG_K + c, PROJ_CHUNK)
        e = jnp.exp(-jnp.abs(z))
        s_big = 1.0 / (1.0 + e)
        s_small = e * s_big
        sig = jnp.where(z >= 0, s_big, s_small)
        nsig = jnp.where(z >= 0, s_small, s_big)
        f = lb + (1.0 - lb) * sig
        split(hi_ref, lo_ref, 512 + c, jnp.log(jnp.maximum(f, F_MIN)) * LOG2E)
        k_ref[:, 512 + c:512 + c + PROJ_CHUNK] = ((1.0 - lb) * nsig).astype(BF16)

    def hgrn_query(c):
        hq = proj(COL_HQ + c, PROJ_CHUNK)
        q_ref[:, 512 + c:512 + c + PROJ_CHUNK] = (hq * _sigmoid(hq) * (HG_DK ** -0.5)).astype(BF16)

    def gla_query():
        gq = (proj(COL_GQ, GLA_QK) * (GLA_DK ** -0.5)).astype(BF16)
        q_ref[:, 0:512] = _dot(gq, spread_ref[...]).astype(BF16)

    def gla_key():
        gk = _dot(proj(COL_GK, GLA_QK).astype(BF16), spread_ref[...]).astype(BF16)
        kf_ref[:, 0:512] = gk
        kb_ref[:, 0:512] = gk

    def plain(out_ref, col, c):
        out_ref[:, c:c + PROJ_CHUNK] = proj(col + c, PROJ_CHUNK).astype(BF16)

    for direction in range(2):
        heavy += [functools.partial(hgrn_forget, direction, c) for c in chunks(HG_K)]
        heavy += [functools.partial(gla_decay, direction, c) for c in chunks(512)]
    heavy += [functools.partial(hgrn_query, c) for c in chunks(512)]
    light += [gla_query, gla_key]
    light += [functools.partial(plain, v_ref, COL_V, c) for c in chunks(D_MODEL)]
    light += [functools.partial(plain, gate_ref, COL_GATE, c) for c in chunks(D_MODEL)]
    for i in range(max(len(heavy), len(light))):
        for chunk in heavy[i:i + 1] + light[i:i + 1]:
            chunk()


def _pad_heads(w, n_heads, width):
    lead = w.shape[:-1]
    w = w.reshape(lead + (n_heads, width))
    w = jnp.pad(w, [(0, 0)] * len(lead) + [(0, 0), (0, HEAD_W - width)])
    return w.reshape(lead + (n_heads * HEAD_W,))


def _proj_weights(w_in):
    gq, gk, gv, gr, glr_f, glr_b, hq, hf_f, hf_b, hv, hr = jnp.split(
        w_in, [256, 512, 1024, 1536, 1552, 1568, 2080, 2592, 3104, 3616], axis=-1)
    glr = jnp.pad(jnp.concatenate([glr_f, glr_b], -1), ((0, 0), (0, 0), (0, LANES - 2 * GLA_RANK)))
    return jnp.concatenate([hq, hf_f, hf_b, glr, gq, gk, gv, hv, gr, hr], -1).astype(BF16)


def _proj(layer, h, norm_w, w_all, wg2, bg, lb_logits):
    t = h.shape[0]
    wg2_h = _pad_heads(wg2, 4, GLA_DK)
    wg2_p = jnp.stack([jnp.pad(wg2_h[0], ((0, LANES - GLA_RANK), (0, 0))),
                       jnp.pad(wg2_h[1], ((GLA_RANK, LANES - 2 * GLA_RANK), (0, 0)))])
    bg_p = _pad_heads(bg, 4, GLA_DK).reshape(2, 1, 512)
    lbl = lb_logits.reshape(DEPTH * 2, HG_K)
    src = jnp.arange(GLA_QK)
    spread = (jnp.arange(512)[None, :] == (src // GLA_DK * HEAD_W + src % GLA_DK)[:, None])
    row = lambda: pl.BlockSpec((ROW_TILE, D_MODEL), lambda i: (i, 0))
    return pl.pallas_call(
        functools.partial(_proj_kernel, layer),
        grid=(t // ROW_TILE,),
        in_specs=[row(), _const_spec((1, D_MODEL)), _layer_spec((D_MODEL, NP_COLS), layer),
                  _const_spec((2, LANES, 512)), _const_spec((2, 1, 512)),
                  _const_spec((DEPTH * 2, HG_K)), _const_spec((GLA_QK, 512))],
        out_specs=[row() for _ in range(9)],
        out_shape=[jax.ShapeDtypeStruct((t, D_MODEL), BF16) for _ in range(9)],
        scratch_shapes=[pltpu.VMEM((ROW_TILE, D_MODEL), BF16)],
        compiler_params=_params(1),
        name=f"proj_l{layer}",
    )(h, norm_w.reshape(1, D_MODEL), w_all, wg2_p, bg_p, lbl, spread.astype(BF16))


A_GROUPS = 3
A_WIDTH = A_GROUPS * HEAD_W


def _stage_a(rev, q_ref, k_ref, gh_ref, gl_ref, r0, tri, bufs, slot):
    qa, ka, qd, kd, qi, ks, et, _, _ = bufs
    rows = slice(r0, r0 + CHUNK)
    cum = _dot(tri, jnp.concatenate([gh_ref[0, rows, :], gl_ref[0, rows, :]], axis=0))
    q = q_ref[0, rows, :]
    k = k_ref[0, rows, :]
    factor = lambda x: jnp.exp2(x).astype(BF16)

    def put(ref, rows, group, val):
        for n in range(N_HEADS):
            c0 = n * A_WIDTH + group * HEAD_W
            ref[slot, rows, c0:c0 + HEAD_W] = val[:, n * HEAD_W:(n + 1) * HEAD_W]

    for group, (half, a) in enumerate(((2 * SUB, 0), (SUB, 0), (SUB, 2 * SUB))):
        lo, hi = slice(a, a + half), slice(a + half, a + 2 * half)
        if not rev:
            ref, k_rows, q_rows = cum[a + half - 1:a + half], lo, hi
        else:
            ref, q_rows, k_rows = cum[a + half:a + half + 1], lo, hi
        put(qa, q_rows, group, q[q_rows] * factor(cum[q_rows] - ref))
        put(ka, k_rows, group, k[k_rows] * factor(ref - cum[k_rows]))

    mids = []
    for a in range(0, CHUNK, SUB):
        m = 0.5 * (cum[a:a + 1] + cum[a + SUB - 1:a + SUB])
        mids.append(jnp.broadcast_to(m, (SUB, D_MODEL)))
    dd = cum - jnp.concatenate(mids, 0)
    qd[slot] = q * factor(jnp.clip(dd, -EXP2_CLAMP, EXP2_CLAMP))
    kd[slot] = k * factor(jnp.clip(-dd, -EXP2_CLAMP, EXP2_CLAMP))

    tot = cum[0:1] if rev else cum[CHUNK - 1:CHUNK]
    qi[slot] = q * factor(cum)
    ks[slot] = k * factor(tot - cum)
    et[slot] = jnp.exp2(tot)


def _stage_b(bufs, slot, v_ref, st_ref, r0, diag):
    qa, ka, qd, kd, qi, ks, et, ab, oi = bufs
    for n in range(N_HEADS):
        wide = slice(n * A_WIDTH, (n + 1) * A_WIDTH)
        head = slice(n * HEAD_W, (n + 1) * HEAD_W)
        scores = _dot_nt(qa[slot, :, wide], ka[slot, :, wide])
        scores = scores + jnp.where(diag, _dot_nt(qd[slot, :, head], kd[slot, :, head]), 0.0)
        ab[slot, n] = scores.astype(BF16)
        st = st_ref[n]
        oi[slot, :, head] = _dot_nt(qi[slot, :, head], st.astype(BF16))
        st_ref[n] = st * et[slot, :, head] + _dot_tn(v_ref[0, r0:r0 + CHUNK, head],
                                                    ks[slot, :, head])


def _stage_c(bufs, slot, v_ref, o_ref, r0):
    ab, oi = bufs[7], bufs[8]
    for n in range(N_HEADS):
        head = slice(n * HEAD_W, (n + 1) * HEAD_W)
        o = oi[slot, :, head] + _dot(ab[slot, n], v_ref[0, r0:r0 + CHUNK, head])
        o_ref[0, r0:r0 + CHUNK, head] = o.astype(o_ref.dtype)


def _scan_kernel(qf_ref, qb_ref, kf_ref, kb_ref, vf_ref, vb_ref, gfh_ref, gfl_ref, gbh_ref,
                 gbl_ref, of_ref, ob_ref, sf_ref, sb_ref, *buf_refs):
    n_chunks = MIX_TILE // CHUNK
    bufs_f, bufs_b = buf_refs[:len(buf_refs) // 2], buf_refs[len(buf_refs) // 2:]

    @pl.when(pl.program_id(1) == 0)
    def _():
        sf_ref[...] = jnp.zeros_like(sf_ref)
        sb_ref[...] = jnp.zeros_like(sb_ref)
        for bufs in (bufs_f, bufs_b):
            bufs[0][...] = jnp.zeros_like(bufs[0])
            bufs[1][...] = jnp.zeros_like(bufs[1])

    ri = lax.broadcasted_iota(jnp.int32, (CHUNK, CHUNK), 0)
    ci = lax.broadcasted_iota(jnp.int32, (CHUNK, CHUNK), 1)
    tri_f = (ci <= ri).astype(BF16)
    tri_b = (ci >= ri).astype(BF16)
    tri_f = jnp.concatenate([tri_f, tri_f], axis=1)
    tri_b = jnp.concatenate([tri_b, tri_b], axis=1)
    same16 = (ri // SUB) == (ci // SUB)
    diag_f = same16 & (ci <= ri)
    diag_b = same16 & (ci >= ri)
    row_f = lambda c: c * CHUNK
    row_b = lambda c: (n_chunks - 1 - c) * CHUNK

    def stage_a(c):
        _stage_a(False, qf_ref, kf_ref, gfh_ref, gfl_ref, row_f(c), tri_f, bufs_f, c % 2)
        _stage_a(True, qb_ref, kb_ref, gbh_ref, gbl_ref, row_b(c), tri_b, bufs_b, c % 2)

    stage_a(0)
    for c in range(n_chunks):
        if c + 1 < n_chunks:
            stage_a(c + 1)
        _stage_b(bufs_f, c % 2, vf_ref, sf_ref, row_f(c), diag_f)
        _stage_b(bufs_b, c % 2, vb_ref, sb_ref, row_b(c), diag_b)
        _stage_c(bufs_f, c % 2, vf_ref, of_ref, row_f(c))
        _stage_c(bufs_b, c % 2, vb_ref, ob_ref, row_b(c))


def _scan_bufs():
    wide = pltpu.VMEM((2, CHUNK, N_HEADS * A_WIDTH), BF16)
    narrow = pltpu.VMEM((2, CHUNK, D_MODEL), BF16)
    return [wide, wide, narrow, narrow, narrow, narrow,
            pltpu.VMEM((2, 1, D_MODEL), F32),
            pltpu.VMEM((2, N_HEADS, CHUNK, CHUNK), BF16),
            pltpu.VMEM((2, CHUNK, D_MODEL), F32)]


def _scan(layer, bsz, q, kf, kb, v, gfh, gfl, gbh, gbl):
    seq = q.shape[0] // bsz
    nt = seq // MIX_TILE
    to3 = lambda a: a.reshape(bsz, seq, D_MODEL)
    tile = lambda idx: pl.BlockSpec((1, MIX_TILE, D_MODEL), idx)
    fwd = lambda b, j: (b, j, 0)
    bwd = lambda b, j: (b, nt - 1 - j, 0)
    state = pltpu.VMEM((N_HEADS, HEAD_W, HEAD_W), F32)
    return pl.pallas_call(
        _scan_kernel,
        grid=(bsz, nt),
        in_specs=[tile(fwd), tile(bwd), tile(fwd), tile(bwd), tile(fwd), tile(bwd),
                  tile(fwd), tile(fwd), tile(bwd), tile(bwd)],
        out_specs=[tile(fwd), tile(bwd)],
        out_shape=[jax.ShapeDtypeStruct((bsz, seq, D_MODEL), BF16),
                   jax.ShapeDtypeStruct((bsz, seq, D_MODEL), BF16)],
        scratch_shapes=[state, state] + _scan_bufs() + _scan_bufs(),
        compiler_params=_params(2),
        name=f"scan_l{layer}",
    )(to3(q), to3(q), to3(kf), to3(kb), to3(v), to3(v), to3(gfh), to3(gfl), to3(gbh), to3(gbl))


def _post_math(of_ref, ob_ref, gate_ref, ones_ref, onw_ref, wout_ref, h):
    o = of_ref[...].astype(F32) + ob_ref[...].astype(F32)
    sq = (o * o).astype(BF16)
    ms = jnp.concatenate([_dot(sq[:, n * HEAD_W:(n + 1) * HEAD_W], ones_ref[...])
                          for n in range(N_HEADS)], axis=1) * (1.0 / HEAD_W)
    y = o * lax.rsqrt(ms + HEAD_EPS) * onw_ref[...]
    g = gate_ref[...].astype(F32)
    y = y * (g * _sigmoid(g))
    return h + _dot(y.astype(BF16), wout_ref[0])


def _post_inputs(layer, o_f, o_b, gates, gla_onorm, hg_onorm, w_out_all):
    onw = jnp.concatenate([jnp.tile(gla_onorm, 4), jnp.tile(hg_onorm, 4)]).reshape(1, D_MODEL)
    row = lambda: pl.BlockSpec((ROW_TILE, D_MODEL), lambda i: (i, 0))
    specs = [row(), row(), row(), _const_spec((HEAD_W, HEAD_W)), _const_spec((1, D_MODEL)),
             _layer_spec((D_MODEL, D_MODEL), layer)]
    return (o_f, o_b, gates, jnp.ones((HEAD_W, HEAD_W), BF16), onw, w_out_all), specs


def _ple_tail(h2, p, nple, wpp_ref, wpg_ref, nfinal, final):
    gate = _sigmoid(_dot(_rmsnorm(h2, nple).astype(BF16), wpg_ref[0]))
    h3 = h2 + _dot(p.astype(BF16), wpp_ref[0]) * gate
    if final:
        h3 = _rmsnorm(h3, nfinal)
    return h3


def _dense_kernel(final, of_ref, ob_ref, gate_ref, ones_ref, onw_ref, wout_ref, h_ref, p_ref,
                  nffn_ref, wg_ref, wu_ref, wd_ref, nple_ref, wpp_ref, wpg_ref, nfin_ref, out_ref):
    h1 = _post_math(of_ref, ob_ref, gate_ref, ones_ref, onw_ref, wout_ref, h_ref[...])
    v = _rmsnorm(h1, nffn_ref[...]).astype(BF16)
    acc = jnp.zeros_like(h1)
    for c in range(0, D_FF_DENSE, FF_CHUNK):
        a = _dot(v, wg_ref[0, :, c:c + FF_CHUNK])
        b = _dot(v, wu_ref[0, :, c:c + FF_CHUNK])
        acc = acc + _dot((a * _sigmoid(a) * b).astype(BF16), wd_ref[0, c:c + FF_CHUNK, :])
    out_ref[...] = _ple_tail(h1 + acc, p_ref[0], nple_ref[...], wpp_ref, wpg_ref,
                             nfin_ref[...], final)


def _dense_ffn(layer, final, post, h, p, norm_ffn, wg, wu, wd, norm_ple, w_pp, w_pg, norm_final):
    t = h.shape[0]
    vec = lambda w: w.reshape(1, D_MODEL)
    j = layer // 2
    return pl.pallas_call(
        functools.partial(_dense_kernel, final),
        grid=(t // ROW_TILE,),
        in_specs=post[1] + [
            pl.BlockSpec((ROW_TILE, D_MODEL), lambda i: (i, 0)),
            pl.BlockSpec((1, ROW_TILE, PLE_DIM), lambda i: (layer, i, 0)),
            _const_spec((1, D_MODEL)),
            _layer_spec((D_MODEL, D_FF_DENSE), j), _layer_spec((D_MODEL, D_FF_DENSE), j),
            _layer_spec((D_FF_DENSE, D_MODEL), j),
            _const_spec((1, D_MODEL)),
            _layer_spec((PLE_DIM, D_MODEL), layer), _layer_spec((D_MODEL, D_MODEL), layer),
            _const_spec((1, D_MODEL))],
        out_specs=pl.BlockSpec((ROW_TILE, D_MODEL), lambda i: (i, 0)),
        out_shape=jax.ShapeDtypeStruct((t, D_MODEL), F32),
        compiler_params=_params(1),
        name=f"dense_l{layer}",
    )(*post[0], h, p, vec(norm_ffn), wg, wu, wd, vec(norm_ple), w_pp, w_pg, vec(norm_final))


META_G1, META_G2, META_E1, META_E2, META_R1, META_R2 = range(6)
HALF = D_MODEL // 2
HI_MASK = 0xFFFF0000


def _pack_rows(x):
    bits = pltpu.bitcast(x.astype(BF16).astype(F32), jnp.uint32)
    return (bits[:, :HALF] >> 16) | (bits[:, HALF:] & jnp.uint32(HI_MASK))


def _unpack_rows(w):
    lo = pltpu.bitcast(w << 16, F32)
    hi = pltpu.bitcast(w & jnp.uint32(HI_MASK), F32)
    return jnp.concatenate([lo, hi], axis=1)


def _router_kernel(of_ref, ob_ref, gate_ref, ones_ref, onw_ref, wout_ref, h_ref, nffn_ref, wr_ref,
                   tri_ref, h1_ref, xp_ref, meta_ref, cnt_ref, base_ref):
    @pl.when(pl.program_id(0) == 0)
    def _():
        base_ref[...] = jnp.zeros_like(base_ref)

    h1 = _post_math(of_ref, ob_ref, gate_ref, ones_ref, onw_ref, wout_ref, h_ref[...])
    h1_ref[...] = h1
    vf = _rmsnorm(h1, nffn_ref[...])
    xp_ref[...] = _pack_rows(vf)
    v_hi = vf.astype(BF16)
    v_lo = (vf - v_hi.astype(F32)).astype(BF16)
    w = wr_ref[...]
    w_hi = w.astype(BF16)
    w_lo = (w - w_hi.astype(F32)).astype(BF16)
    logits = _dot(v_hi, w_hi) + _dot(v_lo, w_hi) + _dot(v_hi, w_lo)
    lane = lax.broadcasted_iota(jnp.int32, logits.shape, 1)
    neg = jnp.float32(-jnp.inf)
    lg = jnp.where(lane < N_EXPERTS, logits, neg)
    m1 = jnp.max(lg, axis=-1, keepdims=True)
    i1 = jnp.min(jnp.where(lg == m1, lane, LANES), axis=-1, keepdims=True)
    first = lane == i1
    lg2 = jnp.where(first, neg, lg)
    m2 = jnp.max(lg2, axis=-1, keepdims=True)
    i2 = jnp.min(jnp.where(lg2 == m2, lane, LANES), axis=-1, keepdims=True)
    second = lane == i2
    e = jnp.exp(m2 - m1)
    g1 = 1.0 / (1.0 + e)
    g2 = e * g1
    cnt = jnp.where(first | second, 1.0, 0.0)
    rank = base_ref[...] + _dot(tri_ref[...], cnt.astype(BF16))
    r1 = jnp.sum(jnp.where(first, rank, 0.0), axis=-1, keepdims=True)
    r2 = jnp.sum(jnp.where(second, rank, 0.0), axis=-1, keepdims=True)
    base_ref[...] = base_ref[...] + jnp.sum(cnt, axis=0, keepdims=True)
    cnt_ref[...] = base_ref[...]
    meta = jnp.zeros(logits.shape, F32)
    for col, val in ((META_G1, g1), (META_G2, g2), (META_E1, i1.astype(F32)),
                     (META_E2, i2.astype(F32)), (META_R1, r1), (META_R2, r2)):
        meta = jnp.where(lane == col, val, meta)
    meta_ref[...] = meta


def _router(layer, post, h, norm_ffn, w_router):
    t = h.shape[0]
    wr = jnp.pad(w_router, ((0, 0), (0, LANES - N_EXPERTS)))
    ids = jnp.arange(ROW_TILE)
    tri = (ids[None, :] < ids[:, None]).astype(BF16)
    return pl.pallas_call(
        _router_kernel,
        grid=(t // ROW_TILE,),
        in_specs=post[1] + [
            pl.BlockSpec((ROW_TILE, D_MODEL), lambda i: (i, 0)),
            _const_spec((1, D_MODEL)), _const_spec((D_MODEL, LANES)),
            _const_spec((ROW_TILE, ROW_TILE))],
        out_specs=[pl.BlockSpec((ROW_TILE, D_MODEL), lambda i: (i, 0)),
                   pl.BlockSpec((ROW_TILE, HALF), lambda i: (i, 0)),
                   pl.BlockSpec((ROW_TILE, LANES), lambda i: (i, 0)),
                   pl.BlockSpec((1, LANES), lambda i: (0, 0))],
        out_shape=[jax.ShapeDtypeStruct((t, D_MODEL), F32),
                   jax.ShapeDtypeStruct((t, HALF), jnp.uint32),
                   jax.ShapeDtypeStruct((t, LANES), F32),
                   jax.ShapeDtypeStruct((1, LANES), F32)],
        scratch_shapes=[pltpu.VMEM((1, LANES), F32)],
        compiler_params=_params(1),
        name=f"router_l{layer}",
    )(*post[0], h, norm_ffn.reshape(1, D_MODEL), wr, tri)


SC_CORES, SC_SUBCORES = 2, 16
SC_WORKERS = SC_CORES * SC_SUBCORES
SC_WINDOW = 64


def _sc_gather(table, idx):
    n_out, d = idx.shape[0], table.shape[1]
    per_worker = n_out // SC_WORKERS
    n_win = per_worker // SC_WINDOW
    assert n_win * SC_WINDOW * SC_WORKERS == n_out and n_win % 2 == 0
    mesh = plsc.VectorSubcoreMesh(core_axis_name="c", subcore_axis_name="s")

    def body(table_hbm, idx_hbm, out_hbm, idx_a, idx_b, rows_a, rows_b, sem_a, sem_b):
        wid = lax.axis_index("s") * SC_CORES + lax.axis_index("c")

        def rows_of(w):
            return pl.ds(pl.multiple_of(wid * per_worker + w * SC_WINDOW, SC_WINDOW), SC_WINDOW)

        def start(w, idx_v, rows_v, sem):
            pltpu.sync_copy(idx_hbm.at[rows_of(w)], idx_v)
            pltpu.async_copy(table_hbm.at[idx_v], rows_v, sem)

        def finish(w, idx_v, rows_v, sem):
            pltpu.make_async_copy(table_hbm.at[idx_v], rows_v, sem).wait()
            pltpu.sync_copy(rows_v, out_hbm.at[rows_of(w)])

        start(0, idx_a, rows_a, sem_a)

        @pl.loop(0, n_win, step=2)
        def _(w):
            start(w + 1, idx_b, rows_b, sem_b)
            finish(w, idx_a, rows_a, sem_a)

            @pl.when(w + 2 < n_win)
            def _():
                start(w + 2, idx_a, rows_a, sem_a)

            finish(w + 1, idx_b, rows_b, sem_b)

    return pl.kernel(
        body, mesh=mesh,
        out_type=jax.ShapeDtypeStruct((n_out, d), table.dtype),
        scratch_types=[pltpu.VMEM((SC_WINDOW,), jnp.int32), pltpu.VMEM((SC_WINDOW,), jnp.int32),
                       pltpu.VMEM((SC_WINDOW, d), table.dtype),
                       pltpu.VMEM((SC_WINDOW, d), table.dtype),
                       pltpu.SemaphoreType.DMA, pltpu.SemaphoreType.DMA],
    )(table, idx)


def _sc_scatter2(x, pos_a, pos_b, n_out):
    t, d = x.shape
    per_worker = t // SC_WORKERS
    n_win = per_worker // SC_WINDOW
    assert n_win * SC_WINDOW * SC_WORKERS == t and n_win % 2 == 0
    mesh = plsc.VectorSubcoreMesh(core_axis_name="c", subcore_axis_name="s")

    def body(x_hbm, pa_hbm, pb_hbm, out_hbm, ia0, ib0, ia1, ib1, rows0, rows1, sem0, sem1):
        wid = lax.axis_index("s") * SC_CORES + lax.axis_index("c")

        def rows_of(w):
            return pl.ds(pl.multiple_of(wid * per_worker + w * SC_WINDOW, SC_WINDOW), SC_WINDOW)

        def start(w, ia, ib, rows_v, sem):
            pltpu.sync_copy(x_hbm.at[rows_of(w)], rows_v)
            pltpu.sync_copy(pa_hbm.at[rows_of(w)], ia)
            pltpu.sync_copy(pb_hbm.at[rows_of(w)], ib)
            pltpu.async_copy(rows_v, out_hbm.at[ia], sem)
            pltpu.async_copy(rows_v, out_hbm.at[ib], sem)

        def finish(ia, ib, rows_v, sem):
            pltpu.make_async_copy(rows_v, out_hbm.at[ia], sem).wait()
            pltpu.make_async_copy(rows_v, out_hbm.at[ib], sem).wait()

        start(0, ia0, ib0, rows0, sem0)

        @pl.loop(0, n_win, step=2)
        def _(w):
            start(w + 1, ia1, ib1, rows1, sem1)
            finish(ia0, ib0, rows0, sem0)

            @pl.when(w + 2 < n_win)
            def _():
                start(w + 2, ia0, ib0, rows0, sem0)

            finish(ia1, ib1, rows1, sem1)

    idx = lambda: pltpu.VMEM((SC_WINDOW,), jnp.int32)
    return pl.kernel(
        body, mesh=mesh,
        out_type=jax.ShapeDtypeStruct((n_out, d), x.dtype),
        scratch_types=[idx(), idx(), idx(), idx(),
                       pltpu.VMEM((SC_WINDOW, d), x.dtype), pltpu.VMEM((SC_WINDOW, d), x.dtype),
                       pltpu.SemaphoreType.DMA, pltpu.SemaphoreType.DMA],
    )(x, pos_a, pos_b)


EXP_ROWS = 1024


def _experts_kernel(be_ref, na_ref, nv_ref, xs_ref, wg_ref, wu_ref, wd_ref, ys_ref, x_ref,
                    acc_ref):
    b = pl.program_id(0)
    f = pl.program_id(1)

    @pl.when(b < na_ref[0])
    def _():
        @pl.when(f == 0)
        def _():
            row = lax.broadcasted_iota(jnp.int32, (EXP_ROWS, D_MODEL), 0)
            x = jnp.where(row < nv_ref[b], _unpack_rows(xs_ref[...]), 0.0)
            x_ref[...] = x.astype(BF16)
            acc_ref[...] = jnp.zeros_like(acc_ref)

        x = x_ref[...]
        halves = [slice(c, c + EXP_FF_CHUNK // 2) for c in (0, EXP_FF_CHUNK // 2)]
        gate_up = [(_dot(x, wg_ref[0, 0, :, cols].astype(BF16)),
                    _dot(x, wu_ref[0, 0, :, cols].astype(BF16))) for cols in halves]
        down = acc_ref[...]
        for (a, u), cols in zip(gate_up, halves):
            down = down + _dot((a * _sigmoid(a) * u).astype(BF16),
                               wd_ref[0, 0, cols, :].astype(BF16))
        acc_ref[...] = down

        @pl.when(f == pl.num_programs(1) - 1)
        def _():
            ys_ref[...] = _pack_rows(acc_ref[...])


def _experts(layer, xs, blk_expert, n_active, blk_valid, wg, wu, wd):
    moe_idx = layer // 2
    rows = xs.shape[0]
    nf = D_FF_EXPERT // EXP_FF_CHUNK
    fsel = lambda b, f, na: jnp.where(b < na[0], f, nf - 1)
    grid_spec = pltpu.PrefetchScalarGridSpec(
        num_scalar_prefetch=3,
        grid=(rows // EXP_ROWS, nf),
        in_specs=[pl.BlockSpec((EXP_ROWS, HALF), lambda b, f, be, na, nv: (b, 0)),
                  pl.BlockSpec((1, 1, D_MODEL, EXP_FF_CHUNK),
                               lambda b, f, be, na, nv: (moe_idx, be[b], 0, fsel(b, f, na))),
                  pl.BlockSpec((1, 1, D_MODEL, EXP_FF_CHUNK),
                               lambda b, f, be, na, nv: (moe_idx, be[b], 0, fsel(b, f, na))),
                  pl.BlockSpec((1, 1, EXP_FF_CHUNK, D_MODEL),
                               lambda b, f, be, na, nv: (moe_idx, be[b], fsel(b, f, na), 0))],
        out_specs=pl.BlockSpec((EXP_ROWS, HALF), lambda b, f, be, na, nv: (b, 0)),
        scratch_shapes=[pltpu.VMEM((EXP_ROWS, D_MODEL), BF16),
                        pltpu.VMEM((EXP_ROWS, D_MODEL), F32)])
    return pl.pallas_call(
        _experts_kernel,
        grid_spec=grid_spec,
        out_shape=jax.ShapeDtypeStruct((rows, HALF), jnp.uint32),
        compiler_params=_params(2),
        name=f"experts_l{layer}",
    )(blk_expert, n_active, blk_valid, xs, wg, wu, wd)


def _combine_kernel(final, h_ref, y1_ref, y2_ref, meta_ref, p_ref, nple_ref, wpp_ref, wpg_ref,
                    nfin_ref, out_ref):
    meta = meta_ref[...]
    g1 = meta[:, META_G1:META_G1 + 1]
    g2 = meta[:, META_G2:META_G2 + 1]
    h2 = h_ref[...] + g1 * _unpack_rows(y1_ref[...]) + g2 * _unpack_rows(y2_ref[...])
    out_ref[...] = _ple_tail(h2, p_ref[0], nple_ref[...], wpp_ref, wpg_ref, nfin_ref[...], final)


def _combine(layer, final, h, yg, meta, p, norm_ple, w_pp, w_pg, norm_final):
    t = h.shape[0]
    nt = t // ROW_TILE
    vec = lambda w: w.reshape(1, D_MODEL)
    return pl.pallas_call(
        functools.partial(_combine_kernel, final),
        grid=(nt,),
        in_specs=[pl.BlockSpec((ROW_TILE, D_MODEL), lambda i: (i, 0)),
                  pl.BlockSpec((ROW_TILE, HALF), lambda i: (i, 0)),
                  pl.BlockSpec((ROW_TILE, HALF), lambda i: (i + nt, 0)),
                  pl.BlockSpec((ROW_TILE, LANES), lambda i: (i, 0)),
                  pl.BlockSpec((1, ROW_TILE, PLE_DIM), lambda i: (layer, i, 0)),
                  _const_spec((1, D_MODEL)),
                  _layer_spec((PLE_DIM, D_MODEL), layer), _layer_spec((D_MODEL, D_MODEL), layer),
                  _const_spec((1, D_MODEL))],
        out_specs=pl.BlockSpec((ROW_TILE, D_MODEL), lambda i: (i, 0)),
        out_shape=jax.ShapeDtypeStruct((t, D_MODEL), F32),
        compiler_params=_params(1),
        name=f"combine_l{layer}",
    )(h, yg, yg, meta, p, vec(norm_ple), w_pp, w_pg, vec(norm_final))


def _moe(layer, final, post, h, p, norm_ffn, w_router, wg, wu, wd, norm_ple, w_pp, w_pg,
         norm_final):
    t = h.shape[0]
    h1, xp, meta, counts = _router(layer, post, h, norm_ffn, w_router)
    cnt = counts[0, :N_EXPERTS].astype(jnp.int32)
    padded = ((cnt + EXP_ROWS - 1) // EXP_ROWS) * EXP_ROWS
    ends = jnp.cumsum(padded)
    offs = ends - padded
    e1 = meta[:, META_E1].astype(jnp.int32)
    e2 = meta[:, META_E2].astype(jnp.int32)
    pos1 = offs[e1] + meta[:, META_R1].astype(jnp.int32)
    pos2 = offs[e2] + meta[:, META_R2].astype(jnp.int32)
    rows = 2 * t + N_EXPERTS * EXP_ROWS
    blk_start = jnp.arange(rows // EXP_ROWS, dtype=jnp.int32) * EXP_ROWS
    blk_expert = jnp.minimum(
        jnp.sum(blk_start[:, None] >= ends[None, :], axis=1), N_EXPERTS - 1).astype(jnp.int32)
    n_active = (ends[-1:] // EXP_ROWS).astype(jnp.int32)
    blk_valid = jnp.clip((offs + cnt)[blk_expert] - blk_start, 0, EXP_ROWS).astype(jnp.int32)

    xs = _sc_scatter2(xp, pos1, pos2, rows)
    ys = _experts(layer, xs, blk_expert, n_active, blk_valid, wg, wu, wd)
    yg = _sc_gather(ys, jnp.concatenate([pos1, pos2]))
    return _combine(layer, final, h1, yg, meta, p, norm_ple, w_pp, w_pg, norm_final)


def kernel(x, p, w_in, gla_wg2, gla_bg, hg_lb_logits, gla_onorm, hg_onorm, w_out, norm_mix,
           norm_ffn, w_dense_gate, w_dense_up, w_dense_down, w_router, w_exp_gate, w_exp_up,
           w_exp_down, w_ple_proj, w_ple_gate, norm_ple, norm_final):
    bsz, seq, _ = x.shape
    t = bsz * seq
    w_proj = _proj_weights(w_in)
    w_out_b, w_pp, w_pg = w_out.astype(BF16), w_ple_proj.astype(BF16), w_ple_gate.astype(BF16)
    w_dg, w_du, w_dd = (w.astype(BF16) for w in (w_dense_gate, w_dense_up, w_dense_down))
    p_all = p.reshape(DEPTH, t, PLE_DIM)
    h = x.reshape(t, D_MODEL)
    for i in range(DEPTH):
        final = i == DEPTH - 1
        q, kf, kb, v, gates, gfh, gfl, gbh, gbl = _proj(
            i, h, norm_mix[i], w_proj, gla_wg2[i], gla_bg[i], hg_lb_logits)
        o_f, o_b = _scan(i, bsz, q, kf, kb, v, gfh, gfl, gbh, gbl)
        post = _post_inputs(i, o_f.reshape(t, D_MODEL), o_b.reshape(t, D_MODEL), gates,
                            gla_onorm[i], hg_onorm[i], w_out_b)
        if i % 2 == 0:
            h = _dense_ffn(i, final, post, h, p_all, norm_ffn[i], w_dg, w_du, w_dd, norm_ple[i],
                           w_pp, w_pg, norm_final)
        else:
            h = _moe(i, final, post, h, p_all, norm_ffn[i], w_router[i // 2], w_exp_gate, w_exp_up,
                     w_exp_down, norm_ple[i], w_pp, w_pg, norm_final)
    return h.reshape(bsz, seq, D_MODEL)
```

```python
import functools

import jax
import jax.numpy as jnp
from jax import lax
from jax.experimental import pallas as pl
from jax.experimental.pallas import tpu as pltpu
from jax.experimental.pallas import tpu_sc as plsc

F32 = jnp.float32
BF16 = jnp.bfloat16

D_MODEL = 1024
DEPTH = 4
N_HEADS = 8
HEAD_W = 128
GLA_DK = 64
GLA_RANK = 16
GLA_GATE_NORM = 16.0
HG_DK = 128
HG_K = 512
D_FF_DENSE = 2816
N_EXPERTS = 8
D_FF_EXPERT = 3584
PLE_DIM = 256
EPS = 1e-6
HEAD_EPS = 1e-5
F_MIN = 1e-6

CHUNK = 64
SUB = 16
EXP2_CLAMP = 86.0
LANES = 128
VMEM_LIMIT = 56 * 1024 * 1024

MIX_TILE = 256
ROW_TILE = 512
FF_CHUNK = 256
EXP_FF_CHUNK = 512

COL_HQ = 0
COL_HF = 512
COL_GLR = 1536
COL_GQ = 1664
COL_GK = 1920
COL_V = 2176
COL_GATE = 3200
NP_COLS = 4224
GLA_QK = 4 * GLA_DK
LOG2E = 1.4426950408889634


def _dot(a, b):
    return jnp.dot(a, b, preferred_element_type=F32)


def _dot_nt(a, b):
    return lax.dot_general(a, b, (((1,), (1,)), ((), ())), preferred_element_type=F32)


def _dot_tn(a, b):
    return lax.dot_general(a, b, (((0,), (0,)), ((), ())), preferred_element_type=F32)


def _sigmoid(x):
    return 1.0 / (1.0 + jnp.exp(-x))


def _rmsnorm(x, w):
    ms = jnp.mean(x * x, axis=-1, keepdims=True)
    return x * lax.rsqrt(ms + EPS) * w


def _const_spec(shape):
    nd = len(shape)
    return pl.BlockSpec(shape, lambda *_: (0,) * nd, pipeline_mode=pl.Buffered(1))


def _layer_spec(shape, layer):
    nd = len(shape)
    return pl.BlockSpec((1,) + tuple(shape), lambda *_: (layer,) + (0,) * nd,
                        pipeline_mode=pl.Buffered(1))


def _params(n_grid):
    return pltpu.CompilerParams(
        dimension_semantics=("arbitrary",) * n_grid, vmem_limit_bytes=VMEM_LIMIT)


PROJ_CHUNK = 256


def _proj_kernel(layer, h_ref, nw_ref, w_ref, wg2_ref, bg_ref, lbl_ref, spread_ref,
                 q_ref, kf_ref, kb_ref, v_ref, gate_ref, gfh_ref, gfl_ref, gbh_ref, gbl_ref,
                 u_ref):
    u_ref[...] = _rmsnorm(h_ref[...], nw_ref[...]).astype(BF16)
    proj = lambda c0, width: _dot(u_ref[...], w_ref[0, :, c0:c0 + width])
    chunks = lambda width: range(0, width, PROJ_CHUNK)

    def split(hi_ref, lo_ref, c0, g):
        hi = g.astype(BF16)
        hi_ref[:, c0:c0 + PROJ_CHUNK] = hi
        lo_ref[:, c0:c0 + PROJ_CHUNK] = (g - hi.astype(F32)).astype(BF16)

    dirs = ((kf_ref, gfh_ref, gfl_ref), (kb_ref, gbh_ref, gbl_ref))
    heavy, light = [], []

    glr = proj(COL_GLR, LANES).astype(BF16)

    def gla_decay(direction, c):
        _, hi_ref, lo_ref = dirs[direction]
        x = _dot(glr, wg2_ref[direction, :, c:c + PROJ_CHUNK].astype(BF16))
        x = x + bg_ref[direction, :, c:c + PROJ_CHUNK]
        log_sig = jnp.minimum(x, 0.0) - jnp.log(1.0 + jnp.exp(-jnp.abs(x)))
        split(hi_ref, lo_ref, c, log_sig * (LOG2E / GLA_GATE_NORM))

    def hgrn_forget(direction, c):
        k_ref, hi_ref, lo_ref = dirs[direction]
        rows = [lbl_ref[2 * d + direction:2 * d + direction + 1, c:c + PROJ_CHUNK]
                for d in range(DEPTH)]
        mx = functools.reduce(jnp.maximum, rows)
        ex = [jnp.exp(r - mx) for r in rows]
        lb = sum(ex[1:layer + 1], jnp.zeros_like(mx)) / sum(ex)
        z = proj(COL_HF + direction * HG_K + c, PROJ_CHUNK)
        e = jnp.exp(-jnp.abs(z))
        s_big = 1.0 / (1.0 + e)
        s_small = e * s_big
        sig = jnp.where(z >= 0, s_big, s_small)
        nsig = jnp.where(z >= 0, s_small, s_big)
        f = lb + (1.0 - lb) * sig
        split(hi_ref, lo_ref, 512 + c, jnp.log(jnp.maximum(f, F_MIN)) * LOG2E)
        k_ref[:, 512 + c:512 + c + PROJ_CHUNK] = ((1.0 - lb) * nsig).astype(BF16)

    def hgrn_query(c):
        hq = proj(COL_HQ + c, PROJ_CHUNK)
        q_ref[:, 512 + c:512 + c + PROJ_CHUNK] = (hq * _sigmoid(hq) * (HG_DK ** -0.5)).astype(BF16)

    def gla_query():
        gq = (proj(COL_GQ, GLA_QK) * (GLA_DK ** -0.5)).astype(BF16)
        q_ref[:, 0:512] = _dot(gq, spread_ref[...]).astype(BF16)

    def gla_key():
        gk = _dot(proj(COL_GK, GLA_QK).astype(BF16), spread_ref[...]).astype(BF16)
        kf_ref[:, 0:512] = gk
        kb_ref[:, 0:512] = gk

    def plain(out_ref, col, c):
        out_ref[:, c:c + PROJ_CHUNK] = proj(col + c, PROJ_CHUNK).astype(BF16)

    for direction in range(2):
        heavy += [functools.partial(hgrn_forget, direction, c) for c in chunks(HG_K)]
        heavy += [functools.partial(gla_decay, direction, c) for c in chunks(512)]
    heavy += [functools.partial(hgrn_query, c) for c in chunks(512)]
    light += [gla_query, gla_key]
    light += [functools.partial(plain, v_ref, COL_V, c) for c in chunks(D_MODEL)]
    light += [functools.partial(plain, gate_ref, COL_GATE, c) for c in chunks(D_MODEL)]
    for i in range(max(len(heavy), len(light))):
        for chunk in heavy[i:i + 1] + light[i:i + 1]:
            chunk()


def _pad_heads(w, n_heads, width):
    lead = w.shape[:-1]
    w = w.reshape(lead + (n_heads, width))
    w = jnp.pad(w, [(0, 0)] * len(lead) + [(0, 0), (0, HEAD_W - width)])
    return w.reshape(lead + (n_heads * HEAD_W,))


def _proj_weights(w_in):
    gq, gk, gv, gr, glr_f, glr_b, hq, hf_f, hf_b, hv, hr = jnp.split(
        w_in, [256, 512, 1024, 1536, 1552, 1568, 2080, 2592, 3104, 3616], axis=-1)
    glr = jnp.pad(jnp.concatenate([glr_f, glr_b], -1), ((0, 0), (0, 0), (0, LANES - 2 * GLA_RANK)))
    return jnp.concatenate([hq, hf_f, hf_b, glr, gq, gk, gv, hv, gr, hr], -1).astype(BF16)


def _proj(layer, h, norm_w, w_all, wg2, bg, lb_logits):
    t = h.shape[0]
    wg2_h = _pad_heads(wg2, 4, GLA_DK)
    wg2_p = jnp.stack([jnp.pad(wg2_h[0], ((0, LANES - GLA_RANK), (0, 0))),
                       jnp.pad(wg2_h[1], ((GLA_RANK, LANES - 2 * GLA_RANK), (0, 0)))])
    bg_p = _pad_heads(bg, 4, GLA_DK).reshape(2, 1, 512)
    lbl = lb_logits.reshape(DEPTH * 2, HG_K)
    src = jnp.arange(GLA_QK)
    spread = (jnp.arange(512)[None, :] == (src // GLA_DK * HEAD_W + src % GLA_DK)[:, None])
    row = lambda: pl.BlockSpec((ROW_TILE, D_MODEL), lambda i: (i, 0))
    return pl.pallas_call(
        functools.partial(_proj_kernel, layer),
        grid=(t // ROW_TILE,),
        in_specs=[row(), _const_spec((1, D_MODEL)), _layer_spec((D_MODEL, NP_COLS), layer),
                  _const_spec((2, LANES, 512)), _const_spec((2, 1, 512)),
                  _const_spec((DEPTH * 2, HG_K)), _const_spec((GLA_QK, 512))],
        out_specs=[row() for _ in range(9)],
        out_shape=[jax.ShapeDtypeStruct((t, D_MODEL), BF16) for _ in range(9)],
        scratch_shapes=[pltpu.VMEM((ROW_TILE, D_MODEL), BF16)],
        compiler_params=_params(1),
        name=f"proj_l{layer}",
    )(h, norm_w.reshape(1, D_MODEL), w_all, wg2_p, bg_p, lbl, spread.astype(BF16))


A_GROUPS = 3
A_WIDTH = A_GROUPS * HEAD_W


def _stage_a(rev, q_ref, k_ref, gh_ref, gl_ref, r0, tri, bufs, slot):
    qa, ka, qd, kd, qi, ks, et, _, _ = bufs
    rows = slice(r0, r0 + CHUNK)
    cum = _dot(tri, jnp.concatenate([gh_ref[0, rows, :], gl_ref[0, rows, :]], axis=0))
    q = q_ref[0, rows, :]
    k = k_ref[0, rows, :]
    factor = lambda x: jnp.exp2(x).astype(BF16)

    def put(ref, rows, group, val):
        for n in range(N_HEADS):
            c0 = n * A_WIDTH + group * HEAD_W
            ref[slot, rows, c0:c0 + HEAD_W] = val[:, n * HEAD_W:(n + 1) * HEAD_W]

    for group, (half, a) in enumerate(((2 * SUB, 0), (SUB, 0), (SUB, 2 * SUB))):
        lo, hi = slice(a, a + half), slice(a + half, a + 2 * half)
        if not rev:
            ref, k_rows, q_rows = cum[a + half - 1:a + half], lo, hi
        else:
            ref, q_rows, k_rows = cum[a + half:a + half + 1], lo, hi
        put(qa, q_rows, group, q[q_rows] * factor(cum[q_rows] - ref))
        put(ka, k_rows, group, k[k_rows] * factor(ref - cum[k_rows]))

    mids = []
    for a in range(0, CHUNK, SUB):
        m = 0.5 * (cum[a:a + 1] + cum[a + SUB - 1:a + SUB])
        mids.append(jnp.broadcast_to(m, (SUB, D_MODEL)))
    dd = cum - jnp.concatenate(mids, 0)
    qd[slot] = q * factor(jnp.clip(dd, -EXP2_CLAMP, EXP2_CLAMP))
    kd[slot] = k * factor(jnp.clip(-dd, -EXP2_CLAMP, EXP2_CLAMP))

    tot = cum[0:1] if rev else cum[CHUNK - 1:CHUNK]
    qi[slot] = q * factor(cum)
    ks[slot] = k * factor(tot - cum)
    et[slot] = jnp.exp2(tot)


def _stage_b(bufs, slot, v_ref, st_ref, r0, diag):
    qa, ka, qd, kd, qi, ks, et, ab, oi = bufs
    for n in range(N_HEADS):
        wide = slice(n * A_WIDTH, (n + 1) * A_WIDTH)
        head = slice(n * HEAD_W, (n + 1) * HEAD_W)
        scores = _dot_nt(qa[slot, :, wide], ka[slot, :, wide])
        scores = scores + jnp.where(diag, _dot_nt(qd[slot, :, head], kd[slot, :, head]), 0.0)
        ab[slot, n] = scores.astype(BF16)
        st = st_ref[n]
        oi[slot, :, head] = _dot(qi[slot, :, head], st.astype(BF16))
        decay = jnp.transpose(jnp.broadcast_to(et[slot, :, head], (HEAD_W, HEAD_W)))
        st_ref[n] = st * decay + _dot_tn(ks[slot, :, head], v_ref[0, r0:r0 + CHUNK, head])


def _stage_c(bufs, slot, v_ref, o_ref, r0):
    ab, oi = bufs[7], bufs[8]
    for n in range(N_HEADS):
        head = slice(n * HEAD_W, (n + 1) * HEAD_W)
        o = oi[slot, :, head] + _dot(ab[slot, n], v_ref[0, r0:r0 + CHUNK, head])
        o_ref[0, r0:r0 + CHUNK, head] = o.astype(o_ref.dtype)


def _scan_kernel(qf_ref, qb_ref, kf_ref, kb_ref, vf_ref, vb_ref, gfh_ref, gfl_ref, gbh_ref,
                 gbl_ref, of_ref, ob_ref, sf_ref, sb_ref, *buf_refs):
    n_chunks = MIX_TILE // CHUNK
    bufs_f, bufs_b = buf_refs[:len(buf_refs) // 2], buf_refs[len(buf_refs) // 2:]

    @pl.when(pl.program_id(1) == 0)
    def _():
        sf_ref[...] = jnp.zeros_like(sf_ref)
        sb_ref[...] = jnp.zeros_like(sb_ref)
        for bufs in (bufs_f, bufs_b):
            bufs[0][...] = jnp.zeros_like(bufs[0])
            bufs[1][...] = jnp.zeros_like(bufs[1])

    ri = lax.broadcasted_iota(jnp.int32, (CHUNK, CHUNK), 0)
    ci = lax.broadcasted_iota(jnp.int32, (CHUNK, CHUNK), 1)
    tri_f = (ci <= ri).astype(BF16)
    tri_b = (ci >= ri).astype(BF16)
    tri_f = jnp.concatenate([tri_f, tri_f], axis=1)
    tri_b = jnp.concatenate([tri_b, tri_b], axis=1)
    same16 = (ri // SUB) == (ci // SUB)
    diag_f = same16 & (ci <= ri)
    diag_b = same16 & (ci >= ri)
    row_f = lambda c: c * CHUNK
    row_b = lambda c: (n_chunks - 1 - c) * CHUNK

    def stage_a(c):
        _stage_a(False, qf_ref, kf_ref, gfh_ref, gfl_ref, row_f(c), tri_f, bufs_f, c % 2)
        _stage_a(True, qb_ref, kb_ref, gbh_ref, gbl_ref, row_b(c), tri_b, bufs_b, c % 2)

    stage_a(0)
    for c in range(n_chunks):
        if c + 1 < n_chunks:
            stage_a(c + 1)
        _stage_b(bufs_f, c % 2, vf_ref, sf_ref, row_f(c), diag_f)
        _stage_b(bufs_b, c % 2, vb_ref, sb_ref, row_b(c), diag_b)
        _stage_c(bufs_f, c % 2, vf_ref, of_ref, row_f(c))
        _stage_c(bufs_b, c % 2, vb_ref, ob_ref, row_b(c))


def _scan_bufs():
    wide = pltpu.VMEM((2, CHUNK, N_HEADS * A_WIDTH), BF16)
    narrow = pltpu.VMEM((2, CHUNK, D_MODEL), BF16)
    return [wide, wide, narrow, narrow, narrow, narrow,
            pltpu.VMEM((2, 1, D_MODEL), F32),
            pltpu.VMEM((2, N_HEADS, CHUNK, CHUNK), BF16),
            pltpu.VMEM((2, CHUNK, D_MODEL), F32)]


def _scan(layer, bsz, q, kf, kb, v, gfh, gfl, gbh, gbl):
    seq = q.shape[0] // bsz
    nt = seq // MIX_TILE
    to3 = lambda a: a.reshape(bsz, seq, D_MODEL)
    tile = lambda idx: pl.BlockSpec((1, MIX_TILE, D_MODEL), idx)
    fwd = lambda b, j: (b, j, 0)
    bwd = lambda b, j: (b, nt - 1 - j, 0)
    state = pltpu.VMEM((N_HEADS, HEAD_W, HEAD_W), F32)
    return pl.pallas_call(
        _scan_kernel,
        grid=(bsz, nt),
        in_specs=[tile(fwd), tile(bwd), tile(fwd), tile(bwd), tile(fwd), tile(bwd),
                  tile(fwd), tile(fwd), tile(bwd), tile(bwd)],
        out_specs=[tile(fwd), tile(bwd)],
        out_shape=[jax.ShapeDtypeStruct((bsz, seq, D_MODEL), BF16),
                   jax.ShapeDtypeStruct((bsz, seq, D_MODEL), BF16)],
        scratch_shapes=[state, state] + _scan_bufs() + _scan_bufs(),
        compiler_params=_params(2),
        name=f"scan_l{layer}",
    )(to3(q), to3(q), to3(kf), to3(kb), to3(v), to3(v), to3(gfh), to3(gfl), to3(gbh), to3(gbl))


def _post_math(of_ref, ob_ref, gate_ref, ones_ref, onw_ref, wout_ref, h):
    o = of_ref[...].astype(F32) + ob_ref[...].astype(F32)
    sq = (o * o).astype(BF16)
    ms = jnp.concatenate([_dot(sq[:, n * HEAD_W:(n + 1) * HEAD_W], ones_ref[...])
                          for n in range(N_HEADS)], axis=1) * (1.0 / HEAD_W)
    y = o * lax.rsqrt(ms + HEAD_EPS) * onw_ref[...]
    g = gate_ref[...].astype(F32)
    y = y * (g * _sigmoid(g))
    return h + _dot(y.astype(BF16), wout_ref[0])


def _post_inputs(layer, o_f, o_b, gates, gla_onorm, hg_onorm, w_out_all):
    onw = jnp.concatenate([jnp.tile(gla_onorm, 4), jnp.tile(hg_onorm, 4)]).reshape(1, D_MODEL)
    row = lambda: pl.BlockSpec((ROW_TILE, D_MODEL), lambda i: (i, 0))
    specs = [row(), row(), row(), _const_spec((HEAD_W, HEAD_W)), _const_spec((1, D_MODEL)),
             _layer_spec((D_MODEL, D_MODEL), layer)]
    return (o_f, o_b, gates, jnp.ones((HEAD_W, HEAD_W), BF16), onw, w_out_all), specs


def _ple_tail(h2, p, nple, wpp_ref, wpg_ref, nfinal, final):
    gate = _sigmoid(_dot(_rmsnorm(h2, nple).astype(BF16), wpg_ref[0]))
    h3 = h2 + _dot(p.astype(BF16), wpp_ref[0]) * gate
    if final:
        h3 = _rmsnorm(h3, nfinal)
    return h3


def _dense_kernel(final, of_ref, ob_ref, gate_ref, ones_ref, onw_ref, wout_ref, h_ref, p_ref,
                  nffn_ref, wg_ref, wu_ref, wd_ref, nple_ref, wpp_ref, wpg_ref, nfin_ref, out_ref):
    h1 = _post_math(of_ref, ob_ref, gate_ref, ones_ref, onw_ref, wout_ref, h_ref[...])
    v = _rmsnorm(h1, nffn_ref[...]).astype(BF16)
    acc = jnp.zeros_like(h1)
    for c in range(0, D_FF_DENSE, FF_CHUNK):
        a = _dot(v, wg_ref[0, :, c:c + FF_CHUNK])
        b = _dot(v, wu_ref[0, :, c:c + FF_CHUNK])
        acc = acc + _dot((a * _sigmoid(a) * b).astype(BF16), wd_ref[0, c:c + FF_CHUNK, :])
    out_ref[...] = _ple_tail(h1 + acc, p_ref[0], nple_ref[...], wpp_ref, wpg_ref,
                             nfin_ref[...], final)


def _dense_ffn(layer, final, post, h, p, norm_ffn, wg, wu, wd, norm_ple, w_pp, w_pg, norm_final):
    t = h.shape[0]
    vec = lambda w: w.reshape(1, D_MODEL)
    j = layer // 2
    return pl.pallas_call(
        functools.partial(_dense_kernel, final),
        grid=(t // ROW_TILE,),
        in_specs=post[1] + [
            pl.BlockSpec((ROW_TILE, D_MODEL), lambda i: (i, 0)),
            pl.BlockSpec((1, ROW_TILE, PLE_DIM), lambda i: (layer, i, 0)),
            _const_spec((1, D_MODEL)),
            _layer_spec((D_MODEL, D_FF_DENSE), j), _layer_spec((D_MODEL, D_FF_DENSE), j),
            _layer_spec((D_FF_DENSE, D_MODEL), j),
            _const_spec((1, D_MODEL)),
            _layer_spec((PLE_DIM, D_MODEL), layer), _layer_spec((D_MODEL, D_MODEL), layer),
            _const_spec((1, D_MODEL))],
        out_specs=pl.BlockSpec((ROW_TILE, D_MODEL), lambda i: (i, 0)),
        out_shape=jax.ShapeDtypeStruct((t, D_MODEL), F32),
        compiler_params=_params(1),
        name=f"dense_l{layer}",
    )(*post[0], h, p, vec(norm_ffn), wg, wu, wd, vec(norm_ple), w_pp, w_pg, vec(norm_final))


META_G1, META_G2, META_E1, META_E2, META_R1, META_R2 = range(6)
HALF = D_MODEL // 2
HI_MASK = 0xFFFF0000


def _pack_rows(x):
    bits = pltpu.bitcast(x.astype(BF16).astype(F32), jnp.uint32)
    return (bits[:, :HALF] >> 16) | (bits[:, HALF:] & jnp.uint32(HI_MASK))


def _unpack_rows(w):
    lo = pltpu.bitcast(w << 16, F32)
    hi = pltpu.bitcast(w & jnp.uint32(HI_MASK), F32)
    return jnp.concatenate([lo, hi], axis=1)


def _router_kernel(of_ref, ob_ref, gate_ref, ones_ref, onw_ref, wout_ref, h_ref, nffn_ref, wr_ref,
                   tri_ref, h1_ref, xp_ref, meta_ref, cnt_ref, base_ref):
    @pl.when(pl.program_id(0) == 0)
    def _():
        base_ref[...] = jnp.zeros_like(base_ref)

    h1 = _post_math(of_ref, ob_ref, gate_ref, ones_ref, onw_ref, wout_ref, h_ref[...])
    h1_ref[...] = h1
    vf = _rmsnorm(h1, nffn_ref[...])
    xp_ref[...] = _pack_rows(vf)
    v_hi = vf.astype(BF16)
    v_lo = (vf - v_hi.astype(F32)).astype(BF16)
    w = wr_ref[...]
    w_hi = w.astype(BF16)
    w_lo = (w - w_hi.astype(F32)).astype(BF16)
    logits = _dot(v_hi, w_hi) + _dot(v_lo, w_hi) + _dot(v_hi, w_lo)
    lane = lax.broadcasted_iota(jnp.int32, logits.shape, 1)
    neg = jnp.float32(-jnp.inf)
    lg = jnp.where(lane < N_EXPERTS, logits, neg)
    m1 = jnp.max(lg, axis=-1, keepdims=True)
    i1 = jnp.min(jnp.where(lg == m1, lane, LANES), axis=-1, keepdims=True)
    first = lane == i1
    lg2 = jnp.where(first, neg, lg)
    m2 = jnp.max(lg2, axis=-1, keepdims=True)
    i2 = jnp.min(jnp.where(lg2 == m2, lane, LANES), axis=-1, keepdims=True)
    second = lane == i2
    e = jnp.exp(m2 - m1)
    g1 = 1.0 / (1.0 + e)
    g2 = e * g1
    cnt = jnp.where(first | second, 1.0, 0.0)
    rank = base_ref[...] + _dot(tri_ref[...], cnt.astype(BF16))
    r1 = jnp.sum(jnp.where(first, rank, 0.0), axis=-1, keepdims=True)
    r2 = jnp.sum(jnp.where(second, rank, 0.0), axis=-1, keepdims=True)
    base_ref[...] = base_ref[...] + jnp.sum(cnt, axis=0, keepdims=True)
    cnt_ref[...] = base_ref[...]
    meta = jnp.zeros(logits.shape, F32)
    for col, val in ((META_G1, g1), (META_G2, g2), (META_E1, i1.astype(F32)),
                     (META_E2, i2.astype(F32)), (META_R1, r1), (META_R2, r2)):
        meta = jnp.where(lane == col, val, meta)
    meta_ref[...] = meta


def _router(layer, post, h, norm_ffn, w_router):
    t = h.shape[0]
    wr = jnp.pad(w_router, ((0, 0), (0, LANES - N_EXPERTS)))
    ids = jnp.arange(ROW_TILE)
    tri = (ids[None, :] < ids[:, None]).astype(BF16)
    return pl.pallas_call(
        _router_kernel,
        grid=(t // ROW_TILE,),
        in_specs=post[1] + [
            pl.BlockSpec((ROW_TILE, D_MODEL), lambda i: (i, 0)),
            _const_spec((1, D_MODEL)), _const_spec((D_MODEL, LANES)),
            _const_spec((ROW_TILE, ROW_TILE))],
        out_specs=[pl.BlockSpec((ROW_TILE, D_MODEL), lambda i: (i, 0)),
                   pl.BlockSpec((ROW_TILE, HALF), lambda i: (i, 0)),
                   pl.BlockSpec((ROW_TILE, LANES), lambda i: (i, 0)),
                   pl.BlockSpec((1, LANES), lambda i: (0, 0))],
        out_shape=[jax.ShapeDtypeStruct((t, D_MODEL), F32),
                   jax.ShapeDtypeStruct((t, HALF), jnp.uint32),
                   jax.ShapeDtypeStruct((t, LANES), F32),
                   jax.ShapeDtypeStruct((1, LANES), F32)],
        scratch_shapes=[pltpu.VMEM((1, LANES), F32)],
        compiler_params=_params(1),
        name=f"router_l{layer}",
    )(*post[0], h, norm_ffn.reshape(1, D_MODEL), wr, tri)


SC_CORES, SC_SUBCORES = 2, 16
SC_WORKERS = SC_CORES * SC_SUBCORES
SC_WINDOW = 64


def _sc_gather(table, idx):
    n_out, d = idx.shape[0], table.shape[1]
    per_worker = n_out // SC_WORKERS
    n_win = per_worker // SC_WINDOW
    assert n_win * SC_WINDOW * SC_WORKERS == n_out and n_win % 2 == 0
    mesh = plsc.VectorSubcoreMesh(core_axis_name="c", subcore_axis_name="s")

    def body(table_hbm, idx_hbm, out_hbm, idx_a, idx_b, rows_a, rows_b, sem_a, sem_b):
        wid = lax.axis_index("s") * SC_CORES + lax.axis_index("c")

        def rows_of(w):
            return pl.ds(pl.multiple_of(wid * per_worker + w * SC_WINDOW, SC_WINDOW), SC_WINDOW)

        def start(w, idx_v, rows_v, sem):
            pltpu.sync_copy(idx_hbm.at[rows_of(w)], idx_v)
            pltpu.async_copy(table_hbm.at[idx_v], rows_v, sem)

        def finish(w, idx_v, rows_v, sem):
            pltpu.make_async_copy(table_hbm.at[idx_v], rows_v, sem).wait()
            pltpu.sync_copy(rows_v, out_hbm.at[rows_of(w)])

        start(0, idx_a, rows_a, sem_a)

        @pl.loop(0, n_win, step=2)
        def _(w):
            start(w + 1, idx_b, rows_b, sem_b)
            finish(w, idx_a, rows_a, sem_a)

            @pl.when(w + 2 < n_win)
            def _():
                start(w + 2, idx_a, rows_a, sem_a)

            finish(w + 1, idx_b, rows_b, sem_b)

    return pl.kernel(
        body, mesh=mesh,
        out_type=jax.ShapeDtypeStruct((n_out, d), table.dtype),
        scratch_types=[pltpu.VMEM((SC_WINDOW,), jnp.int32), pltpu.VMEM((SC_WINDOW,), jnp.int32),
                       pltpu.VMEM((SC_WINDOW, d), table.dtype),
                       pltpu.VMEM((SC_WINDOW, d), table.dtype),
                       pltpu.SemaphoreType.DMA, pltpu.SemaphoreType.DMA],
    )(table, idx)


def _sc_scatter2(x, pos_a, pos_b, n_out):
    t, d = x.shape
    per_worker = t // SC_WORKERS
    n_win = per_worker // SC_WINDOW
    assert n_win * SC_WINDOW * SC_WORKERS == t and n_win % 2 == 0
    mesh = plsc.VectorSubcoreMesh(core_axis_name="c", subcore_axis_name="s")

    def body(x_hbm, pa_hbm, pb_hbm, out_hbm, ia0, ib0, ia1, ib1, rows0, rows1, sem0, sem1):
        wid = lax.axis_index("s") * SC_CORES + lax.axis_index("c")

        def rows_of(w):
            return pl.ds(pl.multiple_of(wid * per_worker + w * SC_WINDOW, SC_WINDOW), SC_WINDOW)

        def start(w, ia, ib, rows_v, sem):
            pltpu.sync_copy(x_hbm.at[rows_of(w)], rows_v)
            pltpu.sync_copy(pa_hbm.at[rows_of(w)], ia)
            pltpu.sync_copy(pb_hbm.at[rows_of(w)], ib)
            pltpu.async_copy(rows_v, out_hbm.at[ia], sem)
            pltpu.async_copy(rows_v, out_hbm.at[ib], sem)

        def finish(ia, ib, rows_v, sem):
            pltpu.make_async_copy(rows_v, out_hbm.at[ia], sem).wait()
            pltpu.make_async_copy(rows_v, out_hbm.at[ib], sem).wait()

        start(0, ia0, ib0, rows0, sem0)

        @pl.loop(0, n_win, step=2)
        def _(w):
            start(w + 1, ia1, ib1, rows1, sem1)
            finish(ia0, ib0, rows0, sem0)

            @pl.when(w + 2 < n_win)
            def _():
                start(w + 2, ia0, ib0, rows0, sem0)

            finish(ia1, ib1, rows1, sem1)

    idx = lambda: pltpu.VMEM((SC_WINDOW,), jnp.int32)
    return pl.kernel(
        body, mesh=mesh,
        out_type=jax.ShapeDtypeStruct((n_out, d), x.dtype),
        scratch_types=[idx(), idx(), idx(), idx(),
                       pltpu.VMEM((SC_WINDOW, d), x.dtype), pltpu.VMEM((SC_WINDOW, d), x.dtype),
                       pltpu.SemaphoreType.DMA, pltpu.SemaphoreType.DMA],
    )(x, pos_a, pos_b)


EXP_ROWS = 1024


def _experts_kernel(be_ref, na_ref, nv_ref, xs_ref, wg_ref, wu_ref, wd_ref, ys_ref, x_ref,
                    acc_ref):
    b = pl.program_id(0)
    f = pl.program_id(1)

    @pl.when(b < na_ref[0])
    def _():
        @pl.when(f == 0)
        def _():
            row = lax.broadcasted_iota(jnp.int32, (EXP_ROWS, D_MODEL), 0)
            x = jnp.where(row < nv_ref[b], _unpack_rows(xs_ref[...]), 0.0)
            x_ref[...] = x.astype(BF16)
            acc_ref[...] = jnp.zeros_like(acc_ref)

        x = x_ref[...]
        halves = [slice(c, c + EXP_FF_CHUNK // 2) for c in (0, EXP_FF_CHUNK // 2)]
        gate_up = [(_dot(x, wg_ref[0, 0, :, cols].astype(BF16)),
                    _dot(x, wu_ref[0, 0, :, cols].astype(BF16))) for cols in halves]
        down = acc_ref[...]
        for (a, u), cols in zip(gate_up, halves):
            down = down + _dot((a * _sigmoid(a) * u).astype(BF16),
                               wd_ref[0, 0, cols, :].astype(BF16))
        acc_ref[...] = down

        @pl.when(f == pl.num_programs(1) - 1)
        def _():
            ys_ref[...] = _pack_rows(acc_ref[...])


def _experts(layer, xs, blk_expert, n_active, blk_valid, wg, wu, wd):
    moe_idx = layer // 2
    rows = xs.shape[0]
    nf = D_FF_EXPERT // EXP_FF_CHUNK
    fsel = lambda b, f, na: jnp.where(b < na[0], f, nf - 1)
    grid_spec = pltpu.PrefetchScalarGridSpec(
        num_scalar_prefetch=3,
        grid=(rows // EXP_ROWS, nf),
        in_specs=[pl.BlockSpec((EXP_ROWS, HALF), lambda b, f, be, na, nv: (b, 0)),
                  pl.BlockSpec((1, 1, D_MODEL, EXP_FF_CHUNK),
                               lambda b, f, be, na, nv: (moe_idx, be[b], 0, fsel(b, f, na))),
                  pl.BlockSpec((1, 1, D_MODEL, EXP_FF_CHUNK),
                               lambda b, f, be, na, nv: (moe_idx, be[b], 0, fsel(b, f, na))),
                  pl.BlockSpec((1, 1, EXP_FF_CHUNK, D_MODEL),
                               lambda b, f, be, na, nv: (moe_idx, be[b], fsel(b, f, na), 0))],
        out_specs=pl.BlockSpec((EXP_ROWS, HALF), lambda b, f, be, na, nv: (b, 0)),
        scratch_shapes=[pltpu.VMEM((EXP_ROWS, D_MODEL), BF16),
                        pltpu.VMEM((EXP_ROWS, D_MODEL), F32)])
    return pl.pallas_call(
        _experts_kernel,
        grid_spec=grid_spec,
        out_shape=jax.ShapeDtypeStruct((rows, HALF), jnp.uint32),
        compiler_params=_params(2),
        name=f"experts_l{layer}",
    )(blk_expert, n_active, blk_valid, xs, wg, wu, wd)


def _combine_kernel(final, h_ref, y1_ref, y2_ref, meta_ref, p_ref, nple_ref, wpp_ref, wpg_ref,
                    nfin_ref, out_ref):
    meta = meta_ref[...]
    g1 = meta[:, META_G1:META_G1 + 1]
    g2 = meta[:, META_G2:META_G2 + 1]
    h2 = h_ref[...] + g1 * _unpack_rows(y1_ref[...]) + g2 * _unpack_rows(y2_ref[...])
    out_ref[...] = _ple_tail(h2, p_ref[0], nple_ref[...], wpp_ref, wpg_ref, nfin_ref[...], final)


def _combine(layer, final, h, yg, meta, p, norm_ple, w_pp, w_pg, norm_final):
    t = h.shape[0]
    nt = t // ROW_TILE
    vec = lambda w: w.reshape(1, D_MODEL)
    return pl.pallas_call(
        functools.partial(_combine_kernel, final),
        grid=(nt,),
        in_specs=[pl.BlockSpec((ROW_TILE, D_MODEL), lambda i: (i, 0)),
                  pl.BlockSpec((ROW_TILE, HALF), lambda i: (i, 0)),
                  pl.BlockSpec((ROW_TILE, HALF), lambda i: (i + nt, 0)),
                  pl.BlockSpec((ROW_TILE, LANES), lambda i: (i, 0)),
                  pl.BlockSpec((1, ROW_TILE, PLE_DIM), lambda i: (layer, i, 0)),
                  _const_spec((1, D_MODEL)),
                  _layer_spec((PLE_DIM, D_MODEL), layer), _layer_spec((D_MODEL, D_MODEL), layer),
                  _const_spec((1, D_MODEL))],
        out_specs=pl.BlockSpec((ROW_TILE, D_MODEL), lambda i: (i, 0)),
        out_shape=jax.ShapeDtypeStruct((t, D_MODEL), F32),
        compiler_params=_params(1),
        name=f"combine_l{layer}",
    )(h, yg, yg, meta, p, vec(norm_ple), w_pp, w_pg, vec(norm_final))


def _moe(layer, final, post, h, p, norm_ffn, w_router, wg, wu, wd, norm_ple, w_pp, w_pg,
         norm_final):
    t = h.shape[0]
    h1, xp, meta, counts = _router(layer, post, h, norm_ffn, w_router)
    cnt = counts[0, :N_EXPERTS].astype(jnp.int32)
    padded = ((cnt + EXP_ROWS - 1) // EXP_ROWS) * EXP_ROWS
    ends = jnp.cumsum(padded)
    offs = ends - padded
    e1 = meta[:, META_E1].astype(jnp.int32)
    e2 = meta[:, META_E2].astype(jnp.int32)
    pos1 = offs[e1] + meta[:, META_R1].astype(jnp.int32)
    pos2 = offs[e2] + meta[:, META_R2].astype(jnp.int32)
    rows = 2 * t + N_EXPERTS * EXP_ROWS
    blk_start = jnp.arange(rows // EXP_ROWS, dtype=jnp.int32) * EXP_ROWS
    blk_expert = jnp.minimum(
        jnp.sum(blk_start[:, None] >= ends[None, :], axis=1), N_EXPERTS - 1).astype(jnp.int32)
    n_active = (ends[-1:] // EXP_ROWS).astype(jnp.int32)
    blk_valid = jnp.clip((offs + cnt)[blk_expert] - blk_start, 0, EXP_ROWS).astype(jnp.int32)

    xs = _sc_scatter2(xp, pos1, pos2, rows)
    ys = _experts(layer, xs, blk_expert, n_active, blk_valid, wg, wu, wd)
    yg = _sc_gather(ys, jnp.concatenate([pos1, pos2]))
    return _combine(layer, final, h1, yg, meta, p, norm_ple, w_pp, w_pg, norm_final)


def kernel(x, p, w_in, gla_wg2, gla_bg, hg_lb_logits, gla_onorm, hg_onorm, w_out, norm_mix,
           norm_ffn, w_dense_gate, w_dense_up, w_dense_down, w_router, w_exp_gate, w_exp_up,
           w_exp_down, w_ple_proj, w_ple_gate, norm_ple, norm_final):
    bsz, seq, _ = x.shape
    t = bsz * seq
    w_proj = _proj_weights(w_in)
    w_out_b, w_pp, w_pg = w_out.astype(BF16), w_ple_proj.astype(BF16), w_ple_gate.astype(BF16)
    w_dg, w_du, w_dd = (w.astype(BF16) for w in (w_dense_gate, w_dense_up, w_dense_down))
    p_all = p.reshape(DEPTH, t, PLE_DIM)
    h = x.reshape(t, D_MODEL)
    for i in range(DEPTH):
        final = i == DEPTH - 1
        q, kf, kb, v, gates, gfh, gfl, gbh, gbl = _proj(
            i, h, norm_mix[i], w_proj, gla_wg2[i], gla_bg[i], hg_lb_logits)
        o_f, o_b = _scan(i, bsz, q, kf, kb, v, gfh, gfl, gbh, gbl)
        post = _post_inputs(i, o_f.reshape(t, D_MODEL), o_b.reshape(t, D_MODEL), gates,
                            gla_onorm[i], hg_onorm[i], w_out_b)
        if i % 2 == 0:
            h = _dense_ffn(i, final, post, h, p_all, norm_ffn[i], w_dg, w_du, w_dd, norm_ple[i],
                           w_pp, w_pg, norm_final)
        else:
            h = _moe(i, final, post, h, p_all, norm_ffn[i], w_router[i // 2], w_exp_gate, w_exp_up,
                     w_exp_down, norm_ple[i], w_pp, w_pg, norm_final)
    return h.reshape(bsz, seq, D_MODEL)
```

```python
import functools

import jax
import jax.numpy as jnp
from jax import lax
from jax.experimental import pallas as pl
from jax.experimental.pallas import tpu as pltpu
from jax.experimental.pallas import tpu_sc as plsc

F32 = jnp.float32
BF16 = jnp.bfloat16

D_MODEL = 1024
DEPTH = 4
N_HEADS = 8
HEAD_W = 128
GLA_DK = 64
GLA_RANK = 16
GLA_GATE_NORM = 16.0
HG_DK = 128
HG_K = 512
D_FF_DENSE = 2816
N_EXPERTS = 8
D_FF_EXPERT = 3584
PLE_DIM = 256
EPS = 1e-6
HEAD_EPS = 1e-5
F_MIN = 1e-6

CHUNK = 64
SUB = 16
EXP2_CLAMP = 86.0
LANES = 128
VMEM_LIMIT = 56 * 1024 * 1024

MIX_TILE = 512
ROW_TILE = 512
FF_CHUNK = 256
EXP_FF_CHUNK = 512

COL_HQ = 0
COL_HF = 512
COL_GLR = 1536
COL_GQ = 1664
COL_GK = 1920
COL_V = 2176
COL_GATE = 3200
NP_COLS = 4224
GLA_QK = 4 * GLA_DK
LOG2E = 1.4426950408889634


def _dot(a, b):
    return jnp.dot(a, b, preferred_element_type=F32)


def _dot_nt(a, b):
    return lax.dot_general(a, b, (((1,), (1,)), ((), ())), preferred_element_type=F32)


def _dot_tn(a, b):
    return lax.dot_general(a, b, (((0,), (0,)), ((), ())), preferred_element_type=F32)


def _sigmoid(x):
    return 1.0 / (1.0 + jnp.exp(-x))


def _rmsnorm(x, w):
    ms = jnp.mean(x * x, axis=-1, keepdims=True)
    return x * lax.rsqrt(ms + EPS) * w


def _const_spec(shape):
    nd = len(shape)
    return pl.BlockSpec(shape, lambda *_: (0,) * nd, pipeline_mode=pl.Buffered(1))


def _layer_spec(shape, layer):
    nd = len(shape)
    return pl.BlockSpec((1,) + tuple(shape), lambda *_: (layer,) + (0,) * nd,
                        pipeline_mode=pl.Buffered(1))


def _params(n_grid):
    return pltpu.CompilerParams(
        dimension_semantics=("arbitrary",) * n_grid, vmem_limit_bytes=VMEM_LIMIT)


PROJ_CHUNK = 256


def _proj_kernel(layer, h_ref, nw_ref, w_ref, wg2_ref, bg_ref, lbl_ref, spread_ref,
                 q_ref, kf_ref, kb_ref, v_ref, gate_ref, gfh_ref, gfl_ref, gbh_ref, gbl_ref,
                 u_ref):
    u_ref[...] = _rmsnorm(h_ref[...], nw_ref[...]).astype(BF16)
    proj = lambda c0, width: _dot(u_ref[...], w_ref[0, :, c0:c0 + width])
    chunks = lambda width: range(0, width, PROJ_CHUNK)

    def split(hi_ref, lo_ref, c0, g):
        hi = g.astype(BF16)
        hi_ref[:, c0:c0 + PROJ_CHUNK] = hi
        lo_ref[:, c0:c0 + PROJ_CHUNK] = (g - hi.astype(F32)).astype(BF16)

    dirs = ((kf_ref, gfh_ref, gfl_ref), (kb_ref, gbh_ref, gbl_ref))
    heavy, light = [], []

    glr = proj(COL_GLR, LANES).astype(BF16)

    def gla_decay(direction, c):
        _, hi_ref, lo_ref = dirs[direction]
        x = _dot(glr, wg2_ref[direction, :, c:c + PROJ_CHUNK].astype(BF16))
        x = x + bg_ref[direction, :, c:c + PROJ_CHUNK]
        log_sig = jnp.minimum(x, 0.0) - jnp.log(1.0 + jnp.exp(-jnp.abs(x)))
        split(hi_ref, lo_ref, c, log_sig * (LOG2E / GLA_GATE_NORM))

    def hgrn_forget(direction, c):
        k_ref, hi_ref, lo_ref = dirs[direction]
        rows = [lbl_ref[2 * d + direction:2 * d + direction + 1, c:c + PROJ_CHUNK]
                for d in range(DEPTH)]
        mx = functools.reduce(jnp.maximum, rows)
        ex = [jnp.exp(r - mx) for r in rows]
        lb = sum(ex[1:layer + 1], jnp.zeros_like(mx)) / sum(ex)
        z = proj(COL_HF + direction * HG_K + c, PROJ_CHUNK)
        e = jnp.exp(-jnp.abs(z))
        s_big = 1.0 / (1.0 + e)
        s_small = e * s_big
        sig = jnp.where(z >= 0, s_big, s_small)
        nsig = jnp.where(z >= 0, s_small, s_big)
        f = lb + (1.0 - lb) * sig
        split(hi_ref, lo_ref, 512 + c, jnp.log(jnp.maximum(f, F_MIN)) * LOG2E)
        k_ref[:, 512 + c:512 + c + PROJ_CHUNK] = ((1.0 - lb) * nsig).astype(BF16)

    def hgrn_query(c):
        hq = proj(COL_HQ + c, PROJ_CHUNK)
        q_ref[:, 512 + c:512 + c + PROJ_CHUNK] = (hq * _sigmoid(hq) * (HG_DK ** -0.5)).astype(BF16)

    def gla_query():
        gq = (proj(COL_GQ, GLA_QK) * (GLA_DK ** -0.5)).astype(BF16)
        q_ref[:, 0:512] = _dot(gq, spread_ref[...]).astype(BF16)

    def gla_key():
        gk = _dot(proj(COL_GK, GLA_QK).astype(BF16), spread_ref[...]).astype(BF16)
        kf_ref[:, 0:512] = gk
        kb_ref[:, 0:512] = gk

    def plain(out_ref, col, c):
        out_ref[:, c:c + PROJ_CHUNK] = proj(col + c, PROJ_CHUNK).astype(BF16)

    for direction in range(2):
        heavy += [functools.partial(hgrn_forget, direction, c) for c in chunks(HG_K)]
        heavy += [functools.partial(gla_decay, direction, c) for c in chunks(512)]
    heavy += [functools.partial(hgrn_query, c) for c in chunks(512)]
    light += [gla_query, gla_key]
    light += [functools.partial(plain, v_ref, COL_V, c) for c in chunks(D_MODEL)]
    light += [functools.partial(plain, gate_ref, COL_GATE, c) for c in chunks(D_MODEL)]
    for i in range(max(len(heavy), len(light))):
        for chunk in heavy[i:i + 1] + light[i:i + 1]:
            chunk()


def _pad_heads(w, n_heads, width):
    lead = w.shape[:-1]
    w = w.reshape(lead + (n_heads, width))
    w = jnp.pad(w, [(0, 0)] * len(lead) + [(0, 0), (0, HEAD_W - width)])
    return w.reshape(lead + (n_heads * HEAD_W,))


def _proj_weights(w_in):
    gq, gk, gv, gr, glr_f, glr_b, hq, hf_f, hf_b, hv, hr = jnp.split(
        w_in, [256, 512, 1024, 1536, 1552, 1568, 2080, 2592, 3104, 3616], axis=-1)
    glr = jnp.pad(jnp.concatenate([glr_f, glr_b], -1), ((0, 0), (0, 0), (0, LANES - 2 * GLA_RANK)))
    return jnp.concatenate([hq, hf_f, hf_b, glr, gq, gk, gv, hv, gr, hr], -1).astype(BF16)


def _proj(layer, h, norm_w, w_all, wg2, bg, lb_logits):
    t = h.shape[0]
    wg2_h = _pad_heads(wg2, 4, GLA_DK)
    wg2_p = jnp.stack([jnp.pad(wg2_h[0], ((0, LANES - GLA_RANK), (0, 0))),
                       jnp.pad(wg2_h[1], ((GLA_RANK, LANES - 2 * GLA_RANK), (0, 0)))])
    bg_p = _pad_heads(bg, 4, GLA_DK).reshape(2, 1, 512)
    lbl = lb_logits.reshape(DEPTH * 2, HG_K)
    src = jnp.arange(GLA_QK)
    spread = (jnp.arange(512)[None, :] == (src // GLA_DK * HEAD_W + src % GLA_DK)[:, None])
    row = lambda: pl.BlockSpec((ROW_TILE, D_MODEL), lambda i: (i, 0))
    return pl.pallas_call(
        functools.partial(_proj_kernel, layer),
        grid=(t // ROW_TILE,),
        in_specs=[row(), _const_spec((1, D_MODEL)), _layer_spec((D_MODEL, NP_COLS), layer),
                  _const_spec((2, LANES, 512)), _const_spec((2, 1, 512)),
                  _const_spec((DEPTH * 2, HG_K)), _const_spec((GLA_QK, 512))],
        out_specs=[row() for _ in range(9)],
        out_shape=[jax.ShapeDtypeStruct((t, D_MODEL), BF16) for _ in range(9)],
        scratch_shapes=[pltpu.VMEM((ROW_TILE, D_MODEL), BF16)],
        compiler_params=_params(1),
        name=f"proj_l{layer}",
    )(h, norm_w.reshape(1, D_MODEL), w_all, wg2_p, bg_p, lbl, spread.astype(BF16))


A_GROUPS = 3
A_WIDTH = A_GROUPS * HEAD_W


def _stage_a(rev, q_ref, k_ref, gh_ref, gl_ref, r0, tri, bufs, slot):
    qa, ka, qd, kd, qi, ks, et, _, _ = bufs
    rows = slice(r0, r0 + CHUNK)
    cum = _dot(tri, jnp.concatenate([gh_ref[0, rows, :], gl_ref[0, rows, :]], axis=0))
    q = q_ref[0, rows, :]
    k = k_ref[0, rows, :]
    factor = lambda x: jnp.exp2(x).astype(BF16)

    def put(ref, rows, group, val):
        for n in range(N_HEADS):
            c0 = n * A_WIDTH + group * HEAD_W
            ref[slot, rows, c0:c0 + HEAD_W] = val[:, n * HEAD_W:(n + 1) * HEAD_W]

    for group, (half, a) in enumerate(((2 * SUB, 0), (SUB, 0), (SUB, 2 * SUB))):
        lo, hi = slice(a, a + half), slice(a + half, a + 2 * half)
        if not rev:
            ref, k_rows, q_rows = cum[a + half - 1:a + half], lo, hi
        else:
            ref, q_rows, k_rows = cum[a + half:a + half + 1], lo, hi
        put(qa, q_rows, group, q[q_rows] * factor(cum[q_rows] - ref))
        put(ka, k_rows, group, k[k_rows] * factor(ref - cum[k_rows]))

    mids = []
    for a in range(0, CHUNK, SUB):
        m = 0.5 * (cum[a:a + 1] + cum[a + SUB - 1:a + SUB])
        mids.append(jnp.broadcast_to(m, (SUB, D_MODEL)))
    dd = cum - jnp.concatenate(mids, 0)
    qd[slot] = q * factor(jnp.clip(dd, -EXP2_CLAMP, EXP2_CLAMP))
    kd[slot] = k * factor(jnp.clip(-dd, -EXP2_CLAMP, EXP2_CLAMP))

    tot = cum[0:1] if rev else cum[CHUNK - 1:CHUNK]
    qi[slot] = q * factor(cum)
    ks[slot] = k * factor(tot - cum)
    et[slot] = jnp.exp2(tot)


def _stage_b(bufs, slot, v_ref, st_ref, r0, diag):
    qa, ka, qd, kd, qi, ks, et, ab, oi = bufs
    for n in range(N_HEADS):
        wide = slice(n * A_WIDTH, (n + 1) * A_WIDTH)
        head = slice(n * HEAD_W, (n + 1) * HEAD_W)
        scores = _dot_nt(qa[slot, :, wide], ka[slot, :, wide])
        scores = scores + jnp.where(diag, _dot_nt(qd[slot, :, head], kd[slot, :, head]), 0.0)
        ab[slot, n] = scores.astype(BF16)
        st = st_ref[n]
        oi[slot, :, head] = _dot(qi[slot, :, head], st.astype(BF16))
        decay = jnp.transpose(jnp.broadcast_to(et[slot, :, head], (HEAD_W, HEAD_W)))
        st_ref[n] = st * decay + _dot_tn(ks[slot, :, head], v_ref[0, r0:r0 + CHUNK, head])


def _stage_c(bufs, slot, v_ref, o_ref, r0):
    ab, oi = bufs[7], bufs[8]
    for n in range(N_HEADS):
        head = slice(n * HEAD_W, (n + 1) * HEAD_W)
        o = oi[slot, :, head] + _dot(ab[slot, n], v_ref[0, r0:r0 + CHUNK, head])
        o_ref[0, r0:r0 + CHUNK, head] = o.astype(o_ref.dtype)


def _scan_kernel(qf_ref, qb_ref, kf_ref, kb_ref, vf_ref, vb_ref, gfh_ref, gfl_ref, gbh_ref,
                 gbl_ref, of_ref, ob_ref, sf_ref, sb_ref, *buf_refs):
    n_chunks = MIX_TILE // CHUNK
    bufs_f, bufs_b = buf_refs[:len(buf_refs) // 2], buf_refs[len(buf_refs) // 2:]

    @pl.when(pl.program_id(1) == 0)
    def _():
        sf_ref[...] = jnp.zeros_like(sf_ref)
        sb_ref[...] = jnp.zeros_like(sb_ref)
        for bufs in (bufs_f, bufs_b):
            bufs[0][...] = jnp.zeros_like(bufs[0])
            bufs[1][...] = jnp.zeros_like(bufs[1])

    ri = lax.broadcasted_iota(jnp.int32, (CHUNK, CHUNK), 0)
    ci = lax.broadcasted_iota(jnp.int32, (CHUNK, CHUNK), 1)
    tri_f = (ci <= ri).astype(BF16)
    tri_b = (ci >= ri).astype(BF16)
    tri_f = jnp.concatenate([tri_f, tri_f], axis=1)
    tri_b = jnp.concatenate([tri_b, tri_b], axis=1)
    same16 = (ri // SUB) == (ci // SUB)
    diag_f = same16 & (ci <= ri)
    diag_b = same16 & (ci >= ri)
    row_f = lambda c: c * CHUNK
    row_b = lambda c: (n_chunks - 1 - c) * CHUNK

    def stage_a(c):
        _stage_a(False, qf_ref, kf_ref, gfh_ref, gfl_ref, row_f(c), tri_f, bufs_f, c % 2)
        _stage_a(True, qb_ref, kb_ref, gbh_ref, gbl_ref, row_b(c), tri_b, bufs_b, c % 2)

    stage_a(0)
    for c in range(n_chunks):
        if c + 1 < n_chunks:
            stage_a(c + 1)
        _stage_b(bufs_f, c % 2, vf_ref, sf_ref, row_f(c), diag_f)
        _stage_b(bufs_b, c % 2, vb_ref, sb_ref, row_b(c), diag_b)
        _stage_c(bufs_f, c % 2, vf_ref, of_ref, row_f(c))
        _stage_c(bufs_b, c % 2, vb_ref, ob_ref, row_b(c))


def _scan_bufs():
    wide = pltpu.VMEM((2, CHUNK, N_HEADS * A_WIDTH), BF16)
    narrow = pltpu.VMEM((2, CHUNK, D_MODEL), BF16)
    return [wide, wide, narrow, narrow, narrow, narrow,
            pltpu.VMEM((2, 1, D_MODEL), F32),
            pltpu.VMEM((2, N_HEADS, CHUNK, CHUNK), BF16),
            pltpu.VMEM((2, CHUNK, D_MODEL), F32)]


def _scan(layer, bsz, q, kf, kb, v, gfh, gfl, gbh, gbl):
    seq = q.shape[0] // bsz
    nt = seq // MIX_TILE
    to3 = lambda a: a.reshape(bsz, seq, D_MODEL)
    tile = lambda idx: pl.BlockSpec((1, MIX_TILE, D_MODEL), idx)
    fwd = lambda b, j: (b, j, 0)
    bwd = lambda b, j: (b, nt - 1 - j, 0)
    state = pltpu.VMEM((N_HEADS, HEAD_W, HEAD_W), F32)
    return pl.pallas_call(
        _scan_kernel,
        grid=(bsz, nt),
        in_specs=[tile(fwd), tile(bwd), tile(fwd), tile(bwd), tile(fwd), tile(bwd),
                  tile(fwd), tile(fwd), tile(bwd), tile(bwd)],
        out_specs=[tile(fwd), tile(bwd)],
        out_shape=[jax.ShapeDtypeStruct((bsz, seq, D_MODEL), BF16),
                   jax.ShapeDtypeStruct((bsz, seq, D_MODEL), BF16)],
        scratch_shapes=[state, state] + _scan_bufs() + _scan_bufs(),
        compiler_params=_params(2),
        name=f"scan_l{layer}",
    )(to3(q), to3(q), to3(kf), to3(kb), to3(v), to3(v), to3(gfh), to3(gfl), to3(gbh), to3(gbl))


def _post_math(of_ref, ob_ref, gate_ref, ones_ref, onw_ref, wout_ref, h):
    o = of_ref[...].astype(F32) + ob_ref[...].astype(F32)
    sq = (o * o).astype(BF16)
    ms = jnp.concatenate([_dot(sq[:, n * HEAD_W:(n + 1) * HEAD_W], ones_ref[...])
                          for n in range(N_HEADS)], axis=1) * (1.0 / HEAD_W)
    y = o * lax.rsqrt(ms + HEAD_EPS) * onw_ref[...]
    g = gate_ref[...].astype(F32)
    y = y * (g * _sigmoid(g))
    return h + _dot(y.astype(BF16), wout_ref[0])


def _post_inputs(layer, o_f, o_b, gates, gla_onorm, hg_onorm, w_out_all):
    onw = jnp.concatenate([jnp.tile(gla_onorm, 4), jnp.tile(hg_onorm, 4)]).reshape(1, D_MODEL)
    row = lambda: pl.BlockSpec((ROW_TILE, D_MODEL), lambda i: (i, 0))
    specs = [row(), row(), row(), _const_spec((HEAD_W, HEAD_W)), _const_spec((1, D_MODEL)),
             _layer_spec((D_MODEL, D_MODEL), layer)]
    return (o_f, o_b, gates, jnp.ones((HEAD_W, HEAD_W), BF16), onw, w_out_all), specs


def _ple_tail(h2, p, nple, wpp_ref, wpg_ref, nfinal, final):
    gate = _sigmoid(_dot(_rmsnorm(h2, nple).astype(BF16), wpg_ref[0]))
    h3 = h2 + _dot(p.astype(BF16), wpp_ref[0]) * gate
    if final:
        h3 = _rmsnorm(h3, nfinal)
    return h3


def _dense_kernel(final, of_ref, ob_ref, gate_ref, ones_ref, onw_ref, wout_ref, h_ref, p_ref,
                  nffn_ref, wg_ref, wu_ref, wd_ref, nple_ref, wpp_ref, wpg_ref, nfin_ref, out_ref):
    h1 = _post_math(of_ref, ob_ref, gate_ref, ones_ref, onw_ref, wout_ref, h_ref[...])
    v = _rmsnorm(h1, nffn_ref[...]).astype(BF16)
    acc = jnp.zeros_like(h1)
    for c in range(0, D_FF_DENSE, FF_CHUNK):
        a = _dot(v, wg_ref[0, :, c:c + FF_CHUNK])
        b = _dot(v, wu_ref[0, :, c:c + FF_CHUNK])
        acc = acc + _dot((a * _sigmoid(a) * b).astype(BF16), wd_ref[0, c:c + FF_CHUNK, :])
    out_ref[...] = _ple_tail(h1 + acc, p_ref[0], nple_ref[...], wpp_ref, wpg_ref,
                             nfin_ref[...], final)


def _dense_ffn(layer, final, post, h, p, norm_ffn, wg, wu, wd, norm_ple, w_pp, w_pg, norm_final):
    t = h.shape[0]
    vec = lambda w: w.reshape(1, D_MODEL)
    j = layer // 2
    return pl.pallas_call(
        functools.partial(_dense_kernel, final),
        grid=(t // ROW_TILE,),
        in_specs=post[1] + [
            pl.BlockSpec((ROW_TILE, D_MODEL), lambda i: (i, 0)),
            pl.BlockSpec((1, ROW_TILE, PLE_DIM), lambda i: (layer, i, 0)),
            _const_spec((1, D_MODEL)),
            _layer_spec((D_MODEL, D_FF_DENSE), j), _layer_spec((D_MODEL, D_FF_DENSE), j),
            _layer_spec((D_FF_DENSE, D_MODEL), j),
            _const_spec((1, D_MODEL)),
            _layer_spec((PLE_DIM, D_MODEL), layer), _layer_spec((D_MODEL, D_MODEL), layer),
            _const_spec((1, D_MODEL))],
        out_specs=pl.BlockSpec((ROW_TILE, D_MODEL), lambda i: (i, 0)),
        out_shape=jax.ShapeDtypeStruct((t, D_MODEL), F32),
        compiler_params=_params(1),
        name=f"dense_l{layer}",
    )(*post[0], h, p, vec(norm_ffn), wg, wu, wd, vec(norm_ple), w_pp, w_pg, vec(norm_final))


META_G1, META_G2, META_E1, META_E2, META_R1, META_R2 = range(6)
HALF = D_MODEL // 2
HI_MASK = 0xFFFF0000


def _pack_rows(x):
    bits = pltpu.bitcast(x.astype(BF16).astype(F32), jnp.uint32)
    return (bits[:, :HALF] >> 16) | (bits[:, HALF:] & jnp.uint32(HI_MASK))


def _unpack_rows(w):
    lo = pltpu.bitcast(w << 16, F32)
    hi = pltpu.bitcast(w & jnp.uint32(HI_MASK), F32)
    return jnp.concatenate([lo, hi], axis=1)


def _router_kernel(of_ref, ob_ref, gate_ref, ones_ref, onw_ref, wout_ref, h_ref, nffn_ref, wr_ref,
                   tri_ref, h1_ref, xp_ref, meta_ref, cnt_ref, base_ref):
    @pl.when(pl.program_id(0) == 0)
    def _():
        base_ref[...] = jnp.zeros_like(base_ref)

    h1 = _post_math(of_ref, ob_ref, gate_ref, ones_ref, onw_ref, wout_ref, h_ref[...])
    h1_ref[...] = h1
    vf = _rmsnorm(h1, nffn_ref[...])
    xp_ref[...] = _pack_rows(vf)
    v_hi = vf.astype(BF16)
    v_lo = (vf - v_hi.astype(F32)).astype(BF16)
    w = wr_ref[...]
    w_hi = w.astype(BF16)
    w_lo = (w - w_hi.astype(F32)).astype(BF16)
    logits = _dot(v_hi, w_hi) + _dot(v_lo, w_hi) + _dot(v_hi, w_lo)
    lane = lax.broadcasted_iota(jnp.int32, logits.shape, 1)
    neg = jnp.float32(-jnp.inf)
    lg = jnp.where(lane < N_EXPERTS, logits, neg)
    m1 = jnp.max(lg, axis=-1, keepdims=True)
    i1 = jnp.min(jnp.where(lg == m1, lane, LANES), axis=-1, keepdims=True)
    first = lane == i1
    lg2 = jnp.where(first, neg, lg)
    m2 = jnp.max(lg2, axis=-1, keepdims=True)
    i2 = jnp.min(jnp.where(lg2 == m2, lane, LANES), axis=-1, keepdims=True)
    second = lane == i2
    e = jnp.exp(m2 - m1)
    g1 = 1.0 / (1.0 + e)
    g2 = e * g1
    cnt = jnp.where(first | second, 1.0, 0.0)
    rank = base_ref[...] + _dot(tri_ref[...], cnt.astype(BF16))
    r1 = jnp.sum(jnp.where(first, rank, 0.0), axis=-1, keepdims=True)
    r2 = jnp.sum(jnp.where(second, rank, 0.0), axis=-1, keepdims=True)
    base_ref[...] = base_ref[...] + jnp.sum(cnt, axis=0, keepdims=True)
    cnt_ref[...] = base_ref[...]
    meta = jnp.zeros(logits.shape, F32)
    for col, val in ((META_G1, g1), (META_G2, g2), (META_E1, i1.astype(F32)),
                     (META_E2, i2.astype(F32)), (META_R1, r1), (META_R2, r2)):
        meta = jnp.where(lane == col, val, meta)
    meta_ref[...] = meta


def _router(layer, post, h, norm_ffn, w_router):
    t = h.shape[0]
    wr = jnp.pad(w_router, ((0, 0), (0, LANES - N_EXPERTS)))
    ids = jnp.arange(ROW_TILE)
    tri = (ids[None, :] < ids[:, None]).astype(BF16)
    return pl.pallas_call(
        _router_kernel,
        grid=(t // ROW_TILE,),
        in_specs=post[1] + [
            pl.BlockSpec((ROW_TILE, D_MODEL), lambda i: (i, 0)),
            _const_spec((1, D_MODEL)), _const_spec((D_MODEL, LANES)),
            _const_spec((ROW_TILE, ROW_TILE))],
        out_specs=[pl.BlockSpec((ROW_TILE, D_MODEL), lambda i: (i, 0)),
                   pl.BlockSpec((ROW_TILE, HALF), lambda i: (i, 0)),
                   pl.BlockSpec((ROW_TILE, LANES), lambda i: (i, 0)),
                   pl.BlockSpec((1, LANES), lambda i: (0, 0))],
        out_shape=[jax.ShapeDtypeStruct((t, D_MODEL), F32),
                   jax.ShapeDtypeStruct((t, HALF), jnp.uint32),
                   jax.ShapeDtypeStruct((t, LANES), F32),
                   jax.ShapeDtypeStruct((1, LANES), F32)],
        scratch_shapes=[pltpu.VMEM((1, LANES), F32)],
        compiler_params=_params(1),
        name=f"router_l{layer}",
    )(*post[0], h, norm_ffn.reshape(1, D_MODEL), wr, tri)


SC_CORES, SC_SUBCORES = 2, 16
SC_WORKERS = SC_CORES * SC_SUBCORES
SC_WINDOW = 64


def _sc_gather(table, idx):
    n_out, d = idx.shape[0], table.shape[1]
    per_worker = n_out // SC_WORKERS
    n_win = per_worker // SC_WINDOW
    assert n_win * SC_WINDOW * SC_WORKERS == n_out and n_win % 2 == 0
    mesh = plsc.VectorSubcoreMesh(core_axis_name="c", subcore_axis_name="s")

    def body(table_hbm, idx_hbm, out_hbm, idx_a, idx_b, rows_a, rows_b, sem_a, sem_b):
        wid = lax.axis_index("s") * SC_CORES + lax.axis_index("c")

        def rows_of(w):
            return pl.ds(pl.multiple_of(wid * per_worker + w * SC_WINDOW, SC_WINDOW), SC_WINDOW)

        def start(w, idx_v, rows_v, sem):
            pltpu.sync_copy(idx_hbm.at[rows_of(w)], idx_v)
            pltpu.async_copy(table_hbm.at[idx_v], rows_v, sem)

        def finish(w, idx_v, rows_v, sem):
            pltpu.make_async_copy(table_hbm.at[idx_v], rows_v, sem).wait()
            pltpu.sync_copy(rows_v, out_hbm.at[rows_of(w)])

        start(0, idx_a, rows_a, sem_a)

        @pl.loop(0, n_win, step=2)
        def _(w):
            start(w + 1, idx_b, rows_b, sem_b)
            finish(w, idx_a, rows_a, sem_a)

            @pl.when(w + 2 < n_win)
            def _():
                start(w + 2, idx_a, rows_a, sem_a)

            finish(w + 1, idx_b, rows_b, sem_b)

    return pl.kernel(
        body, mesh=mesh,
        out_type=jax.ShapeDtypeStruct((n_out, d), table.dtype),
        scratch_types=[pltpu.VMEM((SC_WINDOW,), jnp.int32), pltpu.VMEM((SC_WINDOW,), jnp.int32),
                       pltpu.VMEM((SC_WINDOW, d), table.dtype),
                       pltpu.VMEM((SC_WINDOW, d), table.dtype),
                       pltpu.SemaphoreType.DMA, pltpu.SemaphoreType.DMA],
    )(table, idx)


def _sc_scatter2(x, pos_a, pos_b, n_out):
    t, d = x.shape
    per_worker = t // SC_WORKERS
    n_win = per_worker // SC_WINDOW
    assert n_win * SC_WINDOW * SC_WORKERS == t and n_win % 2 == 0
    mesh = plsc.VectorSubcoreMesh(core_axis_name="c", subcore_axis_name="s")

    def body(x_hbm, pa_hbm, pb_hbm, out_hbm, ia0, ib0, ia1, ib1, rows0, rows1, sem0, sem1):
        wid = lax.axis_index("s") * SC_CORES + lax.axis_index("c")

        def rows_of(w):
            return pl.ds(pl.multiple_of(wid * per_worker + w * SC_WINDOW, SC_WINDOW), SC_WINDOW)

        def start(w, ia, ib, rows_v, sem):
            pltpu.sync_copy(x_hbm.at[rows_of(w)], rows_v)
            pltpu.sync_copy(pa_hbm.at[rows_of(w)], ia)
            pltpu.sync_copy(pb_hbm.at[rows_of(w)], ib)
            pltpu.async_copy(rows_v, out_hbm.at[ia], sem)
            pltpu.async_copy(rows_v, out_hbm.at[ib], sem)

        def finish(ia, ib, rows_v, sem):
            pltpu.make_async_copy(rows_v, out_hbm.at[ia], sem).wait()
            pltpu.make_async_copy(rows_v, out_hbm.at[ib], sem).wait()

        start(0, ia0, ib0, rows0, sem0)

        @pl.loop(0, n_win, step=2)
        def _(w):
            start(w + 1, ia1, ib1, rows1, sem1)
            finish(ia0, ib0, rows0, sem0)

            @pl.when(w + 2 < n_win)
            def _():
                start(w + 2, ia0, ib0, rows0, sem0)

            finish(ia1, ib1, rows1, sem1)

    idx = lambda: pltpu.VMEM((SC_WINDOW,), jnp.int32)
    return pl.kernel(
        body, mesh=mesh,
        out_type=jax.ShapeDtypeStruct((n_out, d), x.dtype),
        scratch_types=[idx(), idx(), idx(), idx(),
                       pltpu.VMEM((SC_WINDOW, d), x.dtype), pltpu.VMEM((SC_WINDOW, d), x.dtype),
                       pltpu.SemaphoreType.DMA, pltpu.SemaphoreType.DMA],
    )(x, pos_a, pos_b)


EXP_ROWS = 1024


def _experts_kernel(be_ref, na_ref, nv_ref, xs_ref, wg_ref, wu_ref, wd_ref, ys_ref, x_ref,
                    acc_ref):
    b = pl.program_id(0)
    f = pl.program_id(1)

    @pl.when(b < na_ref[0])
    def _():
        @pl.when(f == 0)
        def _():
            row = lax.broadcasted_iota(jnp.int32, (EXP_ROWS, D_MODEL), 0)
            x = jnp.where(row < nv_ref[b], _unpack_rows(xs_ref[...]), 0.0)
            x_ref[...] = x.astype(BF16)
            acc_ref[...] = jnp.zeros_like(acc_ref)

        x = x_ref[...]
        halves = [slice(c, c + EXP_FF_CHUNK // 2) for c in (0, EXP_FF_CHUNK // 2)]
        gate_up = [(_dot(x, wg_ref[0, 0, :, cols].astype(BF16)),
                    _dot(x, wu_ref[0, 0, :, cols].astype(BF16))) for cols in halves]
        down = acc_ref[...]
        for (a, u), cols in zip(gate_up, halves):
            down = down + _dot((a * _sigmoid(a) * u).astype(BF16),
                               wd_ref[0, 0, cols, :].astype(BF16))
        acc_ref[...] = down

        @pl.when(f == pl.num_programs(1) - 1)
        def _():
            ys_ref[...] = _pack_rows(acc_ref[...])


def _experts(layer, xs, blk_expert, n_active, blk_valid, wg, wu, wd):
    moe_idx = layer // 2
    rows = xs.shape[0]
    nf = D_FF_EXPERT // EXP_FF_CHUNK
    fsel = lambda b, f, na: jnp.where(b < na[0], f, nf - 1)
    grid_spec = pltpu.PrefetchScalarGridSpec(
        num_scalar_prefetch=3,
        grid=(rows // EXP_ROWS, nf),
        in_specs=[pl.BlockSpec((EXP_ROWS, HALF), lambda b, f, be, na, nv: (b, 0)),
                  pl.BlockSpec((1, 1, D_MODEL, EXP_FF_CHUNK),
                               lambda b, f, be, na, nv: (moe_idx, be[b], 0, fsel(b, f, na))),
                  pl.BlockSpec((1, 1, D_MODEL, EXP_FF_CHUNK),
                               lambda b, f, be, na, nv: (moe_idx, be[b], 0, fsel(b, f, na))),
                  pl.BlockSpec((1, 1, EXP_FF_CHUNK, D_MODEL),
                               lambda b, f, be, na, nv: (moe_idx, be[b], fsel(b, f, na), 0))],
        out_specs=pl.BlockSpec((EXP_ROWS, HALF), lambda b, f, be, na, nv: (b, 0)),
        scratch_shapes=[pltpu.VMEM((EXP_ROWS, D_MODEL), BF16),
                        pltpu.VMEM((EXP_ROWS, D_MODEL), F32)])
    return pl.pallas_call(
        _experts_kernel,
        grid_spec=grid_spec,
        out_shape=jax.ShapeDtypeStruct((rows, HALF), jnp.uint32),
        compiler_params=_params(2),
        name=f"experts_l{layer}",
    )(blk_expert, n_active, blk_valid, xs, wg, wu, wd)


def _combine_kernel(final, h_ref, y1_ref, y2_ref, meta_ref, p_ref, nple_ref, wpp_ref, wpg_ref,
                    nfin_ref, out_ref):
    meta = meta_ref[...]
    g1 = meta[:, META_G1:META_G1 + 1]
    g2 = meta[:, META_G2:META_G2 + 1]
    h2 = h_ref[...] + g1 * _unpack_rows(y1_ref[...]) + g2 * _unpack_rows(y2_ref[...])
    out_ref[...] = _ple_tail(h2, p_ref[0], nple_ref[...], wpp_ref, wpg_ref, nfin_ref[...], final)


def _combine(layer, final, h, yg, meta, p, norm_ple, w_pp, w_pg, norm_final):
    t = h.shape[0]
    nt = t // ROW_TILE
    vec = lambda w: w.reshape(1, D_MODEL)
    return pl.pallas_call(
        functools.partial(_combine_kernel, final),
        grid=(nt,),
        in_specs=[pl.BlockSpec((ROW_TILE, D_MODEL), lambda i: (i, 0)),
                  pl.BlockSpec((ROW_TILE, HALF), lambda i: (i, 0)),
                  pl.BlockSpec((ROW_TILE, HALF), lambda i: (i + nt, 0)),
                  pl.BlockSpec((ROW_TILE, LANES), lambda i: (i, 0)),
                  pl.BlockSpec((1, ROW_TILE, PLE_DIM), lambda i: (layer, i, 0)),
                  _const_spec((1, D_MODEL)),
                  _layer_spec((PLE_DIM, D_MODEL), layer), _layer_spec((D_MODEL, D_MODEL), layer),
                  _const_spec((1, D_MODEL))],
        out_specs=pl.BlockSpec((ROW_TILE, D_MODEL), lambda i: (i, 0)),
        out_shape=jax.ShapeDtypeStruct((t, D_MODEL), F32),
        compiler_params=_params(1),
        name=f"combine_l{layer}",
    )(h, yg, yg, meta, p, vec(norm_ple), w_pp, w_pg, vec(norm_final))


def _moe(layer, final, post, h, p, norm_ffn, w_router, wg, wu, wd, norm_ple, w_pp, w_pg,
         norm_final):
    t = h.shape[0]
    h1, xp, meta, counts = _router(layer, post, h, norm_ffn, w_router)
    cnt = counts[0, :N_EXPERTS].astype(jnp.int32)
    padded = ((cnt + EXP_ROWS - 1) // EXP_ROWS) * EXP_ROWS
    ends = jnp.cumsum(padded)
    offs = ends - padded
    e1 = meta[:, META_E1].astype(jnp.int32)
    e2 = meta[:, META_E2].astype(jnp.int32)
    pos1 = offs[e1] + meta[:, META_R1].astype(jnp.int32)
    pos2 = offs[e2] + meta[:, META_R2].astype(jnp.int32)
    rows = 2 * t + N_EXPERTS * EXP_ROWS
    blk_start = jnp.arange(rows // EXP_ROWS, dtype=jnp.int32) * EXP_ROWS
    blk_expert = jnp.minimum(
        jnp.sum(blk_start[:, None] >= ends[None, :], axis=1), N_EXPERTS - 1).astype(jnp.int32)
    n_active = (ends[-1:] // EXP_ROWS).astype(jnp.int32)
    blk_valid = jnp.clip((offs + cnt)[blk_expert] - blk_start, 0, EXP_ROWS).astype(jnp.int32)

    xs = _sc_scatter2(xp, pos1, pos2, rows)
    ys = _experts(layer, xs, blk_expert, n_active, blk_valid, wg, wu, wd)
    yg = _sc_gather(ys, jnp.concatenate([pos1, pos2]))
    return _combine(layer, final, h1, yg, meta, p, norm_ple, w_pp, w_pg, norm_final)


def kernel(x, p, w_in, gla_wg2, gla_bg, hg_lb_logits, gla_onorm, hg_onorm, w_out, norm_mix,
           norm_ffn, w_dense_gate, w_dense_up, w_dense_down, w_router, w_exp_gate, w_exp_up,
           w_exp_down, w_ple_proj, w_ple_gate, norm_ple, norm_final):
    bsz, seq, _ = x.shape
    t = bsz * seq
    w_proj = _proj_weights(w_in)
    w_out_b, w_pp, w_pg = w_out.astype(BF16), w_ple_proj.astype(BF16), w_ple_gate.astype(BF16)
    w_dg, w_du, w_dd = (w.astype(BF16) for w in (w_dense_gate, w_dense_up, w_dense_down))
    p_all = p.reshape(DEPTH, t, PLE_DIM)
    h = x.reshape(t, D_MODEL)
    for i in range(DEPTH):
        final = i == DEPTH - 1
        q, kf, kb, v, gates, gfh, gfl, gbh, gbl = _proj(
            i, h, norm_mix[i], w_proj, gla_wg2[i], gla_bg[i], hg_lb_logits)
        o_f, o_b = _scan(i, bsz, q, kf, kb, v, gfh, gfl, gbh, gbl)
        post = _post_inputs(i, o_f.reshape(t, D_MODEL), o_b.reshape(t, D_MODEL), gates,
                            gla_onorm[i], hg_onorm[i], w_out_b)
        if i % 2 == 0:
            h = _dense_ffn(i, final, post, h, p_all, norm_ffn[i], w_dg, w_du, w_dd, norm_ple[i],
                           w_pp, w_pg, norm_final)
        else:
            h = _moe(i, final, post, h, p_all, norm_ffn[i], w_router[i // 2], w_exp_gate, w_exp_up,
                     w_exp_down, norm_ple[i], w_pp, w_pg, norm_final)
    return h.reshape(bsz, seq, D_MODEL)
```

```python
import functools

import jax
import jax.numpy as jnp
from jax import lax
from jax.experimental import pallas as pl
from jax.experimental.pallas import tpu as pltpu
from jax.experimental.pallas import tpu_sc as plsc

F32 = jnp.float32
BF16 = jnp.bfloat16

D_MODEL = 1024
DEPTH = 4
N_HEADS = 8
HEAD_W = 128
GLA_DK = 64
GLA_RANK = 16
GLA_GATE_NORM = 16.0
HG_DK = 128
HG_K = 512
D_FF_DENSE = 2816
N_EXPERTS = 8
D_FF_EXPERT = 3584
PLE_DIM = 256
EPS = 1e-6
HEAD_EPS = 1e-5
F_MIN = 1e-6

CHUNK = 64
SUB = 16
EXP2_CLAMP = 86.0
LANES = 128
VMEM_LIMIT = 56 * 1024 * 1024

MIX_TILE = 512
ROW_TILE = 512
FF_CHUNK = 256
EXP_FF_CHUNK = 512

COL_HQ = 0
COL_HF = 512
COL_GLR = 1536
COL_GQ = 1664
COL_GK = 1920
COL_V = 2176
COL_GATE = 3200
NP_COLS = 4224
GLA_QK = 4 * GLA_DK
LOG2E = 1.4426950408889634


def _dot(a, b):
    return jnp.dot(a, b, preferred_element_type=F32)


def _dot_nt(a, b):
    return lax.dot_general(a, b, (((1,), (1,)), ((), ())), preferred_element_type=F32)


def _dot_tn(a, b):
    return lax.dot_general(a, b, (((0,), (0,)), ((), ())), preferred_element_type=F32)


def _sigmoid(x):
    return 1.0 / (1.0 + jnp.exp(-x))


def _rmsnorm(x, w):
    ms = jnp.mean(x * x, axis=-1, keepdims=True)
    return x * lax.rsqrt(ms + EPS) * w


def _const_spec(shape):
    nd = len(shape)
    return pl.BlockSpec(shape, lambda *_: (0,) * nd, pipeline_mode=pl.Buffered(1))


def _layer_spec(shape, layer):
    nd = len(shape)
    return pl.BlockSpec((1,) + tuple(shape), lambda *_: (layer,) + (0,) * nd,
                        pipeline_mode=pl.Buffered(1))


def _params(n_grid):
    return pltpu.CompilerParams(
        dimension_semantics=("arbitrary",) * n_grid, vmem_limit_bytes=VMEM_LIMIT)


PROJ_CHUNK = 256


def _proj_kernel(layer, h_ref, nw_ref, w_ref, wg2_ref, bg_ref, lbl_ref, spread_ref,
                 q_ref, kf_ref, kb_ref, v_ref, gate_ref, gfh_ref, gfl_ref, gbh_ref, gbl_ref,
                 u_ref):
    u_ref[...] = _rmsnorm(h_ref[...], nw_ref[...]).astype(BF16)
    proj = lambda c0, width: _dot(u_ref[...], w_ref[0, :, c0:c0 + width])
    chunks = lambda width: range(0, width, PROJ_CHUNK)

    def split(hi_ref, lo_ref, c0, g):
        hi = g.astype(BF16)
        hi_ref[:, c0:c0 + PROJ_CHUNK] = hi
        lo_ref[:, c0:c0 + PROJ_CHUNK] = (g - hi.astype(F32)).astype(BF16)

    dirs = ((kf_ref, gfh_ref, gfl_ref), (kb_ref, gbh_ref, gbl_ref))
    heavy, light = [], []

    glr = proj(COL_GLR, LANES).astype(BF16)

    def gla_decay(direction, c):
        _, hi_ref, lo_ref = dirs[direction]
        x = _dot(glr, wg2_ref[direction, :, c:c + PROJ_CHUNK].astype(BF16))
        x = x + bg_ref[direction, :, c:c + PROJ_CHUNK]
        log_sig = jnp.minimum(x, 0.0) - jnp.log(1.0 + jnp.exp(-jnp.abs(x)))
        split(hi_ref, lo_ref, c, log_sig * (LOG2E / GLA_GATE_NORM))

    def hgrn_forget(direction, c):
        k_ref, hi_ref, lo_ref = dirs[direction]
        rows = [lbl_ref[2 * d + direction:2 * d + direction + 1, c:c + PROJ_CHUNK]
                for d in range(DEPTH)]
        mx = functools.reduce(jnp.maximum, rows)
        ex = [jnp.exp(r - mx) for r in rows]
        lb = sum(ex[1:layer + 1], jnp.zeros_like(mx)) / sum(ex)
        z = proj(COL_HF + direction * HG_K + c, PROJ_CHUNK)
        e = jnp.exp(-jnp.abs(z))
        s_big = 1.0 / (1.0 + e)
        s_small = e * s_big
        sig = jnp.where(z >= 0, s_big, s_small)
        nsig = jnp.where(z >= 0, s_small, s_big)
        f = lb + (1.0 - lb) * sig
        split(hi_ref, lo_ref, 512 + c, jnp.log(jnp.maximum(f, F_MIN)) * LOG2E)
        k_ref[:, 512 + c:512 + c + PROJ_CHUNK] = ((1.0 - lb) * nsig).astype(BF16)

    def hgrn_query(c):
        hq = proj(COL_HQ + c, PROJ_CHUNK)
        q_ref[:, 512 + c:512 + c + PROJ_CHUNK] = (hq * _sigmoid(hq) * (HG_DK ** -0.5)).astype(BF16)

    def gla_query():
        gq = (proj(COL_GQ, GLA_QK) * (GLA_DK ** -0.5)).astype(BF16)
        q_ref[:, 0:512] = _dot(gq, spread_ref[...]).astype(BF16)

    def gla_key():
        gk = _dot(proj(COL_GK, GLA_QK).astype(BF16), spread_ref[...]).astype(BF16)
        kf_ref[:, 0:512] = gk
        kb_ref[:, 0:512] = gk

    def plain(out_ref, col, c):
        out_ref[:, c:c + PROJ_CHUNK] = proj(col + c, PROJ_CHUNK).astype(BF16)

    for direction in range(2):
        heavy += [functools.partial(hgrn_forget, direction, c) for c in chunks(HG_K)]
        heavy += [functools.partial(gla_decay, direction, c) for c in chunks(512)]
    heavy += [functools.partial(hgrn_query, c) for c in chunks(512)]
    light += [gla_query, gla_key]
    light += [functools.partial(plain, v_ref, COL_V, c) for c in chunks(D_MODEL)]
    light += [functools.partial(plain, gate_ref, COL_GATE, c) for c in chunks(D_MODEL)]
    for i in range(max(len(heavy), len(light))):
        for chunk in heavy[i:i + 1] + light[i:i + 1]:
            chunk()


def _pad_heads(w, n_heads, width):
    lead = w.shape[:-1]
    w = w.reshape(lead + (n_heads, width))
    w = jnp.pad(w, [(0, 0)] * len(lead) + [(0, 0), (0, HEAD_W - width)])
    return w.reshape(lead + (n_heads * HEAD_W,))


def _proj_weights(w_in):
    gq, gk, gv, gr, glr_f, glr_b, hq, hf_f, hf_b, hv, hr = jnp.split(
        w_in, [256, 512, 1024, 1536, 1552, 1568, 2080, 2592, 3104, 3616], axis=-1)
    glr = jnp.pad(jnp.concatenate([glr_f, glr_b], -1), ((0, 0), (0, 0), (0, LANES - 2 * GLA_RANK)))
    return jnp.concatenate([hq, hf_f, hf_b, glr, gq, gk, gv, hv, gr, hr], -1).astype(BF16)


def _proj(layer, h, norm_w, w_all, wg2, bg, lb_logits):
    t = h.shape[0]
    wg2_h = _pad_heads(wg2, 4, GLA_DK)
    wg2_p = jnp.stack([jnp.pad(wg2_h[0], ((0, LANES - GLA_RANK), (0, 0))),
                       jnp.pad(wg2_h[1], ((GLA_RANK, LANES - 2 * GLA_RANK), (0, 0)))])
    bg_p = _pad_heads(bg, 4, GLA_DK).reshape(2, 1, 512)
    lbl = lb_logits.reshape(DEPTH * 2, HG_K)
    src = jnp.arange(GLA_QK)
    spread = (jnp.arange(512)[None, :] == (src // GLA_DK * HEAD_W + src % GLA_DK)[:, None])
    row = lambda: pl.BlockSpec((ROW_TILE, D_MODEL), lambda i: (i, 0))
    return pl.pallas_call(
        functools.partial(_proj_kernel, layer),
        grid=(t // ROW_TILE,),
        in_specs=[row(), _const_spec((1, D_MODEL)), _layer_spec((D_MODEL, NP_COLS), layer),
                  _const_spec((2, LANES, 512)), _const_spec((2, 1, 512)),
                  _const_spec((DEPTH * 2, HG_K)), _const_spec((GLA_QK, 512))],
        out_specs=[row() for _ in range(9)],
        out_shape=[jax.ShapeDtypeStruct((t, D_MODEL), BF16) for _ in range(9)],
        scratch_shapes=[pltpu.VMEM((ROW_TILE, D_MODEL), BF16)],
        compiler_params=_params(1),
        name=f"proj_l{layer}",
    )(h, norm_w.reshape(1, D_MODEL), w_all, wg2_p, bg_p, lbl, spread.astype(BF16))


A_GROUPS = 3
A_WIDTH = A_GROUPS * HEAD_W


def _stage_a(rev, q_ref, k_ref, gh_ref, gl_ref, r0, tri, bufs, slot):
    qa, ka, qd, kd, qi, ks, et, _, _ = bufs
    rows = slice(r0, r0 + CHUNK)
    cum = _dot(tri, jnp.concatenate([gh_ref[0, rows, :], gl_ref[0, rows, :]], axis=0))
    q = q_ref[0, rows, :]
    k = k_ref[0, rows, :]
    factor = lambda x: jnp.exp2(x).astype(BF16)

    def put(ref, rows, group, val):
        for n in range(N_HEADS):
            c0 = n * A_WIDTH + group * HEAD_W
            ref[slot, rows, c0:c0 + HEAD_W] = val[:, n * HEAD_W:(n + 1) * HEAD_W]

    for group, (half, a) in enumerate(((2 * SUB, 0), (SUB, 0), (SUB, 2 * SUB))):
        lo, hi = slice(a, a + half), slice(a + half, a + 2 * half)
        if not rev:
            ref, k_rows, q_rows = cum[a + half - 1:a + half], lo, hi
        else:
            ref, q_rows, k_rows = cum[a + half:a + half + 1], lo, hi
        put(qa, q_rows, group, q[q_rows] * factor(cum[q_rows] - ref))
        put(ka, k_rows, group, k[k_rows] * factor(ref - cum[k_rows]))

    mids = []
    for a in range(0, CHUNK, SUB):
        m = 0.5 * (cum[a:a + 1] + cum[a + SUB - 1:a + SUB])
        mids.append(jnp.broadcast_to(m, (SUB, D_MODEL)))
    dd = cum - jnp.concatenate(mids, 0)
    qd[slot] = q * factor(jnp.clip(dd, -EXP2_CLAMP, EXP2_CLAMP))
    kd[slot] = k * factor(jnp.clip(-dd, -EXP2_CLAMP, EXP2_CLAMP))

    tot = cum[0:1] if rev else cum[CHUNK - 1:CHUNK]
    qi[slot] = q * factor(cum)
    ks[slot] = k * factor(tot - cum)
    et[slot] = jnp.exp2(tot)


def _stage_b(bufs, slot, v_ref, st_ref, r0, diag):
    qa, ka, qd, kd, qi, ks, et, ab, oi = bufs
    for n in range(N_HEADS):
        wide = slice(n * A_WIDTH, (n + 1) * A_WIDTH)
        head = slice(n * HEAD_W, (n + 1) * HEAD_W)
        scores = _dot_nt(qa[slot, :, wide], ka[slot, :, wide])
        scores = scores + jnp.where(diag, _dot_nt(qd[slot, :, head], kd[slot, :, head]), 0.0)
        ab[slot, n] = scores.astype(BF16)
        st = st_ref[n]
        oi[slot, :, head] = _dot(qi[slot, :, head], st.astype(BF16))
        decay = jnp.transpose(jnp.broadcast_to(et[slot, :, head], (HEAD_W, HEAD_W)))
        st_ref[n] = st * decay + _dot_tn(ks[slot, :, head], v_ref[0, r0:r0 + CHUNK, head])


def _stage_c(bufs, slot, v_ref, o_ref, r0):
    ab, oi = bufs[7], bufs[8]
    for n in range(N_HEADS):
        head = slice(n * HEAD_W, (n + 1) * HEAD_W)
        o = oi[slot, :, head] + _dot(ab[slot, n], v_ref[0, r0:r0 + CHUNK, head])
        o_ref[0, r0:r0 + CHUNK, head] = o.astype(o_ref.dtype)


def _scan_kernel(qf_ref, qb_ref, kf_ref, kb_ref, vf_ref, vb_ref, gfh_ref, gfl_ref, gbh_ref,
                 gbl_ref, of_ref, ob_ref, sf_ref, sb_ref, *buf_refs):
    n_chunks = MIX_TILE // CHUNK
    bufs_f, bufs_b = buf_refs[:len(buf_refs) // 2], buf_refs[len(buf_refs) // 2:]

    @pl.when(pl.program_id(1) == 0)
    def _():
        sf_ref[...] = jnp.zeros_like(sf_ref)
        sb_ref[...] = jnp.zeros_like(sb_ref)
        for bufs in (bufs_f, bufs_b):
            bufs[0][...] = jnp.zeros_like(bufs[0])
            bufs[1][...] = jnp.zeros_like(bufs[1])

    ri = lax.broadcasted_iota(jnp.int32, (CHUNK, CHUNK), 0)
    ci = lax.broadcasted_iota(jnp.int32, (CHUNK, CHUNK), 1)
    tri_f = (ci <= ri).astype(BF16)
    tri_b = (ci >= ri).astype(BF16)
    tri_f = jnp.concatenate([tri_f, tri_f], axis=1)
    tri_b = jnp.concatenate([tri_b, tri_b], axis=1)
    same16 = (ri // SUB) == (ci // SUB)
    diag_f = same16 & (ci <= ri)
    diag_b = same16 & (ci >= ri)
    row_f = lambda c: c * CHUNK
    row_b = lambda c: (n_chunks - 1 - c) * CHUNK

    def stage_a(c):
        _stage_a(False, qf_ref, kf_ref, gfh_ref, gfl_ref, row_f(c), tri_f, bufs_f, c % 2)
        _stage_a(True, qb_ref, kb_ref, gbh_ref, gbl_ref, row_b(c), tri_b, bufs_b, c % 2)

    stage_a(0)
    for c in range(n_chunks):
        if c + 1 < n_chunks:
            stage_a(c + 1)
        _stage_b(bufs_f, c % 2, vf_ref, sf_ref, row_f(c), diag_f)
        _stage_b(bufs_b, c % 2, vb_ref, sb_ref, row_b(c), diag_b)
        _stage_c(bufs_f, c % 2, vf_ref, of_ref, row_f(c))
        _stage_c(bufs_b, c % 2, vb_ref, ob_ref, row_b(c))


def _scan_bufs():
    wide = pltpu.VMEM((2, CHUNK, N_HEADS * A_WIDTH), BF16)
    narrow = pltpu.VMEM((2, CHUNK, D_MODEL), BF16)
    return [wide, wide, narrow, narrow, narrow, narrow,
            pltpu.VMEM((2, 1, D_MODEL), F32),
            pltpu.VMEM((2, N_HEADS, CHUNK, CHUNK), BF16),
            pltpu.VMEM((2, CHUNK, D_MODEL), F32)]


def _scan(layer, bsz, q, kf, kb, v, gfh, gfl, gbh, gbl):
    seq = q.shape[0] // bsz
    nt = seq // MIX_TILE
    to3 = lambda a: a.reshape(bsz, seq, D_MODEL)
    tile = lambda idx: pl.BlockSpec((1, MIX_TILE, D_MODEL), idx)
    fwd = lambda b, j: (b, j, 0)
    bwd = lambda b, j: (b, nt - 1 - j, 0)
    state = pltpu.VMEM((N_HEADS, HEAD_W, HEAD_W), F32)
    return pl.pallas_call(
        _scan_kernel,
        grid=(bsz, nt),
        in_specs=[tile(fwd), tile(bwd), tile(fwd), tile(bwd), tile(fwd), tile(bwd),
                  tile(fwd), tile(fwd), tile(bwd), tile(bwd)],
        out_specs=[tile(fwd), tile(bwd)],
        out_shape=[jax.ShapeDtypeStruct((bsz, seq, D_MODEL), BF16),
                   jax.ShapeDtypeStruct((bsz, seq, D_MODEL), BF16)],
        scratch_shapes=[state, state] + _scan_bufs() + _scan_bufs(),
        compiler_params=_params(2),
        name=f"scan_l{layer}",
    )(to3(q), to3(q), to3(kf), to3(kb), to3(v), to3(v), to3(gfh), to3(gfl), to3(gbh), to3(gbl))


def _post_math(of_ref, ob_ref, gate_ref, ones_ref, onw_ref, wout_ref, h):
    o = (of_ref[...] + ob_ref[...]).astype(F32)
    sq = (o * o).astype(BF16)
    ms = jnp.concatenate([_dot(sq[:, n * HEAD_W:(n + 1) * HEAD_W], ones_ref[...])
                          for n in range(N_HEADS)], axis=1) * (1.0 / HEAD_W)
    y = o * lax.rsqrt(ms + HEAD_EPS) * onw_ref[...]
    g = gate_ref[...].astype(F32)
    y = y * (g * _sigmoid(g))
    return h + _dot(y.astype(BF16), wout_ref[0])


def _post_inputs(layer, o_f, o_b, gates, gla_onorm, hg_onorm, w_out_all):
    onw = jnp.concatenate([jnp.tile(gla_onorm, 4), jnp.tile(hg_onorm, 4)]).reshape(1, D_MODEL)
    row = lambda: pl.BlockSpec((ROW_TILE, D_MODEL), lambda i: (i, 0))
    specs = [row(), row(), row(), _const_spec((HEAD_W, HEAD_W)), _const_spec((1, D_MODEL)),
             _layer_spec((D_MODEL, D_MODEL), layer)]
    return (o_f, o_b, gates, jnp.ones((HEAD_W, HEAD_W), BF16), onw, w_out_all), specs


def _ple_tail(h2, p, nple, wpp_ref, wpg_ref, nfinal, final):
    gate = _sigmoid(_dot(_rmsnorm(h2, nple).astype(BF16), wpg_ref[0]))
    h3 = h2 + _dot(p.astype(BF16), wpp_ref[0]) * gate
    if final:
        h3 = _rmsnorm(h3, nfinal)
    return h3


def _dense_kernel(final, of_ref, ob_ref, gate_ref, ones_ref, onw_ref, wout_ref, h_ref, p_ref,
                  nffn_ref, wg_ref, wu_ref, wd_ref, nple_ref, wpp_ref, wpg_ref, nfin_ref, out_ref):
    h1 = _post_math(of_ref, ob_ref, gate_ref, ones_ref, onw_ref, wout_ref, h_ref[...])
    v = _rmsnorm(h1, nffn_ref[...]).astype(BF16)
    acc = jnp.zeros_like(h1)
    for c in range(0, D_FF_DENSE, FF_CHUNK):
        a = _dot(v, wg_ref[0, :, c:c + FF_CHUNK])
        b = _dot(v, wu_ref[0, :, c:c + FF_CHUNK])
        acc = acc + _dot((a * _sigmoid(a) * b).astype(BF16), wd_ref[0, c:c + FF_CHUNK, :])
    out_ref[...] = _ple_tail(h1 + acc, p_ref[0], nple_ref[...], wpp_ref, wpg_ref,
                             nfin_ref[...], final)


def _dense_ffn(layer, final, post, h, p, norm_ffn, wg, wu, wd, norm_ple, w_pp, w_pg, norm_final):
    t = h.shape[0]
    vec = lambda w: w.reshape(1, D_MODEL)
    j = layer // 2
    return pl.pallas_call(
        functools.partial(_dense_kernel, final),
        grid=(t // ROW_TILE,),
        in_specs=post[1] + [
            pl.BlockSpec((ROW_TILE, D_MODEL), lambda i: (i, 0)),
            pl.BlockSpec((1, ROW_TILE, PLE_DIM), lambda i: (layer, i, 0)),
            _const_spec((1, D_MODEL)),
            _layer_spec((D_MODEL, D_FF_DENSE), j), _layer_spec((D_MODEL, D_FF_DENSE), j),
            _layer_spec((D_FF_DENSE, D_MODEL), j),
            _const_spec((1, D_MODEL)),
            _layer_spec((PLE_DIM, D_MODEL), layer), _layer_spec((D_MODEL, D_MODEL), layer),
            _const_spec((1, D_MODEL))],
        out_specs=pl.BlockSpec((ROW_TILE, D_MODEL), lambda i: (i, 0)),
        out_shape=jax.ShapeDtypeStruct((t, D_MODEL), F32),
        compiler_params=_params(1),
        name=f"dense_l{layer}",
    )(*post[0], h, p, vec(norm_ffn), wg, wu, wd, vec(norm_ple), w_pp, w_pg, vec(norm_final))


META_G1, META_G2, META_E1, META_E2, META_R1, META_R2 = range(6)
HALF = D_MODEL // 2
HI_MASK = 0xFFFF0000


def _pack_rows(x):
    bits = pltpu.bitcast(x.astype(BF16).astype(F32), jnp.uint32)
    return (bits[:, :HALF] >> 16) | (bits[:, HALF:] & jnp.uint32(HI_MASK))


def _unpack_rows(w):
    lo = pltpu.bitcast(w << 16, F32)
    hi = pltpu.bitcast(w & jnp.uint32(HI_MASK), F32)
    return jnp.concatenate([lo, hi], axis=1)


def _router_kernel(of_ref, ob_ref, gate_ref, ones_ref, onw_ref, wout_ref, h_ref, nffn_ref, wr_ref,
                   tri_ref, h1_ref, xp_ref, meta_ref, cnt_ref, base_ref):
    @pl.when(pl.program_id(0) == 0)
    def _():
        base_ref[...] = jnp.zeros_like(base_ref)

    h1 = _post_math(of_ref, ob_ref, gate_ref, ones_ref, onw_ref, wout_ref, h_ref[...])
    h1_ref[...] = h1
    vf = _rmsnorm(h1, nffn_ref[...])
    xp_ref[...] = _pack_rows(vf)
    v_hi = vf.astype(BF16)
    v_lo = (vf - v_hi.astype(F32)).astype(BF16)
    w_hi, w_lo = wr_ref[0], wr_ref[1]
    logits = _dot(v_hi, w_hi) + _dot(v_lo, w_hi) + _dot(v_hi, w_lo)
    lane = lax.broadcasted_iota(jnp.int32, logits.shape, 1)
    neg = jnp.float32(-jnp.inf)
    lg = jnp.where(lane < N_EXPERTS, logits, neg)
    m1 = jnp.max(lg, axis=-1, keepdims=True)
    i1 = jnp.min(jnp.where(lg == m1, lane, LANES), axis=-1, keepdims=True)
    first = lane == i1
    lg2 = jnp.where(first, neg, lg)
    m2 = jnp.max(lg2, axis=-1, keepdims=True)
    i2 = jnp.min(jnp.where(lg2 == m2, lane, LANES), axis=-1, keepdims=True)
    second = lane == i2
    e = jnp.exp(m2 - m1)
    g1 = 1.0 / (1.0 + e)
    g2 = e * g1
    cnt = jnp.where(first | second, 1.0, 0.0)
    rank = base_ref[...] + _dot(tri_ref[...], cnt.astype(BF16))
    r1 = jnp.sum(jnp.where(first, rank, 0.0), axis=-1, keepdims=True)
    r2 = jnp.sum(jnp.where(second, rank, 0.0), axis=-1, keepdims=True)
    base_ref[...] = base_ref[...] + jnp.sum(cnt, axis=0, keepdims=True)
    cnt_ref[...] = base_ref[...]
    meta = jnp.zeros(logits.shape, F32)
    for col, val in ((META_G1, g1), (META_G2, g2), (META_E1, i1.astype(F32)),
                     (META_E2, i2.astype(F32)), (META_R1, r1), (META_R2, r2)):
        meta = jnp.where(lane == col, val, meta)
    meta_ref[...] = meta


def _router(layer, post, h, norm_ffn, w_router):
    t = h.shape[0]
    wr = jnp.pad(w_router, ((0, 0), (0, LANES - N_EXPERTS)))
    wr_hi = wr.astype(BF16)
    wr = jnp.stack([wr_hi, (wr - wr_hi.astype(F32)).astype(BF16)])
    ids = jnp.arange(ROW_TILE)
    tri = (ids[None, :] < ids[:, None]).astype(BF16)
    return pl.pallas_call(
        _router_kernel,
        grid=(t // ROW_TILE,),
        in_specs=post[1] + [
            pl.BlockSpec((ROW_TILE, D_MODEL), lambda i: (i, 0)),
            _const_spec((1, D_MODEL)), _const_spec((2, D_MODEL, LANES)),
            _const_spec((ROW_TILE, ROW_TILE))],
        out_specs=[pl.BlockSpec((ROW_TILE, D_MODEL), lambda i: (i, 0)),
                   pl.BlockSpec((ROW_TILE, HALF), lambda i: (i, 0)),
                   pl.BlockSpec((ROW_TILE, LANES), lambda i: (i, 0)),
                   pl.BlockSpec((1, LANES), lambda i: (0, 0))],
        out_shape=[jax.ShapeDtypeStruct((t, D_MODEL), F32),
                   jax.ShapeDtypeStruct((t, HALF), jnp.uint32),
                   jax.ShapeDtypeStruct((t, LANES), F32),
                   jax.ShapeDtypeStruct((1, LANES), F32)],
        scratch_shapes=[pltpu.VMEM((1, LANES), F32)],
        compiler_params=_params(1),
        name=f"router_l{layer}",
    )(*post[0], h, norm_ffn.reshape(1, D_MODEL), wr, tri)


SC_CORES, SC_SUBCORES = 2, 16
SC_WORKERS = SC_CORES * SC_SUBCORES
SC_WINDOW = 64


def _sc_gather(table, idx):
    n_out, d = idx.shape[0], table.shape[1]
    per_worker = n_out // SC_WORKERS
    n_win = per_worker // SC_WINDOW
    assert n_win * SC_WINDOW * SC_WORKERS == n_out and n_win % 2 == 0
    mesh = plsc.VectorSubcoreMesh(core_axis_name="c", subcore_axis_name="s")

    def body(table_hbm, idx_hbm, out_hbm, idx_a, idx_b, rows_a, rows_b, sem_a, sem_b):
        wid = lax.axis_index("s") * SC_CORES + lax.axis_index("c")

        def rows_of(w):
            return pl.ds(pl.multiple_of(wid * per_worker + w * SC_WINDOW, SC_WINDOW), SC_WINDOW)

        def start(w, idx_v, rows_v, sem):
            pltpu.sync_copy(idx_hbm.at[rows_of(w)], idx_v)
            pltpu.async_copy(table_hbm.at[idx_v], rows_v, sem)

        def finish(w, idx_v, rows_v, sem):
            pltpu.make_async_copy(table_hbm.at[idx_v], rows_v, sem).wait()
            pltpu.sync_copy(rows_v, out_hbm.at[rows_of(w)])

        start(0, idx_a, rows_a, sem_a)

        @pl.loop(0, n_win, step=2)
        def _(w):
            start(w + 1, idx_b, rows_b, sem_b)
            finish(w, idx_a, rows_a, sem_a)

            @pl.when(w + 2 < n_win)
            def _():
                start(w + 2, idx_a, rows_a, sem_a)

            finish(w + 1, idx_b, rows_b, sem_b)

    return pl.kernel(
        body, mesh=mesh,
        out_type=jax.ShapeDtypeStruct((n_out, d), table.dtype),
        scratch_types=[pltpu.VMEM((SC_WINDOW,), jnp.int32), pltpu.VMEM((SC_WINDOW,), jnp.int32),
                       pltpu.VMEM((SC_WINDOW, d), table.dtype),
                       pltpu.VMEM((SC_WINDOW, d), table.dtype),
                       pltpu.SemaphoreType.DMA, pltpu.SemaphoreType.DMA],
    )(table, idx)


def _sc_scatter2(x, pos_a, pos_b, n_out):
    t, d = x.shape
    per_worker = t // SC_WORKERS
    n_win = per_worker // SC_WINDOW
    assert n_win * SC_WINDOW * SC_WORKERS == t and n_win % 2 == 0
    mesh = plsc.VectorSubcoreMesh(core_axis_name="c", subcore_axis_name="s")

    def body(x_hbm, pa_hbm, pb_hbm, out_hbm, ia0, ib0, ia1, ib1, rows0, rows1, sem0, sem1):
        wid = lax.axis_index("s") * SC_CORES + lax.axis_index("c")

        def rows_of(w):
            return pl.ds(pl.multiple_of(wid * per_worker + w * SC_WINDOW, SC_WINDOW), SC_WINDOW)

        def start(w, ia, ib, rows_v, sem):
            pltpu.sync_copy(x_hbm.at[rows_of(w)], rows_v)
            pltpu.sync_copy(pa_hbm.at[rows_of(w)], ia)
            pltpu.sync_copy(pb_hbm.at[rows_of(w)], ib)
            pltpu.async_copy(rows_v, out_hbm.at[ia], sem)
            pltpu.async_copy(rows_v, out_hbm.at[ib], sem)

        def finish(ia, ib, rows_v, sem):
            pltpu.make_async_copy(rows_v, out_hbm.at[ia], sem).wait()
            pltpu.make_async_copy(rows_v, out_hbm.at[ib], sem).wait()

        start(0, ia0, ib0, rows0, sem0)

        @pl.loop(0, n_win, step=2)
        def _(w):
            start(w + 1, ia1, ib1, rows1, sem1)
            finish(ia0, ib0, rows0, sem0)

            @pl.when(w + 2 < n_win)
            def _():
                start(w + 2, ia0, ib0, rows0, sem0)

            finish(ia1, ib1, rows1, sem1)

    idx = lambda: pltpu.VMEM((SC_WINDOW,), jnp.int32)
    return pl.kernel(
        body, mesh=mesh,
        out_type=jax.ShapeDtypeStruct((n_out, d), x.dtype),
        scratch_types=[idx(), idx(), idx(), idx(),
                       pltpu.VMEM((SC_WINDOW, d), x.dtype), pltpu.VMEM((SC_WINDOW, d), x.dtype),
                       pltpu.SemaphoreType.DMA, pltpu.SemaphoreType.DMA],
    )(x, pos_a, pos_b)


EXP_ROWS = 1024


def _experts_kernel(be_ref, na_ref, nv_ref, xs_ref, wg_ref, wu_ref, wd_ref, ys_ref, x_ref,
                    acc_ref):
    b = pl.program_id(0)
    f = pl.program_id(1)

    @pl.when(b < na_ref[0])
    def _():
        @pl.when(f == 0)
        def _():
            row = lax.broadcasted_iota(jnp.int32, (EXP_ROWS, D_MODEL), 0)
            x = jnp.where(row < nv_ref[b], _unpack_rows(xs_ref[...]), 0.0)
            x_ref[...] = x.astype(BF16)
            acc_ref[...] = jnp.zeros_like(acc_ref)

        x = x_ref[...]
        halves = [slice(c, c + EXP_FF_CHUNK // 2) for c in (0, EXP_FF_CHUNK // 2)]
        gate_up = [(_dot(x, wg_ref[0, 0, :, cols].astype(BF16)),
                    _dot(x, wu_ref[0, 0, :, cols].astype(BF16))) for cols in halves]
        down = acc_ref[...]
        for (a, u), cols in zip(gate_up, halves):
            down = down + _dot((a * _sigmoid(a) * u).astype(BF16),
                               wd_ref[0, 0, cols, :].astype(BF16))
        acc_ref[...] = down

        @pl.when(f == pl.num_programs(1) - 1)
        def _():
            ys_ref[...] = _pack_rows(acc_ref[...])


def _experts(layer, xs, blk_expert, n_active, blk_valid, wg, wu, wd):
    moe_idx = layer // 2
    rows = xs.shape[0]
    nf = D_FF_EXPERT // EXP_FF_CHUNK
    fsel = lambda b, f, na: jnp.where(b < na[0], f, nf - 1)
    grid_spec = pltpu.PrefetchScalarGridSpec(
        num_scalar_prefetch=3,
        grid=(rows // EXP_ROWS, nf),
        in_specs=[pl.BlockSpec((EXP_ROWS, HALF), lambda b, f, be, na, nv: (b, 0)),
                  pl.BlockSpec((1, 1, D_MODEL, EXP_FF_CHUNK),
                               lambda b, f, be, na, nv: (moe_idx, be[b], 0, fsel(b, f, na))),
                  pl.BlockSpec((1, 1, D_MODEL, EXP_FF_CHUNK),
                               lambda b, f, be, na, nv: (moe_idx, be[b], 0, fsel(b, f, na))),
                  pl.BlockSpec((1, 1, EXP_FF_CHUNK, D_MODEL),
                               lambda b, f, be, na, nv: (moe_idx, be[b], fsel(b, f, na), 0))],
        out_specs=pl.BlockSpec((EXP_ROWS, HALF), lambda b, f, be, na, nv: (b, 0)),
        scratch_shapes=[pltpu.VMEM((EXP_ROWS, D_MODEL), BF16),
                        pltpu.VMEM((EXP_ROWS, D_MODEL), F32)])
    return pl.pallas_call(
        _experts_kernel,
        grid_spec=grid_spec,
        out_shape=jax.ShapeDtypeStruct((rows, HALF), jnp.uint32),
        compiler_params=_params(2),
        name=f"experts_l{layer}",
    )(blk_expert, n_active, blk_valid, xs, wg, wu, wd)


def _combine_kernel(final, h_ref, y1_ref, y2_ref, meta_ref, p_ref, nple_ref, wpp_ref, wpg_ref,
                    nfin_ref, out_ref):
    meta = meta_ref[...]
    g1 = meta[:, META_G1:META_G1 + 1]
    g2 = meta[:, META_G2:META_G2 + 1]
    h2 = h_ref[...] + g1 * _unpack_rows(y1_ref[...]) + g2 * _unpack_rows(y2_ref[...])
    out_ref[...] = _ple_tail(h2, p_ref[0], nple_ref[...], wpp_ref, wpg_ref, nfin_ref[...], final)


def _combine(layer, final, h, yg, meta, p, norm_ple, w_pp, w_pg, norm_final):
    t = h.shape[0]
    nt = t // ROW_TILE
    vec = lambda w: w.reshape(1, D_MODEL)
    return pl.pallas_call(
        functools.partial(_combine_kernel, final),
        grid=(nt,),
        in_specs=[pl.BlockSpec((ROW_TILE, D_MODEL), lambda i: (i, 0)),
                  pl.BlockSpec((ROW_TILE, HALF), lambda i: (i, 0)),
                  pl.BlockSpec((ROW_TILE, HALF), lambda i: (i + nt, 0)),
                  pl.BlockSpec((ROW_TILE, LANES), lambda i: (i, 0)),
                  pl.BlockSpec((1, ROW_TILE, PLE_DIM), lambda i: (layer, i, 0)),
                  _const_spec((1, D_MODEL)),
                  _layer_spec((PLE_DIM, D_MODEL), layer), _layer_spec((D_MODEL, D_MODEL), layer),
                  _const_spec((1, D_MODEL))],
        out_specs=pl.BlockSpec((ROW_TILE, D_MODEL), lambda i: (i, 0)),
        out_shape=jax.ShapeDtypeStruct((t, D_MODEL), F32),
        compiler_params=_params(1),
        name=f"combine_l{layer}",
    )(h, yg, yg, meta, p, vec(norm_ple), w_pp, w_pg, vec(norm_final))


def _moe(layer, final, post, h, p, norm_ffn, w_router, wg, wu, wd, norm_ple, w_pp, w_pg,
         norm_final):
    t = h.shape[0]
    h1, xp, meta, counts = _router(layer, post, h, norm_ffn, w_router)
    cnt = counts[0, :N_EXPERTS].astype(jnp.int32)
    padded = ((cnt + EXP_ROWS - 1) // EXP_ROWS) * EXP_ROWS
    ends = jnp.cumsum(padded)
    offs = ends - padded
    e1 = meta[:, META_E1].astype(jnp.int32)
    e2 = meta[:, META_E2].astype(jnp.int32)
    pos1 = offs[e1] + meta[:, META_R1].astype(jnp.int32)
    pos2 = offs[e2] + meta[:, META_R2].astype(jnp.int32)
    rows = 2 * t + N_EXPERTS * EXP_ROWS
    blk_start = jnp.arange(rows // EXP_ROWS, dtype=jnp.int32) * EXP_ROWS
    blk_expert = jnp.minimum(
        jnp.sum(blk_start[:, None] >= ends[None, :], axis=1), N_EXPERTS - 1).astype(jnp.int32)
    n_active = (ends[-1:] // EXP_ROWS).astype(jnp.int32)
    blk_valid = jnp.clip((offs + cnt)[blk_expert] - blk_start, 0, EXP_ROWS).astype(jnp.int32)

    xs = _sc_scatter2(xp, pos1, pos2, rows)
    ys = _experts(layer, xs, blk_expert, n_active, blk_valid, wg, wu, wd)
    yg = _sc_gather(ys, jnp.concatenate([pos1, pos2]))
    return _combine(layer, final, h1, yg, meta, p, norm_ple, w_pp, w_pg, norm_final)


def kernel(x, p, w_in, gla_wg2, gla_bg, hg_lb_logits, gla_onorm, hg_onorm, w_out, norm_mix,
           norm_ffn, w_dense_gate, w_dense_up, w_dense_down, w_router, w_exp_gate, w_exp_up,
           w_exp_down, w_ple_proj, w_ple_gate, norm_ple, norm_final):
    bsz, seq, _ = x.shape
    t = bsz * seq
    w_proj = _proj_weights(w_in)
    w_out_b, w_pp, w_pg = w_out.astype(BF16), w_ple_proj.astype(BF16), w_ple_gate.astype(BF16)
    w_dg, w_du, w_dd = (w.astype(BF16) for w in (w_dense_gate, w_dense_up, w_dense_down))
    p_all = p.reshape(DEPTH, t, PLE_DIM)
    h = x.reshape(t, D_MODEL)
    for i in range(DEPTH):
        final = i == DEPTH - 1
        q, kf, kb, v, gates, gfh, gfl, gbh, gbl = _proj(
            i, h, norm_mix[i], w_proj, gla_wg2[i], gla_bg[i], hg_lb_logits)
        o_f, o_b = _scan(i, bsz, q, kf, kb, v, gfh, gfl, gbh, gbl)
        post = _post_inputs(i, o_f.reshape(t, D_MODEL), o_b.reshape(t, D_MODEL), gates,
                            gla_onorm[i], hg_onorm[i], w_out_b)
        if i % 2 == 0:
            h = _dense_ffn(i, final, post, h, p_all, norm_ffn[i], w_dg, w_du, w_dd, norm_ple[i],
                           w_pp, w_pg, norm_final)
        else:
            h = _moe(i, final, post, h, p_all, norm_ffn[i], w_router[i // 2], w_exp_gate, w_exp_up,
                     w_exp_down, norm_ple[i], w_pp, w_pg, norm_final)
    return h.reshape(bsz, seq, D_MODEL)
```
